```python
import math
import jax, jax.numpy as jnp
from jax import lax
import numpy as np

D_MODEL = 4096
BATCH = 2
SEQ = 4096
DEPTH = 4

HEAD_DIM = 128
N_HEADS_SB = D_MODEL // (2 * HEAD_DIM)
N_HEADS_DIL = D_MODEL // (2 * HEAD_DIM)
DIL_PATTERNS = ((128, 1), (512, 4), (2048, 16))
N_HEADS_MLA = D_MODEL // HEAD_DIM
Q_LORA = D_MODEL // 4
KV_LORA = 512
QK_NOPE = 128
QK_ROPE = 64
V_HEAD = 128
ROPE_THETA = 10000.0
EPS = 1e-6
Q_BLOCK = 128
ADA_SCALE = 0.5
N_EVEN = (DEPTH + 1) // 2
N_ODD = DEPTH // 2
W_SB = N_HEADS_SB * HEAD_DIM
W_DIL = N_HEADS_DIL * HEAD_DIM
EVEN_IN = 4 * W_SB + 4 * W_DIL
QK_HEAD = QK_NOPE + QK_ROPE
MLA_WIDTH = N_HEADS_MLA * V_HEAD
ODD_IN = Q_LORA + KV_LORA + QK_ROPE + MLA_WIDTH

kernel_name = "hybrid_stickbreak_dilated_mla_trunk"


def rms_norm(x, w):
    xf = x.astype(jnp.float32)
    y = xf * lax.rsqrt(jnp.mean(xf * xf, axis=-1, keepdims=True) + EPS)
    return (y * w.astype(jnp.float32)).astype(x.dtype)


def rope_tables(positions, dim):
    inv = ROPE_THETA ** (-jnp.arange(0, dim, 2, dtype=jnp.float32) / dim)
    ang = positions.astype(jnp.float32)[..., None] * inv
    return jnp.cos(ang)[:, :, None, :], jnp.sin(ang)[:, :, None, :]


def apply_rope(x, cos, sin):
    x1, x2 = jnp.split(x.astype(jnp.float32), 2, axis=-1)
    out = jnp.concatenate([x1 * cos - x2 * sin, x2 * cos + x1 * sin], axis=-1)
    return out.astype(x.dtype)


def modulation(c, w, b):
    mod = jax.nn.silu(c) @ w + b
    shift, scale, gate = jnp.split(mod, 3, axis=-1)
    return shift[:, None], scale[:, None], gate[:, None]


def _to_blocks(x):
    b, s, h, d = x.shape
    return x.reshape(b, s // Q_BLOCK, Q_BLOCK, h, d).transpose(1, 0, 2, 3, 4)


def _from_blocks(y):
    nb, b, q, h, d = y.shape
    return y.transpose(1, 0, 2, 3, 4).reshape(b, nb * q, h, d)


def stick_breaking_attention(q, k, v):
    b, s, h, d = q.shape
    scale = 1.0 / math.sqrt(d)
    key_pos = jnp.arange(s)

    def one_block(args):
        qb, q0 = args
        z = jnp.einsum('bqhd,bkhd->bhqk', qb, k,
                       preferred_element_type=jnp.float32) * scale
        qpos = q0 + jnp.arange(Q_BLOCK)
        before = key_pos[None, :] < qpos[:, None]
        log_keep = jnp.where(before, jax.nn.log_sigmoid(-z), 0.0)
        excl = lax.cumsum(log_keep, axis=3, reverse=True) - log_keep
        w = jnp.where(before, jnp.exp(jax.nn.log_sigmoid(z) + excl), 0.0)
        return jnp.einsum('bhqk,bkhd->bqhd', w.astype(v.dtype), v)

    starts = jnp.arange(s // Q_BLOCK) * Q_BLOCK
    return _from_blocks(lax.map(one_block, (_to_blocks(q), starts)))


def _strided_band_attention(q, k, v, dil, span, scale):
    b, s, h, d = q.shape
    unit = dil * Q_BLOCK
    s_pad = -(-s // unit) * unit
    nb = s_pad // unit

    def to_res(x):
        x = jnp.pad(x, ((0, 0), (0, s_pad - s), (0, 0), (0, 0)))
        x = x.reshape(b, s_pad // dil, dil, h, d).transpose(0, 2, 1, 3, 4)
        return x.reshape(b * dil, nb, Q_BLOCK, h, d)

    def with_prev(x):
        prev = jnp.pad(x[:, :-1], ((0, 0), (1, 0), (0, 0), (0, 0), (0, 0)))
        return jnp.concatenate([prev, x], axis=2)

    def from_res(x):
        tail = x.shape[3:]
        x = x.reshape((b, dil, s_pad // dil) + tail)
        x = jnp.moveaxis(x, 1, 2)
        return x.reshape((b, s_pad) + tail)[:, :s]

    qr = to_res(q)
    kb = with_prev(to_res(k))
    vb = with_prev(to_res(v))
    scores = jnp.einsum('nbqhd,nbkhd->nbqhk', qr, kb,
                        preferred_element_type=jnp.float32) * scale
    qi = jnp.arange(Q_BLOCK)[:, None] + Q_BLOCK
    ki = jnp.arange(2 * Q_BLOCK)[None, :]
    dist = qi - ki
    band = (dist >= 0) & (dist <= span)
    has_prev = (jnp.arange(nb)[:, None, None] > 0) | (ki >= Q_BLOCK)[None]
    valid = band[None] & has_prev
    scores = jnp.where(valid[None, :, :, None, :], scores, -jnp.inf)
    lse = jax.nn.logsumexp(scores, axis=-1)
    p = jnp.exp(scores - lse[..., None])
    o = jnp.einsum('nbqhk,nbkhd->nbqhd', p, vb.astype(jnp.float32))
    return from_res(o), from_res(lse)


def dilated_window_attention(q, k, v):
    scale = 1.0 / math.sqrt(q.shape[-1])
    outs, lses = [], []
    for window, dil in DIL_PATTERNS:
        o, lse = _strided_band_attention(q, k, v, dil, window // dil, scale)
        outs.append(o)
        lses.append(lse)
    wts = jax.nn.softmax(jnp.stack(lses, axis=0), axis=0)
    out = jnp.einsum('pbsh,pbshd->bshd', wts, jnp.stack(outs, axis=0))
    return out.astype(q.dtype)


def causal_softmax_attention(q, k, v):
    s = q.shape[1]
    scale = 1.0 / math.sqrt(q.shape[-1])
    key_pos = jnp.arange(s)

    def one_block(args):
        qb, q0 = args
        z = jnp.einsum('bqhd,bkhd->bhqk', qb, k,
                       preferred_element_type=jnp.float32) * scale
        causal = key_pos[None, :] <= (q0 + jnp.arange(Q_BLOCK))[:, None]
        p = jax.nn.softmax(jnp.where(causal, z, -jnp.inf), axis=-1)
        return jnp.einsum('bhqk,bkhd->bqhd', p.astype(v.dtype), v)

    starts = jnp.arange(s // Q_BLOCK) * Q_BLOCK
    return _from_blocks(lax.map(one_block, (_to_blocks(q), starts)))


def even_mixer(h, w_in, q_norm, k_norm, w_out, cos, sin):
    b, s, _ = h.shape
    proj = h @ w_in
    cuts = np.cumsum([W_SB] * 4 + [W_DIL] * 3).tolist()
    q_sb, k_sb, v_sb, g_sb, q_dl, k_dl, v_dl, g_dl = jnp.split(proj, cuts, axis=-1)
    o_sb = stick_breaking_attention(q_sb.reshape(b, s, N_HEADS_SB, HEAD_DIM),
                                    k_sb.reshape(b, s, N_HEADS_SB, HEAD_DIM),
                                    v_sb.reshape(b, s, N_HEADS_SB, HEAD_DIM))
    q_dl = apply_rope(rms_norm(q_dl.reshape(b, s, N_HEADS_DIL, HEAD_DIM), q_norm), cos, sin)
    k_dl = apply_rope(rms_norm(k_dl.reshape(b, s, N_HEADS_DIL, HEAD_DIM), k_norm), cos, sin)
    o_dl = dilated_window_attention(q_dl, k_dl, v_dl.reshape(b, s, N_HEADS_DIL, HEAD_DIM))
    mixed = jnp.concatenate([o_sb.reshape(b, s, W_SB) * jax.nn.silu(g_sb),
                             o_dl.reshape(b, s, W_DIL) * jax.nn.silu(g_dl)], axis=-1)
    return mixed @ w_out


def odd_mixer(h, w_in, q_lat_norm, kv_lat_norm, w_uq, w_ukv, q_norm, k_norm, w_out, cos, sin):
    b, s, _ = h.shape
    proj = h @ w_in
    cuts = np.cumsum([Q_LORA, KV_LORA, QK_ROPE]).tolist()
    c_q, c_kv, k_pe, g = jnp.split(proj, cuts, axis=-1)
    q = (rms_norm(c_q, q_lat_norm) @ w_uq).reshape(b, s, N_HEADS_MLA, QK_HEAD)
    kv = (rms_norm(c_kv, kv_lat_norm) @ w_ukv).reshape(b, s, N_HEADS_MLA, QK_NOPE + V_HEAD)
    k_nope, v = jnp.split(kv, [QK_NOPE], axis=-1)
    k_pe = jnp.broadcast_to(k_pe[:, :, None, :], (b, s, N_HEADS_MLA, QK_ROPE))
    k = jnp.concatenate([k_nope, k_pe], axis=-1)
    q = rms_norm(q, q_norm)
    k = rms_norm(k, k_norm)
    q = jnp.concatenate([q[..., :QK_NOPE], apply_rope(q[..., QK_NOPE:], cos, sin)], axis=-1)
    k = jnp.concatenate([k[..., :QK_NOPE], apply_rope(k[..., QK_NOPE:], cos, sin)], axis=-1)
    o = causal_softmax_attention(q, k, v)
    return (o.reshape(b, s, MLA_WIDTH) * jax.nn.silu(g)) @ w_out


def setup_inputs(seed: int = 0) -> dict:
    key = jax.random.key(seed)
    ks = jax.random.split(key, 20)
    f32 = jnp.float32

    def nrm(k, shape, scale):
        return jax.random.normal(k, shape, f32) * scale

    def gain(k, shape):
        return 1.0 + 0.02 * jax.random.normal(k, shape, f32)

    x = nrm(ks[0], (BATCH, SEQ, D_MODEL), 1.0)
    c = nrm(ks[1], (BATCH, D_MODEL), 1.0)
    positions = (jnp.arange(SEQ, dtype=jnp.int32)[None, :]
                 + jax.random.randint(ks[2], (BATCH, 1), 0, 1024, dtype=jnp.int32))
    return {
        "x": x,
        "c": c,
        "positions": positions,
        "ada_w": nrm(ks[3], (DEPTH, D_MODEL, 3 * D_MODEL), ADA_SCALE * D_MODEL ** -0.5),
        "ada_b": nrm(ks[4], (DEPTH, 3 * D_MODEL), 0.01),
        "norm_w": gain(ks[5], (DEPTH, D_MODEL)),
        "ev_w_in": nrm(ks[6], (N_EVEN, D_MODEL, EVEN_IN), D_MODEL ** -0.5),
        "ev_q_norm": gain(ks[7], (N_EVEN, HEAD_DIM)),
        "ev_k_norm": gain(ks[8], (N_EVEN, HEAD_DIM)),
        "ev_w_out": nrm(ks[9], (N_EVEN, W_SB + W_DIL, D_MODEL), (W_SB + W_DIL) ** -0.5),
        "od_w_in": nrm(ks[10], (N_ODD, D_MODEL, ODD_IN), D_MODEL ** -0.5),
        "od_q_lat_norm": gain(ks[11], (N_ODD, Q_LORA)),
        "od_kv_lat_norm": gain(ks[12], (N_ODD, KV_LORA)),
        "od_w_uq": nrm(ks[13], (N_ODD, Q_LORA, N_HEADS_MLA * QK_HEAD), Q_LORA ** -0.5),
        "od_w_ukv": nrm(ks[14], (N_ODD, KV_LORA, N_HEADS_MLA * (QK_NOPE + V_HEAD)), KV_LORA ** -0.5),
        "od_q_norm": gain(ks[15], (N_ODD, QK_HEAD)),
        "od_k_norm": gain(ks[16], (N_ODD, QK_HEAD)),
        "od_w_out": nrm(ks[17], (N_ODD, MLA_WIDTH, D_MODEL), MLA_WIDTH ** -0.5),
    }


def reference(x, c, positions, ada_w, ada_b, norm_w, ev_w_in, ev_q_norm, ev_k_norm, ev_w_out,
              od_w_in, od_q_lat_norm, od_kv_lat_norm, od_w_uq, od_w_ukv, od_q_norm, od_k_norm,
              od_w_out):
    cos_full, sin_full = rope_tables(positions, HEAD_DIM)
    cos_mla, sin_mla = rope_tables(positions, QK_ROPE)
    for layer in range(DEPTH):
        shift, scale, gate = modulation(c, ada_w[layer], ada_b[layer])
        h = rms_norm(x, norm_w[layer]) * (1.0 + scale) + shift
        if layer % 2 == 0:
            i = layer // 2
            y = even_mixer(h, ev_w_in[i], ev_q_norm[i], ev_k_norm[i], ev_w_out[i],
                           cos_full, sin_full)
        else:
            i = layer // 2
            y = odd_mixer(h, od_w_in[i], od_q_lat_norm[i], od_kv_lat_norm[i], od_w_uq[i],
                          od_w_ukv[i], od_q_norm[i], od_k_norm[i], od_w_out[i],
                          cos_mla, sin_mla)
        x = x + gate * y
    return x
```

```python
import functools
import math

import numpy as np
import jax
import jax.numpy as jnp
from jax import lax
from jax.experimental import pallas as pl
from jax.experimental.pallas import tpu as pltpu

F32 = jnp.float32
BF16 = jnp.bfloat16

HEAD_DIM = 128
DIL_PATTERNS = ((128, 1), (512, 4), (2048, 16))
KV_LORA = 512
QK_NOPE = 128
QK_ROPE = 64
QK_HEAD = QK_NOPE + QK_ROPE
V_HEAD = 128
MLA_QK_PAD = 256
ROPE_THETA = 10000.0
EPS = 1e-6

LANES = 128
SUBLANES = 8
VMEM_LIMIT = 56 * 1024 * 1024
EXP_UNDERFLOW = -104.0
SB_TILE = 256
DL_TILE = 256
MLA_TILE = 512


def _cparams(sem):
    return pltpu.CompilerParams(dimension_semantics=sem, vmem_limit_bytes=VMEM_LIMIT)


def _tile(n, want):
    if n <= want:
        return n
    t = (want // LANES) * LANES
    while n % t:
        t -= LANES
    return t


def _mod_kernel(c_ref, w_ref, b_ref, o_ref):
    a = c_ref[...]
    a = (a * jax.nn.sigmoid(a)).astype(BF16)
    o_ref[...] = jnp.dot(a, w_ref[...].astype(BF16), preferred_element_type=F32) + b_ref[...]


def _modulation(c, ada_w, ada_b):
    depth, d, n3 = ada_w.shape
    b = c.shape[0]
    rows = -(-b // SUBLANES) * SUBLANES
    c_pad = jnp.pad(c, ((0, rows - b), (0, 0)))
    tn = _tile(n3, 512)
    out = pl.pallas_call(
        _mod_kernel,
        grid=(depth, n3 // tn),
        in_specs=[pl.BlockSpec((rows, d), lambda l, j: (0, 0)),
                  pl.BlockSpec((None, d, tn), lambda l, j: (l, 0, j)),
                  pl.BlockSpec((None, 1, tn), lambda l, j: (l, 0, j))],
        out_specs=pl.BlockSpec((None, rows, tn), lambda l, j: (l, 0, j)),
        out_shape=jax.ShapeDtypeStruct((depth, rows, n3), F32),
        compiler_params=_cparams(("parallel", "parallel")),
        name="modulation",
    )(c_pad, ada_w, ada_b.reshape(depth, 1, n3))
    return out[:, :b]


def _norm_mod_kernel(x_ref, w_ref, scale_ref, shift_ref, o_ref):
    x = x_ref[...]
    y = x * lax.rsqrt(jnp.mean(x * x, axis=-1, keepdims=True) + EPS)
    y = y * w_ref[...]
    o_ref[...] = (y * (1.0 + scale_ref[...]) + shift_ref[...]).astype(o_ref.dtype)


def _norm_mod(x, w, scale, shift):
    b, s, d = x.shape
    ts = _tile(s, 256)
    return pl.pallas_call(
        _norm_mod_kernel,
        grid=(b, s // ts),
        in_specs=[pl.BlockSpec((None, ts, d), lambda bi, i: (bi, i, 0)),
                  pl.BlockSpec((1, d), lambda bi, i: (0, 0)),
                  pl.BlockSpec((None, 1, d), lambda bi, i: (bi, 0, 0)),
                  pl.BlockSpec((None, 1, d), lambda bi, i: (bi, 0, 0))],
        out_specs=pl.BlockSpec((None, ts, d), lambda bi, i: (bi, i, 0)),
        out_shape=jax.ShapeDtypeStruct((b, s, d), BF16),
        compiler_params=_cparams(("parallel", "parallel")),
        name="norm_mod",
    )(x, w.reshape(1, d), scale, shift)


def _mm_kernel(*refs, n_extra, n_out, nk, epilogue, prologue):
    x_ref, w_ref = refs[0], refs[1]
    extras = refs[2:2 + n_extra]
    outs = refs[2 + n_extra:2 + n_extra + n_out]
    scratch = refs[2 + n_extra + n_out:]
    acc_ref = scratch[0]
    if prologue is not None:
        h_ref = scratch[1]

        @pl.when(pl.program_id(1) == 0)
        def _():
            h_ref[...] = prologue(x_ref, extras)

        acc_ref[...] = jnp.dot(h_ref[...], w_ref[...], preferred_element_type=F32)
        epilogue(acc_ref, extras, outs)
        return
    if nk == 1:
        acc_ref[...] = jnp.dot(x_ref[...], w_ref[...], preferred_element_type=F32)
        epilogue(acc_ref, extras, outs)
        return
    k = pl.program_id(2)

    @pl.when(k == 0)
    def _():
        acc_ref[...] = jnp.zeros_like(acc_ref)

    acc_ref[...] += jnp.dot(x_ref[...], w_ref[...], preferred_element_type=F32)

    @pl.when(k == nk - 1)
    def _():
        epilogue(acc_ref, extras, outs)


def _matmul(x, w, *, tm, tn, tk, extras, extra_specs, out_shapes, out_specs, epilogue,
            prologue=None, x_col_block=0, name):
    m = x.shape[0]
    kdim, n = w.shape
    nk = kdim // tk
    assert m % tm == 0 and n % tn == 0 and kdim % tk == 0 and x.shape[1] >= (x_col_block + nk) * tk
    scratch = [pltpu.VMEM((tm, tn), F32)]
    if prologue is not None:
        assert nk == 1
        scratch.append(pltpu.VMEM((tm, kdim), BF16))
    kern = functools.partial(_mm_kernel, n_extra=len(extras), n_out=len(out_shapes), nk=nk,
                             epilogue=epilogue, prologue=prologue)
    return pl.pallas_call(
        kern,
        grid=(m // tm, n // tn, nk),
        in_specs=[pl.BlockSpec((tm, tk), lambda i, j, k: (i, x_col_block + k)),
                  pl.BlockSpec((tk, tn), lambda i, j, k: (k, j))] + list(extra_specs),
        out_specs=list(out_specs),
        out_shape=list(out_shapes),
        scratch_shapes=scratch,
        compiler_params=_cparams(("parallel", "arbitrary", "arbitrary")),
        name=name,
    )(x, w, *extras)


def _epi_plain(acc_ref, extras, outs):
    outs[0][...] = acc_ref[...].astype(outs[0].dtype)


def _epi_resid(acc_ref, extras, outs):
    xres_ref, gate_ref = extras
    outs[0][...] = xres_ref[...] + gate_ref[...] * acc_ref[...]


def _rope_full(y, cos_ref, sin_ref):
    return y * cos_ref[...] + pltpu.roll(y, HEAD_DIM // 2, 1) * sin_ref[...]


def _rope_mla(r, c_ref, s1_ref, s2_ref):
    half = QK_ROPE // 2
    return r * c_ref[...] + pltpu.roll(r, half, 1) * s1_ref[...] + pltpu.roll(r, LANES - half, 1) * s2_ref[...]


def _epi_even(acc_ref, extras, outs, *, tn, part_tiles, q_scale):
    qn_ref, kn_ref, cos_ref, sin_ref = extras
    o_ref = outs[0]
    part = pl.program_id(1) // part_tiles

    def headnorm_rope(w_ref, scale):
        for hh in range(tn // HEAD_DIM):
            sl = slice(hh * HEAD_DIM, (hh + 1) * HEAD_DIM)
            a = acc_ref[:, sl]
            y = a * lax.rsqrt(jnp.mean(a * a, axis=-1, keepdims=True) + EPS) * w_ref[...]
            y = _rope_full(y, cos_ref, sin_ref)
            if scale != 1.0:
                y = y * scale
            o_ref[:, sl] = y.astype(o_ref.dtype)

    @pl.when(part == 0)
    def _():
        o_ref[...] = (acc_ref[...] * q_scale).astype(o_ref.dtype)

    @pl.when(part == 4)
    def _():
        headnorm_rope(qn_ref, q_scale)

    @pl.when(part == 5)
    def _():
        headnorm_rope(kn_ref, 1.0)

    @pl.when((part != 0) & (part != 4) & (part != 5))
    def _():
        o_ref[...] = acc_ref[...].astype(o_ref.dtype)


def _pro_rms(x_ref, extras):
    x = x_ref[...]
    w_ref = extras[0]
    y = x * lax.rsqrt(jnp.mean(x * x, axis=-1, keepdims=True) + EPS) * w_ref[...]
    return y.astype(BF16)


def _epi_mla_q(acc_ref, extras, outs, *, tn, q_scale):
    _, qw_ref, c_ref, s1_ref, s2_ref = extras
    o_ref = outs[0]
    for hh in range(tn // MLA_QK_PAD):
        base = hh * MLA_QK_PAD
        a = acc_ref[:, base:base + MLA_QK_PAD]
        rstd = lax.rsqrt(jnp.sum(a * a, axis=-1, keepdims=True) * (1.0 / QK_HEAD) + EPS)
        y = a * rstd * qw_ref[...] * q_scale
        o_ref[:, base:base + QK_NOPE] = y[:, :QK_NOPE].astype(o_ref.dtype)
        r = _rope_mla(y[:, QK_NOPE:], c_ref, s1_ref, s2_ref)
        o_ref[:, base + QK_NOPE:base + MLA_QK_PAD] = r.astype(o_ref.dtype)


def _epi_mla_kv(acc_ref, extras, outs, *, tn):
    _, kpe_ref, kwn_ref, kwp_ref, c_ref, s1_ref, s2_ref = extras
    k_ref, v_ref = outs
    kpe = kpe_ref[...]
    ss_pe = jnp.sum(kpe * kpe, axis=-1, keepdims=True)
    kr = _rope_mla(kpe * kwp_ref[...], c_ref, s1_ref, s2_ref)
    for hh in range(tn // (QK_NOPE + V_HEAD)):
        base = hh * (QK_NOPE + V_HEAD)
        kn = acc_ref[:, base:base + QK_NOPE]
        rstd = lax.rsqrt((jnp.sum(kn * kn, axis=-1, keepdims=True) + ss_pe) * (1.0 / QK_HEAD) + EPS)
        kb = hh * MLA_QK_PAD
        k_ref[:, kb:kb + QK_NOPE] = (kn * rstd * kwn_ref[...]).astype(k_ref.dtype)
        k_ref[:, kb + QK_NOPE:kb + MLA_QK_PAD] = (kr * rstd).astype(k_ref.dtype)
        v_ref[:, hh * V_HEAD:(hh + 1) * V_HEAD] = acc_ref[:, base + QK_NOPE:base + QK_NOPE + V_HEAD].astype(v_ref.dtype)


def _qk(q, k):
    return lax.dot_general(q, k, (((1,), (1,)), ((), ())), preferred_element_type=F32)


def _sb_kernel(q_ref, k_ref, v_ref, g_ref, u_ref, o_ref, acc_ref, carry_ref, *, tile):
    qi = pl.program_id(2)
    q = q_ref[...]
    u = u_ref[...]

    def step(kb, masked):
        start = pl.multiple_of(kb * tile, tile)
        z = _qk(q, k_ref[pl.ds(start, tile), :])
        sp = jnp.log1p(jnp.exp(-jnp.abs(z)))
        log_beta = jnp.minimum(z, 0.0) - sp
        log_keep = jnp.minimum(-z, 0.0) - sp
        if masked:
            row = lax.broadcasted_iota(jnp.int32, z.shape, 0)
            col = lax.broadcasted_iota(jnp.int32, z.shape, 1)
            before = col < row
            log_keep = jnp.where(before, log_keep, 0.0)
        hi = log_keep.astype(BF16)
        lo = (log_keep - hi.astype(F32)).astype(BF16)
        carry = carry_ref[...]
        excl = (jnp.dot(hi, u, preferred_element_type=F32)
                + jnp.dot(lo, u, preferred_element_type=F32) + carry)
        w = jnp.exp(log_beta + excl)
        if masked:
            w = jnp.where(before, w, 0.0)
        acc_ref[...] += jnp.dot(w.astype(BF16), v_ref[pl.ds(start, tile), :], preferred_element_type=F32)
        carry = carry + jnp.sum(log_keep, axis=-1, keepdims=True)
        carry_ref[...] = carry
        return jnp.max(carry)

    acc_ref[...] = jnp.zeros_like(acc_ref)
    carry_ref[...] = jnp.zeros_like(carry_ref)
    cmax0 = step(qi, True)

    def cond(state):
        kb, cmax = state
        return (kb >= 0) & (cmax > EXP_UNDERFLOW)

    def body(state):
        kb, _ = state
        return kb - 1, step(kb, False)

    lax.while_loop(cond, body, (qi - 1, cmax0))
    g = g_ref[...].astype(F32)
    o_ref[...] = (acc_ref[...] * (g * jax.nn.sigmoid(g))).astype(o_ref.dtype)


def _dl_kernel(q_ref, k_ref, v_ref, g_ref, mult_ref, o_ref, acc_ref, m_ref, l_ref, *, tile, n_off):
    qi = pl.program_id(2)
    q = q_ref[...]
    acc_ref[...] = jnp.zeros_like(acc_ref)
    m_ref[...] = jnp.full_like(m_ref, -jnp.inf)
    l_ref[...] = jnp.zeros_like(l_ref)

    def body(d, _):
        start = pl.multiple_of((qi - d) * tile, tile)
        s = _qk(q, k_ref[pl.ds(start, tile), :])
        mult = mult_ref[d]
        s = jnp.where(mult > 0.0, s, -jnp.inf)
        m_old = m_ref[...]
        m_new = jnp.maximum(m_old, jnp.max(s, axis=-1, keepdims=True))
        p = mult * jnp.exp(s - m_new)
        alpha = jnp.exp(m_old - m_new)
        l_ref[...] = alpha * l_ref[...] + jnp.sum(p, axis=-1, keepdims=True)
        acc_ref[...] = alpha * acc_ref[...] + jnp.dot(p.astype(BF16), v_ref[pl.ds(start, tile), :],
                                                      preferred_element_type=F32)
        m_ref[...] = m_new
        return 0

    lax.fori_loop(0, jnp.minimum(qi, n_off - 1) + 1, body, 0)
    g = g_ref[...].astype(F32)
    o_ref[...] = (acc_ref[...] / l_ref[...] * (g * jax.nn.sigmoid(g))).astype(o_ref.dtype)


def _mla_kernel(q_ref, k_ref, v_ref, g_ref, o_ref, acc_ref, m_ref, l_ref, *, tile):
    qi = pl.program_id(2)
    q = q_ref[...]
    acc_ref[...] = jnp.zeros_like(acc_ref)
    m_ref[...] = jnp.full_like(m_ref, -jnp.inf)
    l_ref[...] = jnp.zeros_like(l_ref)

    def step(kb, masked):
        start = pl.multiple_of(kb * tile, tile)
        s = _qk(q, k_ref[pl.ds(start, tile), :])
        if masked:
            row = lax.broadcasted_iota(jnp.int32, s.shape, 0)
            col = lax.broadcasted_iota(jnp.int32, s.shape, 1)
            s = jnp.where(col <= row, s, -jnp.inf)
        m_old = m_ref[...]
        m_new = jnp.maximum(m_old, jnp.max(s, axis=-1, keepdims=True))
        p = jnp.exp(s - m_new)
        alpha = jnp.exp(m_old - m_new)
        l_ref[...] = alpha * l_ref[...] + jnp.sum(p, axis=-1, keepdims=True)
        acc_ref[...] = alpha * acc_ref[...] + jnp.dot(p.astype(BF16), v_ref[pl.ds(start, tile), :],
                                                      preferred_element_type=F32)
        m_ref[...] = m_new

    def body(kb, _):
        step(kb, False)
        return 0

    lax.fori_loop(0, qi, body, 0)
    step(qi, True)
    g = g_ref[...].astype(F32)
    o_ref[...] = (acc_ref[...] / l_ref[...] * (g * jax.nn.sigmoid(g))).astype(o_ref.dtype)


def _attn_call(kern, arrays, col_blocks, widths, consts, out_width_total, out_col0, n_heads, tile,
               scratch, name):
    q, k, v, g = arrays
    b, s, _ = q.shape
    cq, ck, cv, cg = col_blocks
    wq, wk, wv, wg = widths
    in_specs = [
        pl.BlockSpec((None, tile, wq), lambda bi, h, i: (bi, i, cq + h)),
        pl.BlockSpec((None, s, wk), lambda bi, h, i: (bi, 0, ck + h)),
        pl.BlockSpec((None, s, wv), lambda bi, h, i: (bi, 0, cv + h)),
        pl.BlockSpec((None, tile, wg), lambda bi, h, i: (bi, i, cg + h)),
    ]
    for cst in consts:
        in_specs.append(pl.BlockSpec(cst.shape, lambda bi, h, i, nd=cst.ndim: (0,) * nd))
    return pl.pallas_call(
        kern,
        grid=(b, n_heads, s // tile),
        in_specs=in_specs,
        out_specs=pl.BlockSpec((None, tile, wv), lambda bi, h, i: (bi, i, out_col0 + h)),
        out_shape=jax.ShapeDtypeStruct((b, s, out_width_total), BF16),
        scratch_shapes=scratch,
        compiler_params=_cparams(("parallel", "parallel", "arbitrary")),
        name=name,
    )(q, k, v, g, *consts)


def _dl_multiplicity(tile):
    far = max(w for w, _ in DIL_PATTERNS)
    n_off = far // tile + 1
    dist = (np.arange(n_off)[:, None, None] * tile + np.arange(tile)[None, :, None]
            - np.arange(tile)[None, None, :])
    mult = np.zeros(dist.shape, np.float32)
    for window, dil in DIL_PATTERNS:
        mult += (dist >= 0) & (dist % dil == 0) & (dist <= window)
    return mult


def _rope_angles(positions, dim):
    inv = ROPE_THETA ** (-jnp.arange(0, dim, 2, dtype=F32) / dim)
    ang = positions.astype(F32)[..., None] * inv
    return jnp.cos(ang), jnp.sin(ang)


def _even_layer(x, h, gate, w_in, qn, kn, w_out, cosf, sinf, sb_u, dl_mult):
    b, s, d = x.shape
    m = b * s
    n_in = w_in.shape[1]
    w_part = n_in // 8
    n_heads = w_part // HEAD_DIM
    tm, tn, tk = _tile(s, 1024), _tile(w_part, 1024), _tile(d, 1024)
    q_scale = 1.0 / math.sqrt(HEAD_DIM)
    proj = _matmul(
        h.reshape(m, d), w_in.astype(BF16), tm=tm, tn=tn, tk=tk,
        extras=(qn.reshape(1, HEAD_DIM), kn.reshape(1, HEAD_DIM), cosf, sinf),
        extra_specs=[pl.BlockSpec((1, HEAD_DIM), lambda i, j, k: (0, 0)),
                     pl.BlockSpec((1, HEAD_DIM), lambda i, j, k: (0, 0)),
                     pl.BlockSpec((tm, HEAD_DIM), lambda i, j, k: (i, 0)),
                     pl.BlockSpec((tm, HEAD_DIM), lambda i, j, k: (i, 0))],
        out_shapes=[jax.ShapeDtypeStruct((m, n_in), BF16)],
        out_specs=[pl.BlockSpec((tm, tn), lambda i, j, k: (i, j))],
        epilogue=functools.partial(_epi_even, tn=tn, part_tiles=w_part // tn, q_scale=q_scale),
        name="even_in_proj")[0].reshape(b, s, n_in)

    hb = n_heads
    w4 = (HEAD_DIM,) * 4
    o_sb = _attn_call(
        functools.partial(_sb_kernel, tile=SB_TILE), (proj,) * 4, (0, hb, 2 * hb, 3 * hb), w4,
        (sb_u,), w_part, 0, n_heads, SB_TILE,
        [pltpu.VMEM((SB_TILE, HEAD_DIM), F32), pltpu.VMEM((SB_TILE, 1), F32)], "stick_breaking")
    o_dl = _attn_call(
        functools.partial(_dl_kernel, tile=DL_TILE, n_off=dl_mult.shape[0]), (proj,) * 4,
        (4 * hb, 5 * hb, 6 * hb, 7 * hb), w4, (dl_mult,), w_part, 0, n_heads, DL_TILE,
        [pltpu.VMEM((DL_TILE, HEAD_DIM), F32), pltpu.VMEM((DL_TILE, 1), F32),
         pltpu.VMEM((DL_TILE, 1), F32)], "dilated_window")
    mixed = jnp.concatenate([o_sb, o_dl], axis=-1).reshape(m, 2 * w_part)
    return _out_proj(x, mixed, w_out, gate)


def _out_proj(x, mixed, w_out, gate):
    b, s, d = x.shape
    m = b * s
    kdim = mixed.shape[1]
    tm, tn, tk = _tile(s, 1024), _tile(d, 1024), _tile(kdim, 1024)
    rows_per_batch = s // tm
    return _matmul(
        mixed, w_out.astype(BF16), tm=tm, tn=tn, tk=tk,
        extras=(x.reshape(m, d), gate),
        extra_specs=[pl.BlockSpec((tm, tn), lambda i, j, k: (i, j)),
                     pl.BlockSpec((None, 1, tn), lambda i, j, k: (i // rows_per_batch, 0, j))],
        out_shapes=[jax.ShapeDtypeStruct((m, d), F32)],
        out_specs=[pl.BlockSpec((tm, tn), lambda i, j, k: (i, j))],
        epilogue=_epi_resid, name="out_proj")[0].reshape(b, s, d)


def _odd_layer(x, h, gate, w_in, q_lat_norm, kv_lat_norm, w_uq, w_ukv, q_norm, k_norm, w_out,
               rc, rs1, rs2):
    b, s, d = x.shape
    m = b * s
    q_lora = q_lat_norm.shape[0]
    kv_lora = kv_lat_norm.shape[0]
    n_heads = w_uq.shape[1] // QK_HEAD
    lat_raw = q_lora + kv_lora + QK_ROPE
    n_lat = -(-lat_raw // LANES) * LANES
    h2 = h.reshape(m, d)
    tm, tk = _tile(s, 1024), _tile(d, 1024)

    w_lat = jnp.pad(w_in[:, :lat_raw], ((0, 0), (0, n_lat - lat_raw))).astype(BF16)
    tm_lat = _tile(s, 512)
    lat = _matmul(h2, w_lat, tm=tm_lat, tn=n_lat, tk=tk, extras=(), extra_specs=[],
                  out_shapes=[jax.ShapeDtypeStruct((m, n_lat), F32)],
                  out_specs=[pl.BlockSpec((tm_lat, n_lat), lambda i, j, k: (i, j))],
                  epilogue=_epi_plain, name="odd_latent_proj")[0]
    n_g = w_in.shape[1] - lat_raw
    tn_g = _tile(n_g, 1024)
    g = _matmul(h2, w_in[:, lat_raw:].astype(BF16), tm=tm, tn=tn_g, tk=tk, extras=(), extra_specs=[],
                out_shapes=[jax.ShapeDtypeStruct((m, n_g), BF16)],
                out_specs=[pl.BlockSpec((tm, tn_g), lambda i, j, k: (i, j))],
                epilogue=_epi_plain, name="odd_gate_proj")[0]

    pad = MLA_QK_PAD - QK_HEAD
    w_uq_p = jnp.pad(w_uq.reshape(q_lora, n_heads, QK_HEAD), ((0, 0), (0, 0), (0, pad)))
    w_uq_p = w_uq_p.reshape(q_lora, n_heads * MLA_QK_PAD).astype(BF16)
    qw = jnp.pad(q_norm, (0, pad)).reshape(1, MLA_QK_PAD)
    tn_q = _tile(n_heads * MLA_QK_PAD, 1024)
    rope_specs = [pl.BlockSpec((tm, LANES), lambda i, j, k: (i, 0))] * 3
    q = _matmul(
        lat, w_uq_p, tm=tm, tn=tn_q, tk=q_lora,
        extras=(q_lat_norm.reshape(1, q_lora), qw, rc, rs1, rs2),
        extra_specs=[pl.BlockSpec((1, q_lora), lambda i, j, k: (0, 0)),
                     pl.BlockSpec((1, MLA_QK_PAD), lambda i, j, k: (0, 0))] + rope_specs,
        out_shapes=[jax.ShapeDtypeStruct((m, n_heads * MLA_QK_PAD), BF16)],
        out_specs=[pl.BlockSpec((tm, tn_q), lambda i, j, k: (i, j))],
        epilogue=functools.partial(_epi_mla_q, tn=tn_q, q_scale=1.0 / math.sqrt(QK_HEAD)),
        prologue=_pro_rms, name="mla_q_proj")[0]

    assert q_lora % kv_lora == 0 and (q_lora + kv_lora) % LANES == 0
    kvw = QK_NOPE + V_HEAD
    tn_kv = _tile(n_heads * kvw, 1024)
    heads_per_tile = tn_kv // kvw
    kwn = k_norm[:QK_NOPE].reshape(1, QK_NOPE)
    kwp = jnp.pad(k_norm[QK_NOPE:], (0, LANES - QK_ROPE)).reshape(1, LANES)
    kpe_block = (q_lora + kv_lora) // LANES
    k, v = _matmul(
        lat, w_ukv.astype(BF16), tm=tm, tn=tn_kv, tk=kv_lora,
        extras=(kv_lat_norm.reshape(1, kv_lora), lat, kwn, kwp, rc, rs1, rs2),
        extra_specs=[pl.BlockSpec((1, kv_lora), lambda i, j, k: (0, 0)),
                     pl.BlockSpec((tm, LANES), lambda i, j, k: (i, kpe_block)),
                     pl.BlockSpec((1, QK_NOPE), lambda i, j, k: (0, 0)),
                     pl.BlockSpec((1, LANES), lambda i, j, k: (0, 0))] + rope_specs,
        out_shapes=[jax.ShapeDtypeStruct((m, n_heads * MLA_QK_PAD), BF16),
                    jax.ShapeDtypeStruct((m, n_heads * V_HEAD), BF16)],
        out_specs=[pl.BlockSpec((tm, heads_per_tile * MLA_QK_PAD), lambda i, j, k: (i, j)),
                   pl.BlockSpec((tm, heads_per_tile * V_HEAD), lambda i, j, k: (i, j))],
        epilogue=functools.partial(_epi_mla_kv, tn=tn_kv),
        prologue=_pro_rms, x_col_block=q_lora // kv_lora, name="mla_kv_proj")
    mixed = _attn_call(
        functools.partial(_mla_kernel, tile=MLA_TILE),
        (q.reshape(b, s, -1), k.reshape(b, s, -1), v.reshape(b, s, -1), g.reshape(b, s, -1)),
        (0, 0, 0, 0), (MLA_QK_PAD, MLA_QK_PAD, V_HEAD, V_HEAD), (), n_heads * V_HEAD, 0, n_heads,
        MLA_TILE,
        [pltpu.VMEM((MLA_TILE, V_HEAD), F32), pltpu.VMEM((MLA_TILE, 1), F32),
         pltpu.VMEM((MLA_TILE, 1), F32)], "mla_attention")
    return _out_proj(x, mixed.reshape(m, -1), w_out, gate)


def kernel(x, c, positions, ada_w, ada_b, norm_w, ev_w_in, ev_q_norm, ev_k_norm, ev_w_out, od_w_in,
           od_q_lat_norm, od_kv_lat_norm, od_w_uq, od_w_ukv, od_q_norm, od_k_norm, od_w_out):
    b, s, d = x.shape
    depth = ada_w.shape[0]
    m = b * s

    cos_f, sin_f = _rope_angles(positions, HEAD_DIM)
    cosf = jnp.concatenate([cos_f, cos_f], axis=-1).reshape(m, HEAD_DIM)
    sinf = jnp.concatenate([-sin_f, sin_f], axis=-1).reshape(m, HEAD_DIM)
    cos_m, sin_m = _rope_angles(positions, QK_ROPE)
    z32 = jnp.zeros_like(cos_m)
    z64 = jnp.concatenate([z32, z32], axis=-1)
    rc = jnp.concatenate([cos_m, cos_m, z64], axis=-1).reshape(m, LANES)
    rs1 = jnp.concatenate([z32, sin_m, z64], axis=-1).reshape(m, LANES)
    rs2 = jnp.concatenate([-sin_m, z32, z64], axis=-1).reshape(m, LANES)

    sb_u = jnp.asarray(np.tril(np.ones((SB_TILE, SB_TILE), np.float32), -1), BF16)
    dl_mult = jnp.asarray(_dl_multiplicity(DL_TILE))

    mod = _modulation(c, ada_w, ada_b)
    shift, scale, gate = (mod[:, :, i * d:(i + 1) * d].reshape(depth, b, 1, d) for i in range(3))

    for layer in range(depth):
        h = _norm_mod(x, norm_w[layer], scale[layer], shift[layer])
        i = layer // 2
        if layer % 2 == 0:
            x = _even_layer(x, h, gate[layer], ev_w_in[i], ev_q_norm[i], ev_k_norm[i], ev_w_out[i],
                            cosf, sinf, sb_u, dl_mult)
        else:
            x = _odd_layer(x, h, gate[layer], od_w_in[i], od_q_lat_norm[i], od_kv_lat_norm[i],
                           od_w_uq[i], od_w_ukv[i], od_q_norm[i], od_k_norm[i], od_w_out[i],
                           rc, rs1, rs2)
    return x
```

```python
import functools
import math

import numpy as np
import jax
import jax.numpy as jnp
from jax import lax
from jax.experimental import pallas as pl
from jax.experimental.pallas import tpu as pltpu

F32 = jnp.float32
BF16 = jnp.bfloat16

HEAD_DIM = 128
DIL_PATTERNS = ((128, 1), (512, 4), (2048, 16))
KV_LORA = 512
QK_NOPE = 128
QK_ROPE = 64
QK_HEAD = QK_NOPE + QK_ROPE
V_HEAD = 128
MLA_QK_PAD = 256
ROPE_THETA = 10000.0
EPS = 1e-6

LANES = 128
SUBLANES = 8
VMEM_LIMIT = 56 * 1024 * 1024
EXP_UNDERFLOW = -104.0
SB_TILE = 256
DL_TILE = 128
DL_CHUNK = DL_TILE * max(d for _, d in DIL_PATTERNS)
MLA_TILE = 512


def _cparams(sem):
    return pltpu.CompilerParams(dimension_semantics=sem, vmem_limit_bytes=VMEM_LIMIT)


def _tile(n, want):
    if n <= want:
        return n
    t = (want // LANES) * LANES
    while n % t:
        t -= LANES
    return t


def _mod_kernel(c_ref, w_ref, b_ref, o_ref):
    a = c_ref[...]
    a = (a * jax.nn.sigmoid(a)).astype(BF16)
    o_ref[...] = jnp.dot(a, w_ref[...].astype(BF16), preferred_element_type=F32) + b_ref[...]


def _modulation(c, ada_w, ada_b):
    depth, d, n3 = ada_w.shape
    b = c.shape[0]
    rows = -(-b // SUBLANES) * SUBLANES
    c_pad = jnp.pad(c, ((0, rows - b), (0, 0)))
    tn = _tile(n3, 512)
    out = pl.pallas_call(
        _mod_kernel,
        grid=(depth, n3 // tn),
        in_specs=[pl.BlockSpec((rows, d), lambda l, j: (0, 0)),
                  pl.BlockSpec((None, d, tn), lambda l, j: (l, 0, j)),
                  pl.BlockSpec((None, 1, tn), lambda l, j: (l, 0, j))],
        out_specs=pl.BlockSpec((None, rows, tn), lambda l, j: (l, 0, j)),
        out_shape=jax.ShapeDtypeStruct((depth, rows, n3), F32),
        compiler_params=_cparams(("parallel", "parallel")),
        name="modulation",
    )(c_pad, ada_w, ada_b.reshape(depth, 1, n3))
    return out[:, :b]


def _norm_mod_kernel(x_ref, w_ref, scale_ref, shift_ref, o_ref):
    x = x_ref[...]
    y = x * lax.rsqrt(jnp.mean(x * x, axis=-1, keepdims=True) + EPS)
    y = y * w_ref[...]
    o_ref[...] = (y * (1.0 + scale_ref[...]) + shift_ref[...]).astype(o_ref.dtype)


def _norm_mod(x, w, scale, shift):
    b, s, d = x.shape
    ts = _tile(s, 256)
    return pl.pallas_call(
        _norm_mod_kernel,
        grid=(b, s // ts),
        in_specs=[pl.BlockSpec((None, ts, d), lambda bi, i: (bi, i, 0)),
                  pl.BlockSpec((1, d), lambda bi, i: (0, 0)),
                  pl.BlockSpec((None, 1, d), lambda bi, i: (bi, 0, 0)),
                  pl.BlockSpec((None, 1, d), lambda bi, i: (bi, 0, 0))],
        out_specs=pl.BlockSpec((None, ts, d), lambda bi, i: (bi, i, 0)),
        out_shape=jax.ShapeDtypeStruct((b, s, d), BF16),
        compiler_params=_cparams(("parallel", "parallel")),
        name="norm_mod",
    )(x, w.reshape(1, d), scale, shift)


def _mm_kernel(*refs, n_extra, n_out, nk, epilogue, prologue):
    x_ref, w_ref = refs[0], refs[1]
    extras = refs[2:2 + n_extra]
    outs = refs[2 + n_extra:2 + n_extra + n_out]
    scratch = refs[2 + n_extra + n_out:]
    acc_ref = scratch[0]
    if prologue is not None:
        h_ref = scratch[1]

        @pl.when(pl.program_id(1) == 0)
        def _():
            h_ref[...] = prologue(x_ref, extras)

        acc_ref[...] = jnp.dot(h_ref[...], w_ref[...], preferred_element_type=F32)
        epilogue(acc_ref, extras, outs)
        return
    if nk == 1:
        acc_ref[...] = jnp.dot(x_ref[...], w_ref[...], preferred_element_type=F32)
        epilogue(acc_ref, extras, outs)
        return
    k = pl.program_id(2)

    @pl.when(k == 0)
    def _():
        acc_ref[...] = jnp.zeros_like(acc_ref)

    acc_ref[...] += jnp.dot(x_ref[...], w_ref[...], preferred_element_type=F32)

    @pl.when(k == nk - 1)
    def _():
        epilogue(acc_ref, extras, outs)


def _matmul(x, w, *, tm, tn, tk, extras, extra_specs, out_shapes, out_specs, epilogue,
            prologue=None, x_col_block=0, name):
    m = x.shape[0]
    kdim, n = w.shape
    nk = kdim // tk
    assert m % tm == 0 and n % tn == 0 and kdim % tk == 0 and x.shape[1] >= (x_col_block + nk) * tk
    scratch = [pltpu.VMEM((tm, tn), F32)]
    if prologue is not None:
        assert nk == 1
        scratch.append(pltpu.VMEM((tm, kdim), BF16))
    kern = functools.partial(_mm_kernel, n_extra=len(extras), n_out=len(out_shapes), nk=nk,
                             epilogue=epilogue, prologue=prologue)
    return pl.pallas_call(
        kern,
        grid=(m // tm, n // tn, nk),
        in_specs=[pl.BlockSpec((tm, tk), lambda i, j, k: (i, x_col_block + k)),
                  pl.BlockSpec((tk, tn), lambda i, j, k: (k, j))] + list(extra_specs),
        out_specs=list(out_specs),
        out_shape=list(out_shapes),
        scratch_shapes=scratch,
        compiler_params=_cparams(("parallel", "arbitrary", "arbitrary")),
        name=name,
    )(x, w, *extras)


def _epi_plain(acc_ref, extras, outs):
    outs[0][...] = acc_ref[...].astype(outs[0].dtype)


def _epi_resid(acc_ref, extras, outs):
    xres_ref, gate_ref = extras
    outs[0][...] = xres_ref[...] + gate_ref[...] * acc_ref[...]


def _rope_full(y, cos_ref, sin_ref):
    return y * cos_ref[...] + pltpu.roll(y, HEAD_DIM // 2, 1) * sin_ref[...]


def _rope_mla(r, c_ref, s1_ref, s2_ref):
    half = QK_ROPE // 2
    return r * c_ref[...] + pltpu.roll(r, half, 1) * s1_ref[...] + pltpu.roll(r, LANES - half, 1) * s2_ref[...]


def _epi_even(acc_ref, extras, outs, *, tn, part_tiles, q_scale):
    qn_ref, kn_ref, cos_ref, sin_ref = extras
    o_ref = outs[0]
    part = pl.program_id(1) // part_tiles

    def headnorm_rope(w_ref, scale):
        for hh in range(tn // HEAD_DIM):
            sl = slice(hh * HEAD_DIM, (hh + 1) * HEAD_DIM)
            a = acc_ref[:, sl]
            y = a * lax.rsqrt(jnp.mean(a * a, axis=-1, keepdims=True) + EPS) * w_ref[...]
            y = _rope_full(y, cos_ref, sin_ref)
            if scale != 1.0:
                y = y * scale
            o_ref[:, sl] = y.astype(o_ref.dtype)

    @pl.when(part == 0)
    def _():
        o_ref[...] = (acc_ref[...] * q_scale).astype(o_ref.dtype)

    @pl.when(part == 4)
    def _():
        headnorm_rope(qn_ref, q_scale)

    @pl.when(part == 5)
    def _():
        headnorm_rope(kn_ref, 1.0)

    @pl.when((part != 0) & (part != 4) & (part != 5))
    def _():
        o_ref[...] = acc_ref[...].astype(o_ref.dtype)


def _pro_rms(x_ref, extras):
    x = x_ref[...]
    w_ref = extras[0]
    y = x * lax.rsqrt(jnp.mean(x * x, axis=-1, keepdims=True) + EPS) * w_ref[...]
    return y.astype(BF16)


def _epi_mla_q(acc_ref, extras, outs, *, tn, q_scale):
    _, qw_ref, c_ref, s1_ref, s2_ref = extras
    o_ref = outs[0]
    for hh in range(tn // MLA_QK_PAD):
        base = hh * MLA_QK_PAD
        a = acc_ref[:, base:base + MLA_QK_PAD]
        rstd = lax.rsqrt(jnp.sum(a * a, axis=-1, keepdims=True) * (1.0 / QK_HEAD) + EPS)
        y = a * rstd * qw_ref[...] * q_scale
        o_ref[:, base:base + QK_NOPE] = y[:, :QK_NOPE].astype(o_ref.dtype)
        r = _rope_mla(y[:, QK_NOPE:], c_ref, s1_ref, s2_ref)
        o_ref[:, base + QK_NOPE:base + MLA_QK_PAD] = r.astype(o_ref.dtype)


def _epi_mla_kv(acc_ref, extras, outs, *, tn):
    _, kpe_ref, kwn_ref, kwp_ref, c_ref, s1_ref, s2_ref = extras
    k_ref, v_ref = outs
    kpe = kpe_ref[...]
    ss_pe = jnp.sum(kpe * kpe, axis=-1, keepdims=True)
    kr = _rope_mla(kpe * kwp_ref[...], c_ref, s1_ref, s2_ref)
    for hh in range(tn // (QK_NOPE + V_HEAD)):
        base = hh * (QK_NOPE + V_HEAD)
        kn = acc_ref[:, base:base + QK_NOPE]
        rstd = lax.rsqrt((jnp.sum(kn * kn, axis=-1, keepdims=True) + ss_pe) * (1.0 / QK_HEAD) + EPS)
        kb = hh * MLA_QK_PAD
        k_ref[:, kb:kb + QK_NOPE] = (kn * rstd * kwn_ref[...]).astype(k_ref.dtype)
        k_ref[:, kb + QK_NOPE:kb + MLA_QK_PAD] = (kr * rstd).astype(k_ref.dtype)
        v_ref[:, hh * V_HEAD:(hh + 1) * V_HEAD] = acc_ref[:, base + QK_NOPE:base + QK_NOPE + V_HEAD].astype(v_ref.dtype)


def _qk(q, k):
    return lax.dot_general(q, k, (((1,), (1,)), ((), ())), preferred_element_type=F32)


def _sb_kernel(q_ref, k_ref, v_ref, g_ref, u_ref, o_ref, acc_ref, carry_ref, *, tile):
    qi = pl.program_id(2)
    q = q_ref[...]
    u = u_ref[...]

    def step(kb, masked):
        start = pl.multiple_of(kb * tile, tile)
        z = _qk(q, k_ref[pl.ds(start, tile), :])
        sp = jnp.log1p(jnp.exp(-jnp.abs(z)))
        log_beta = jnp.minimum(z, 0.0) - sp
        log_keep = jnp.minimum(-z, 0.0) - sp
        if masked:
            row = lax.broadcasted_iota(jnp.int32, z.shape, 0)
            col = lax.broadcasted_iota(jnp.int32, z.shape, 1)
            before = col < row
            log_keep = jnp.where(before, log_keep, 0.0)
        hi = log_keep.astype(BF16)
        lo = (log_keep - hi.astype(F32)).astype(BF16)
        carry = carry_ref[...]
        excl = (jnp.dot(hi, u, preferred_element_type=F32)
                + jnp.dot(lo, u, preferred_element_type=F32) + carry)
        w = jnp.exp(log_beta + excl)
        if masked:
            w = jnp.where(before, w, 0.0)
        acc_ref[...] += jnp.dot(w.astype(BF16), v_ref[pl.ds(start, tile), :], preferred_element_type=F32)
        carry = carry + jnp.sum(log_keep, axis=-1, keepdims=True)
        carry_ref[...] = carry
        return jnp.max(carry)

    acc_ref[...] = jnp.zeros_like(acc_ref)
    carry_ref[...] = jnp.zeros_like(carry_ref)
    cmax0 = step(qi, True)

    def cond(state):
        kb, cmax = state
        return (kb >= 0) & (cmax > EXP_UNDERFLOW)

    def body(state):
        kb, _ = state
        return kb - 1, step(kb, False)

    lax.while_loop(cond, body, (qi - 1, cmax0))
    g = g_ref[...].astype(F32)
    o_ref[...] = (acc_ref[...] * (g * jax.nn.sigmoid(g))).astype(o_ref.dtype)


def _dl_kernel(q_ref, k_ref, v_ref, g_ref, bias_ref, o_ref, qf_ref, kf_ref, vf_ref, acc_ref, m_ref, l_ref,
               *, chunk):
    ci = pl.program_id(2)

    @pl.when(ci == 0)
    def _():
        kf_ref[...] = k_ref[...].astype(F32)
        vf_ref[...] = v_ref[...].astype(F32)

    qf_ref[...] = q_ref[...].astype(F32)
    base = ci * chunk

    def body(idx, _):
        for pi, (_, dil) in enumerate(DIL_PATTERNS):
            unit = DL_TILE * dil
            q0 = (idx // dil) * unit + idx % dil
            rows = pl.ds(q0, DL_TILE, stride=dil)
            cur0 = base + q0
            has_prev = cur0 >= unit
            prev0 = jnp.maximum(cur0 - unit, 0)
            prev_rows = pl.ds(prev0, DL_TILE, stride=dil)
            cur_rows = pl.ds(cur0, DL_TILE, stride=dil)
            kk = jnp.concatenate([kf_ref[prev_rows, :], kf_ref[cur_rows, :]], axis=0).astype(BF16)
            vv = jnp.concatenate([vf_ref[prev_rows, :], vf_ref[cur_rows, :]], axis=0).astype(BF16)
            s = _qk(qf_ref[rows, :].astype(BF16), kk) + bias_ref[pi, has_prev.astype(jnp.int32)]
            m = jnp.max(s, axis=-1, keepdims=True)
            p = jnp.exp(s - m)
            l = jnp.sum(p, axis=-1, keepdims=True)
            acc_ref[pi, rows, :] = jnp.dot(p.astype(BF16), vv, preferred_element_type=F32)
            m_ref[pi, rows, :] = jnp.broadcast_to(m, (DL_TILE, LANES))
            l_ref[pi, rows, :] = jnp.broadcast_to(l, (DL_TILE, LANES))
        return 0

    lax.fori_loop(0, chunk // DL_TILE, body, 0, unroll=8)

    n_pat = len(DIL_PATTERNS)
    m_all = functools.reduce(jnp.maximum, [m_ref[pi] for pi in range(n_pat)])
    num = jnp.zeros(acc_ref.shape[1:], F32)
    den = jnp.zeros(acc_ref.shape[1:], F32)
    for pi in range(n_pat):
        w = jnp.exp(m_ref[pi] - m_all)
        num = num + w * acc_ref[pi]
        den = den + w * l_ref[pi]
    g = g_ref[...].astype(F32)
    o_ref[...] = (num / den * (g * jax.nn.sigmoid(g))).astype(o_ref.dtype)


def _softmax_update(s_ref, v, acc_ref, m_ref, l_ref, masked):
    tq, tk = s_ref.shape
    cols = [s_ref[:, c * LANES:(c + 1) * LANES] for c in range(tk // LANES)]
    if masked:
        row = lax.broadcasted_iota(jnp.int32, (tq, LANES), 0)
        lane = lax.broadcasted_iota(jnp.int32, (tq, LANES), 1)
        cols = [jnp.where(lane + c * LANES <= row, col, -jnp.inf) for c, col in enumerate(cols)]
    m_old = m_ref[...]
    tile_max = jnp.max(functools.reduce(jnp.maximum, cols), axis=-1, keepdims=True)
    m_new = jnp.maximum(m_old, tile_max)
    ps = [jnp.exp(col - m_new) for col in cols]
    alpha = jnp.exp(m_old - m_new)
    l_ref[...] = alpha * l_ref[...] + jnp.sum(functools.reduce(jnp.add, ps), axis=-1, keepdims=True)
    p = jnp.concatenate([x.astype(BF16) for x in ps], axis=1)
    acc_ref[...] = alpha * acc_ref[...] + jnp.dot(p, v, preferred_element_type=F32)
    m_ref[...] = m_new


def _mla_kernel(q_ref, k_ref, v_ref, g_ref, o_ref, sa_ref, sb_ref, acc_ref, m_ref, l_ref, *, tile):
    qi = pl.program_id(2)
    acc_ref[...] = jnp.zeros_like(acc_ref)
    m_ref[...] = jnp.full_like(m_ref, -jnp.inf)
    l_ref[...] = jnp.zeros_like(l_ref)

    def rows(kb):
        return pl.ds(pl.multiple_of(kb * tile, tile), tile)

    def scores(kb, dst_ref):
        dst_ref[...] = _qk(q_ref[...], k_ref[rows(kb), :])

    def update(kb, src_ref, masked):
        _softmax_update(src_ref, v_ref[rows(kb), :], acc_ref, m_ref, l_ref, masked)

    scores(0, sa_ref)

    def pair(jj, _):
        j = 2 * jj
        scores(j + 1, sb_ref)
        update(j, sa_ref, False)
        scores(j + 2, sa_ref)
        update(j + 1, sb_ref, False)
        return 0

    lax.fori_loop(0, qi // 2, pair, 0)

    @pl.when(qi % 2 == 1)
    def _():
        scores(qi, sb_ref)
        update(qi - 1, sa_ref, False)
        update(qi, sb_ref, True)

    @pl.when(qi % 2 == 0)
    def _():
        update(qi, sa_ref, True)

    g = g_ref[...].astype(F32)
    o_ref[...] = (acc_ref[...] / l_ref[...] * (g * jax.nn.sigmoid(g))).astype(o_ref.dtype)


def _attn_call(kern, arrays, col_blocks, widths, consts, out_width_total, out_col0, n_heads, tile,
               scratch, name):
    q, k, v, g = arrays
    b, s, _ = q.shape
    cq, ck, cv, cg = col_blocks
    wq, wk, wv, wg = widths
    in_specs = [
        pl.BlockSpec((None, tile, wq), lambda bi, h, i: (bi, i, cq + h)),
        pl.BlockSpec((None, s, wk), lambda bi, h, i: (bi, 0, ck + h)),
        pl.BlockSpec((None, s, wv), lambda bi, h, i: (bi, 0, cv + h)),
        pl.BlockSpec((None, tile, wg), lambda bi, h, i: (bi, i, cg + h)),
    ]
    for cst in consts:
        in_specs.append(pl.BlockSpec(cst.shape, lambda bi, h, i, nd=cst.ndim: (0,) * nd))
    return pl.pallas_call(
        kern,
        grid=(b, n_heads, s // tile),
        in_specs=in_specs,
        out_specs=pl.BlockSpec((None, tile, wv), lambda bi, h, i: (bi, i, out_col0 + h)),
        out_shape=jax.ShapeDtypeStruct((b, s, out_width_total), BF16),
        scratch_shapes=scratch,
        compiler_params=_cparams(("parallel", "parallel", "arbitrary")),
        name=name,
    )(q, k, v, g, *consts)


def _dl_band_bias():
    i = np.arange(DL_TILE)[:, None]
    c = np.arange(2 * DL_TILE)[None, :]
    out = []
    for window, dil in DIL_PATTERNS:
        span = window // dil
        assert span <= DL_TILE
        dist = i + DL_TILE - c
        band = (dist >= 0) & (dist <= span)
        out.append(np.stack([band & (c >= DL_TILE), band]))
    return np.where(np.stack(out), 0.0, -np.inf).astype(np.float32)


def _rope_angles(positions, dim):
    inv = ROPE_THETA ** (-jnp.arange(0, dim, 2, dtype=F32) / dim)
    ang = positions.astype(F32)[..., None] * inv
    return jnp.cos(ang), jnp.sin(ang)


def _even_layer(x, h, gate, w_in, qn, kn, w_out, cosf, sinf, sb_u, dl_bias):
    b, s, d = x.shape
    m = b * s
    n_in = w_in.shape[1]
    w_part = n_in // 8
    n_heads = w_part // HEAD_DIM
    tm, tn, tk = _tile(s, 1024), _tile(w_part, 1024), _tile(d, 1024)
    q_scale = 1.0 / math.sqrt(HEAD_DIM)
    proj = _matmul(
        h.reshape(m, d), w_in.astype(BF16), tm=tm, tn=tn, tk=tk,
        extras=(qn.reshape(1, HEAD_DIM), kn.reshape(1, HEAD_DIM), cosf, sinf),
        extra_specs=[pl.BlockSpec((1, HEAD_DIM), lambda i, j, k: (0, 0)),
                     pl.BlockSpec((1, HEAD_DIM), lambda i, j, k: (0, 0)),
                     pl.BlockSpec((tm, HEAD_DIM), lambda i, j, k: (i, 0)),
                     pl.BlockSpec((tm, HEAD_DIM), lambda i, j, k: (i, 0))],
        out_shapes=[jax.ShapeDtypeStruct((m, n_in), BF16)],
        out_specs=[pl.BlockSpec((tm, tn), lambda i, j, k: (i, j))],
        epilogue=functools.partial(_epi_even, tn=tn, part_tiles=w_part // tn, q_scale=q_scale),
        name="even_in_proj")[0].reshape(b, s, n_in)

    hb = n_heads
    w4 = (HEAD_DIM,) * 4
    o_sb = _attn_call(
        functools.partial(_sb_kernel, tile=SB_TILE), (proj,) * 4, (0, hb, 2 * hb, 3 * hb), w4,
        (sb_u,), w_part, 0, n_heads, SB_TILE,
        [pltpu.VMEM((SB_TILE, HEAD_DIM), F32), pltpu.VMEM((SB_TILE, 1), F32)], "stick_breaking")
    assert s % DL_CHUNK == 0
    n_pat = len(DIL_PATTERNS)
    o_dl = _attn_call(
        functools.partial(_dl_kernel, chunk=DL_CHUNK), (proj,) * 4,
        (4 * hb, 5 * hb, 6 * hb, 7 * hb), w4, (dl_bias,), w_part, 0, n_heads, DL_CHUNK,
        [pltpu.VMEM((DL_CHUNK, HEAD_DIM), F32), pltpu.VMEM((s, HEAD_DIM), F32),
         pltpu.VMEM((s, HEAD_DIM), F32), pltpu.VMEM((n_pat, DL_CHUNK, HEAD_DIM), F32),
         pltpu.VMEM((n_pat, DL_CHUNK, LANES), F32), pltpu.VMEM((n_pat, DL_CHUNK, LANES), F32)],
        "dilated_window")
    mixed = jnp.concatenate([o_sb, o_dl], axis=-1).reshape(m, 2 * w_part)
    return _out_proj(x, mixed, w_out, gate)


def _out_proj(x, mixed, w_out, gate):
    b, s, d = x.shape
    m = b * s
    kdim = mixed.shape[1]
    tm, tn, tk = _tile(s, 1024), _tile(d, 1024), _tile(kdim, 1024)
    rows_per_batch = s // tm
    return _matmul(
        mixed, w_out.astype(BF16), tm=tm, tn=tn, tk=tk,
        extras=(x.reshape(m, d), gate),
        extra_specs=[pl.BlockSpec((tm, tn), lambda i, j, k: (i, j)),
                     pl.BlockSpec((None, 1, tn), lambda i, j, k: (i // rows_per_batch, 0, j))],
        out_shapes=[jax.ShapeDtypeStruct((m, d), F32)],
        out_specs=[pl.BlockSpec((tm, tn), lambda i, j, k: (i, j))],
        epilogue=_epi_resid, name="out_proj")[0].reshape(b, s, d)


def _odd_layer(x, h, gate, w_in, q_lat_norm, kv_lat_norm, w_uq, w_ukv, q_norm, k_norm, w_out,
               rc, rs1, rs2):
    b, s, d = x.shape
    m = b * s
    q_lora = q_lat_norm.shape[0]
    kv_lora = kv_lat_norm.shape[0]
    n_heads = w_uq.shape[1] // QK_HEAD
    lat_raw = q_lora + kv_lora + QK_ROPE
    n_lat = -(-lat_raw // LANES) * LANES
    h2 = h.reshape(m, d)
    tm, tk = _tile(s, 1024), _tile(d, 1024)

    w_lat = jnp.pad(w_in[:, :lat_raw], ((0, 0), (0, n_lat - lat_raw))).astype(BF16)
    tm_lat = _tile(s, 512)
    lat = _matmul(h2, w_lat, tm=tm_lat, tn=n_lat, tk=tk, extras=(), extra_specs=[],
                  out_shapes=[jax.ShapeDtypeStruct((m, n_lat), F32)],
                  out_specs=[pl.BlockSpec((tm_lat, n_lat), lambda i, j, k: (i, j))],
                  epilogue=_epi_plain, name="odd_latent_proj")[0]
    n_g = w_in.shape[1] - lat_raw
    tn_g = _tile(n_g, 1024)
    g = _matmul(h2, w_in[:, lat_raw:].astype(BF16), tm=tm, tn=tn_g, tk=tk, extras=(), extra_specs=[],
                out_shapes=[jax.ShapeDtypeStruct((m, n_g), BF16)],
                out_specs=[pl.BlockSpec((tm, tn_g), lambda i, j, k: (i, j))],
                epilogue=_epi_plain, name="odd_gate_proj")[0]

    pad = MLA_QK_PAD - QK_HEAD
    w_uq_p = jnp.pad(w_uq.reshape(q_lora, n_heads, QK_HEAD), ((0, 0), (0, 0), (0, pad)))
    w_uq_p = w_uq_p.reshape(q_lora, n_heads * MLA_QK_PAD).astype(BF16)
    qw = jnp.pad(q_norm, (0, pad)).reshape(1, MLA_QK_PAD)
    tn_q = _tile(n_heads * MLA_QK_PAD, 1024)
    rope_specs = [pl.BlockSpec((tm, LANES), lambda i, j, k: (i, 0))] * 3
    q = _matmul(
        lat, w_uq_p, tm=tm, tn=tn_q, tk=q_lora,
        extras=(q_lat_norm.reshape(1, q_lora), qw, rc, rs1, rs2),
        extra_specs=[pl.BlockSpec((1, q_lora), lambda i, j, k: (0, 0)),
                     pl.BlockSpec((1, MLA_QK_PAD), lambda i, j, k: (0, 0))] + rope_specs,
        out_shapes=[jax.ShapeDtypeStruct((m, n_heads * MLA_QK_PAD), BF16)],
        out_specs=[pl.BlockSpec((tm, tn_q), lambda i, j, k: (i, j))],
        epilogue=functools.partial(_epi_mla_q, tn=tn_q, q_scale=1.0 / math.sqrt(QK_HEAD)),
        prologue=_pro_rms, name="mla_q_proj")[0]

    assert q_lora % kv_lora == 0 and (q_lora + kv_lora) % LANES == 0
    kvw = QK_NOPE + V_HEAD
    tn_kv = _tile(n_heads * kvw, 1024)
    heads_per_tile = tn_kv // kvw
    kwn = k_norm[:QK_NOPE].reshape(1, QK_NOPE)
    kwp = jnp.pad(k_norm[QK_NOPE:], (0, LANES - QK_ROPE)).reshape(1, LANES)
    kpe_block = (q_lora + kv_lora) // LANES
    k, v = _matmul(
        lat, w_ukv.astype(BF16), tm=tm, tn=tn_kv, tk=kv_lora,
        extras=(kv_lat_norm.reshape(1, kv_lora), lat, kwn, kwp, rc, rs1, rs2),
        extra_specs=[pl.BlockSpec((1, kv_lora), lambda i, j, k: (0, 0)),
                     pl.BlockSpec((tm, LANES), lambda i, j, k: (i, kpe_block)),
                     pl.BlockSpec((1, QK_NOPE), lambda i, j, k: (0, 0)),
                     pl.BlockSpec((1, LANES), lambda i, j, k: (0, 0))] + rope_specs,
        out_shapes=[jax.ShapeDtypeStruct((m, n_heads * MLA_QK_PAD), BF16),
                    jax.ShapeDtypeStruct((m, n_heads * V_HEAD), BF16)],
        out_specs=[pl.BlockSpec((tm, heads_per_tile * MLA_QK_PAD), lambda i, j, k: (i, j)),
                   pl.BlockSpec((tm, heads_per_tile * V_HEAD), lambda i, j, k: (i, j))],
        epilogue=functools.partial(_epi_mla_kv, tn=tn_kv),
        prologue=_pro_rms, x_col_block=q_lora // kv_lora, name="mla_kv_proj")
    mixed = _attn_call(
        functools.partial(_mla_kernel, tile=MLA_TILE),
        (q.reshape(b, s, -1), k.reshape(b, s, -1), v.reshape(b, s, -1), g.reshape(b, s, -1)),
        (0, 0, 0, 0), (MLA_QK_PAD, MLA_QK_PAD, V_HEAD, V_HEAD), (), n_heads * V_HEAD, 0, n_heads,
        MLA_TILE,
        [pltpu.VMEM((MLA_TILE, MLA_TILE), F32), pltpu.VMEM((MLA_TILE, MLA_TILE), F32),
         pltpu.VMEM((MLA_TILE, V_HEAD), F32), pltpu.VMEM((MLA_TILE, LANES), F32),
         pltpu.VMEM((MLA_TILE, LANES), F32)], "mla_attention")
    return _out_proj(x, mixed.reshape(m, -1), w_out, gate)


def kernel(x, c, positions, ada_w, ada_b, norm_w, ev_w_in, ev_q_norm, ev_k_norm, ev_w_out, od_w_in,
           od_q_lat_norm, od_kv_lat_norm, od_w_uq, od_w_ukv, od_q_norm, od_k_norm, od_w_out):
    b, s, d = x.shape
    depth = ada_w.shape[0]
    m = b * s

    cos_f, sin_f = _rope_angles(positions, HEAD_DIM)
    cosf = jnp.concatenate([cos_f, cos_f], axis=-1).reshape(m, HEAD_DIM)
    sinf = jnp.concatenate([-sin_f, sin_f], axis=-1).reshape(m, HEAD_DIM)
    cos_m, sin_m = _rope_angles(positions, QK_ROPE)
    z32 = jnp.zeros_like(cos_m)
    z64 = jnp.concatenate([z32, z32], axis=-1)
    rc = jnp.concatenate([cos_m, cos_m, z64], axis=-1).reshape(m, LANES)
    rs1 = jnp.concatenate([z32, sin_m, z64], axis=-1).reshape(m, LANES)
    rs2 = jnp.concatenate([-sin_m, z32, z64], axis=-1).reshape(m, LANES)

    sb_u = jnp.asarray(np.tril(np.ones((SB_TILE, SB_TILE), np.float32), -1), BF16)
    dl_bias = jnp.asarray(_dl_band_bias())

    mod = _modulation(c, ada_w, ada_b)
    shift, scale, gate = (mod[:, :, i * d:(i + 1) * d].reshape(depth, b, 1, d) for i in range(3))

    for layer in range(depth):
        h = _norm_mod(x, norm_w[layer], scale[layer], shift[layer])
        i = layer // 2
        if layer % 2 == 0:
            x = _even_layer(x, h, gate[layer], ev_w_in[i], ev_q_norm[i], ev_k_norm[i], ev_w_out[i],
                            cosf, sinf, sb_u, dl_bias)
        else:
            x = _odd_layer(x, h, gate[layer], od_w_in[i], od_q_lat_norm[i], od_kv_lat_norm[i],
                           od_w_uq[i], od_w_ukv[i], od_q_norm[i], od_k_norm[i], od_w_out[i],
                           rc, rs1, rs2)
    return x
```

```python
import functools
import math

import numpy as np
import jax
import jax.numpy as jnp
from jax import lax
from jax.experimental import pallas as pl
from jax.experimental.pallas import tpu as pltpu

F32 = jnp.float32
BF16 = jnp.bfloat16

HEAD_DIM = 128
DIL_PATTERNS = ((128, 1), (512, 4), (2048, 16))
KV_LORA = 512
QK_NOPE = 128
QK_ROPE = 64
QK_HEAD = QK_NOPE + QK_ROPE
V_HEAD = 128
MLA_QK_PAD = 256
ROPE_THETA = 10000.0
EPS = 1e-6

LANES = 128
SUBLANES = 8
VMEM_LIMIT = 56 * 1024 * 1024
EXP_UNDERFLOW = -104.0
SB_TILE = 256
SB_SUBTILES = 2
DL_TILE = 128
DL_CHUNK = DL_TILE * max(d for _, d in DIL_PATTERNS)
MLA_TILE = 512
MM_TM = 1024
MM_TN = 512


def _cparams(sem):
    return pltpu.CompilerParams(dimension_semantics=sem, vmem_limit_bytes=VMEM_LIMIT)


def _tile(n, want):
    if n <= want:
        return n
    t = (want // LANES) * LANES
    while n % t:
        t -= LANES
    return t


def _mod_kernel(c_ref, w_ref, b_ref, o_ref):
    a = c_ref[...]
    a = (a * jax.nn.sigmoid(a)).astype(BF16)
    o_ref[...] = jnp.dot(a, w_ref[...].astype(BF16), preferred_element_type=F32) + b_ref[...]


def _modulation(c, ada_w, ada_b):
    depth, d, n3 = ada_w.shape
    b = c.shape[0]
    rows = -(-b // SUBLANES) * SUBLANES
    c_pad = jnp.pad(c, ((0, rows - b), (0, 0)))
    tn = _tile(n3, 512)
    out = pl.pallas_call(
        _mod_kernel,
        grid=(depth, n3 // tn),
        in_specs=[pl.BlockSpec((rows, d), lambda l, j: (0, 0)),
                  pl.BlockSpec((None, d, tn), lambda l, j: (l, 0, j)),
                  pl.BlockSpec((None, 1, tn), lambda l, j: (l, 0, j))],
        out_specs=pl.BlockSpec((None, rows, tn), lambda l, j: (l, 0, j)),
        out_shape=jax.ShapeDtypeStruct((depth, rows, n3), F32),
        compiler_params=_cparams(("parallel", "parallel")),
        name="modulation",
    )(c_pad, ada_w, ada_b.reshape(depth, 1, n3))
    return out[:, :b]


def _norm_mod_kernel(x_ref, w_ref, scale_ref, shift_ref, o_ref):
    x = x_ref[...]
    y = x * lax.rsqrt(jnp.mean(x * x, axis=-1, keepdims=True) + EPS)
    y = y * w_ref[...]
    o_ref[...] = (y * (1.0 + scale_ref[...]) + shift_ref[...]).astype(o_ref.dtype)


def _norm_mod(x, w, scale, shift):
    b, s, d = x.shape
    ts = _tile(s, 256)
    return pl.pallas_call(
        _norm_mod_kernel,
        grid=(b, s // ts),
        in_specs=[pl.BlockSpec((None, ts, d), lambda bi, i: (bi, i, 0)),
                  pl.BlockSpec((1, d), lambda bi, i: (0, 0)),
                  pl.BlockSpec((None, 1, d), lambda bi, i: (bi, 0, 0)),
                  pl.BlockSpec((None, 1, d), lambda bi, i: (bi, 0, 0))],
        out_specs=pl.BlockSpec((None, ts, d), lambda bi, i: (bi, i, 0)),
        out_shape=jax.ShapeDtypeStruct((b, s, d), BF16),
        compiler_params=_cparams(("parallel", "parallel")),
        name="norm_mod",
    )(x, w.reshape(1, d), scale, shift)


def _mm_kernel(*refs, n_extra, n_out, nk, epilogue, prologue):
    x_ref, w_ref = refs[0], refs[1]
    extras = refs[2:2 + n_extra]
    outs = refs[2 + n_extra:2 + n_extra + n_out]
    scratch = refs[2 + n_extra + n_out:]
    acc_ref = scratch[0]
    if prologue is not None:
        h_ref = scratch[1]

        @pl.when(pl.program_id(1) == 0)
        def _():
            h_ref[...] = prologue(x_ref, extras)

        acc_ref[...] = jnp.dot(h_ref[...], w_ref[...], preferred_element_type=F32)
        epilogue(acc_ref, extras, outs)
        return
    if nk == 1:
        acc_ref[...] = jnp.dot(x_ref[...], w_ref[...], preferred_element_type=F32)
        epilogue(acc_ref, extras, outs)
        return
    k = pl.program_id(2)

    @pl.when(k == 0)
    def _():
        acc_ref[...] = jnp.zeros_like(acc_ref)

    acc_ref[...] += jnp.dot(x_ref[...], w_ref[...], preferred_element_type=F32)

    @pl.when(k == nk - 1)
    def _():
        epilogue(acc_ref, extras, outs)


def _matmul(x, w, *, tm, tn, tk, extras, extra_specs, out_shapes, out_specs, epilogue,
            prologue=None, x_col_block=0, name):
    m = x.shape[0]
    kdim, n = w.shape
    nk = kdim // tk
    assert m % tm == 0 and n % tn == 0 and kdim % tk == 0 and x.shape[1] >= (x_col_block + nk) * tk
    scratch = [pltpu.VMEM((tm, tn), F32)]
    if prologue is not None:
        assert nk == 1
        scratch.append(pltpu.VMEM((tm, kdim), BF16))
    kern = functools.partial(_mm_kernel, n_extra=len(extras), n_out=len(out_shapes), nk=nk,
                             epilogue=epilogue, prologue=prologue)
    return pl.pallas_call(
        kern,
        grid=(m // tm, n // tn, nk),
        in_specs=[pl.BlockSpec((tm, tk), lambda i, j, k: (i, x_col_block + k)),
                  pl.BlockSpec((tk, tn), lambda i, j, k: (k, j))] + list(extra_specs),
        out_specs=list(out_specs),
        out_shape=list(out_shapes),
        scratch_shapes=scratch,
        compiler_params=_cparams(("parallel", "arbitrary", "arbitrary")),
        name=name,
    )(x, w, *extras)


def _mm_ws_kernel(*refs, k_parts, n_extra, n_out, cast, epilogue):
    n_x = len(k_parts)
    x_refs = refs[:n_x]
    w_ref = refs[n_x]
    extras = refs[n_x + 1:n_x + 1 + n_extra]
    outs = refs[n_x + 1 + n_extra:n_x + 1 + n_extra + n_out]
    scratch = refs[n_x + 1 + n_extra + n_out:]
    acc_ref = scratch[0]
    if cast:
        wb_ref = scratch[1]

        @pl.when(pl.program_id(1) == 0)
        def _():
            wb_ref[...] = w_ref[...].astype(BF16)
    else:
        wb_ref = w_ref
    acc = None
    k0 = 0
    for x_ref, kp in zip(x_refs, k_parts):
        part = jnp.dot(x_ref[...], wb_ref[k0:k0 + kp, :], preferred_element_type=F32)
        acc = part if acc is None else acc + part
        k0 += kp
    acc_ref[...] = acc
    epilogue(acc_ref, extras, outs)


def _matmul_ws(xs, w, *, tm, tn, extras, extra_specs, out_shapes, out_specs, epilogue, name):
    m = xs[0].shape[0]
    kdim, n = w.shape
    k_parts = tuple(x.shape[1] for x in xs)
    assert m % tm == 0 and n % tn == 0 and sum(k_parts) == kdim
    cast = w.dtype != BF16
    scratch = [pltpu.VMEM((tm, tn), F32)]
    if cast:
        scratch.append(pltpu.VMEM((kdim, tn), BF16))
    kern = functools.partial(_mm_ws_kernel, k_parts=k_parts, n_extra=len(extras),
                             n_out=len(out_shapes), cast=cast, epilogue=epilogue)
    x_specs = [pl.BlockSpec((tm, kp), lambda j, i: (i, 0)) for kp in k_parts]
    return pl.pallas_call(
        kern,
        grid=(n // tn, m // tm),
        in_specs=x_specs + [pl.BlockSpec((kdim, tn), lambda j, i: (0, j))] + list(extra_specs),
        out_specs=list(out_specs),
        out_shape=list(out_shapes),
        scratch_shapes=scratch,
        compiler_params=_cparams(("parallel", "arbitrary")),
        name=name,
    )(*xs, w, *extras)


def _epi_plain(acc_ref, extras, outs):
    outs[0][...] = acc_ref[...].astype(outs[0].dtype)


def _epi_resid(acc_ref, extras, outs):
    xres_ref, gate_ref = extras
    outs[0][...] = xres_ref[...] + gate_ref[...] * acc_ref[...]


def _rope_full(y, cos_ref, sin_ref):
    return y * cos_ref[...] + pltpu.roll(y, HEAD_DIM // 2, 1) * sin_ref[...]


def _rope_mla(r, c_ref, s1_ref, s2_ref):
    half = QK_ROPE // 2
    return r * c_ref[...] + pltpu.roll(r, half, 1) * s1_ref[...] + pltpu.roll(r, LANES - half, 1) * s2_ref[...]


def _epi_even(acc_ref, extras, outs, *, tn, part_tiles, q_scale):
    qn_ref, kn_ref, cos_ref, sin_ref = extras
    o_ref = outs[0]
    part = pl.program_id(0) // part_tiles

    def headnorm_rope(w_ref, scale):
        for hh in range(tn // HEAD_DIM):
            sl = slice(hh * HEAD_DIM, (hh + 1) * HEAD_DIM)
            a = acc_ref[:, sl]
            y = a * lax.rsqrt(jnp.mean(a * a, axis=-1, keepdims=True) + EPS) * w_ref[...]
            y = _rope_full(y, cos_ref, sin_ref)
            if scale != 1.0:
                y = y * scale
            o_ref[:, sl] = y.astype(o_ref.dtype)

    @pl.when(part == 0)
    def _():
        o_ref[...] = (acc_ref[...] * q_scale).astype(o_ref.dtype)

    @pl.when(part == 4)
    def _():
        headnorm_rope(qn_ref, q_scale)

    @pl.when(part == 5)
    def _():
        headnorm_rope(kn_ref, 1.0)

    @pl.when((part != 0) & (part != 4) & (part != 5))
    def _():
        o_ref[...] = acc_ref[...].astype(o_ref.dtype)


def _pro_rms(x_ref, extras):
    x = x_ref[...]
    w_ref = extras[0]
    y = x * lax.rsqrt(jnp.mean(x * x, axis=-1, keepdims=True) + EPS) * w_ref[...]
    return y.astype(BF16)


def _epi_mla_q(acc_ref, extras, outs, *, tn, q_scale):
    _, qw_ref, c_ref, s1_ref, s2_ref = extras
    o_ref = outs[0]
    for hh in range(tn // MLA_QK_PAD):
        base = hh * MLA_QK_PAD
        a = acc_ref[:, base:base + MLA_QK_PAD]
        rstd = lax.rsqrt(jnp.sum(a * a, axis=-1, keepdims=True) * (1.0 / QK_HEAD) + EPS)
        y = a * rstd * qw_ref[...] * q_scale
        o_ref[:, base:base + QK_NOPE] = y[:, :QK_NOPE].astype(o_ref.dtype)
        r = _rope_mla(y[:, QK_NOPE:], c_ref, s1_ref, s2_ref)
        o_ref[:, base + QK_NOPE:base + MLA_QK_PAD] = r.astype(o_ref.dtype)


def _epi_mla_kv(acc_ref, extras, outs, *, tn):
    _, kpe_ref, kwn_ref, kwp_ref, c_ref, s1_ref, s2_ref = extras
    k_ref, v_ref = outs
    kpe = kpe_ref[...]
    ss_pe = jnp.sum(kpe * kpe, axis=-1, keepdims=True)
    kr = _rope_mla(kpe * kwp_ref[...], c_ref, s1_ref, s2_ref)
    for hh in range(tn // (QK_NOPE + V_HEAD)):
        base = hh * (QK_NOPE + V_HEAD)
        kn = acc_ref[:, base:base + QK_NOPE]
        rstd = lax.rsqrt((jnp.sum(kn * kn, axis=-1, keepdims=True) + ss_pe) * (1.0 / QK_HEAD) + EPS)
        kb = hh * MLA_QK_PAD
        k_ref[:, kb:kb + QK_NOPE] = (kn * rstd * kwn_ref[...]).astype(k_ref.dtype)
        k_ref[:, kb + QK_NOPE:kb + MLA_QK_PAD] = (kr * rstd).astype(k_ref.dtype)
        v_ref[:, hh * V_HEAD:(hh + 1) * V_HEAD] = acc_ref[:, base + QK_NOPE:base + QK_NOPE + V_HEAD].astype(v_ref.dtype)


def _qk(q, k):
    return lax.dot_general(q, k, (((1,), (1,)), ((), ())), preferred_element_type=F32)


def _sb_kernel(q_ref, k_ref, v_ref, g_ref, u_ref, o_ref, acc_ref, carry_ref, *, tile):
    qi = pl.program_id(2)
    n_sub = q_ref.shape[0] // tile

    def tile_rows(kb):
        return pl.ds(pl.multiple_of(kb * tile, tile), tile)

    def weights(q, kb, carry, diagonal=False, valid=None):
        z = _qk(q, k_ref[tile_rows(kb), :])
        sp = jnp.log1p(jnp.exp(-jnp.abs(z)))
        log_beta = jnp.minimum(z, 0.0) - sp
        log_keep = jnp.minimum(-z, 0.0) - sp
        keep = valid
        if diagonal:
            row = lax.broadcasted_iota(jnp.int32, z.shape, 0)
            col = lax.broadcasted_iota(jnp.int32, z.shape, 1)
            keep = col < row
        if keep is not None:
            log_keep = jnp.where(keep, log_keep, 0.0)
        hi = log_keep.astype(BF16)
        lo = (log_keep - hi.astype(F32)).astype(BF16)
        u = u_ref[...]
        excl = jnp.dot(hi, u, preferred_element_type=F32) + jnp.dot(lo, u, preferred_element_type=F32)
        if carry is not None:
            excl = excl + carry
        w = jnp.exp(log_beta + excl)
        if keep is not None:
            w = jnp.where(keep, w, 0.0)
        return w.astype(BF16), jnp.sum(log_keep, axis=-1, keepdims=True)

    def pv(w, kb):
        return jnp.dot(w, v_ref[tile_rows(kb), :], preferred_element_type=F32)

    tails = []
    for sub in range(n_sub):
        rows = slice(sub * tile, (sub + 1) * tile)
        q = q_ref[rows, :]
        diag = qi * n_sub + sub
        prev = jnp.maximum(diag - 1, 0)
        w_d, sum_d = weights(q, diag, None, diagonal=True)
        w_p, sum_p = weights(q, prev, sum_d, valid=diag > 0)
        carry = sum_d + sum_p
        acc_ref[rows, :] = pv(w_d, diag) + pv(w_p, prev)
        carry_ref[rows, :] = carry
        tails.append((rows, diag - 2, jnp.max(carry)))

    for rows, kb0, cmax0 in tails:
        def cond(state):
            kb, cmax = state
            return (kb >= 0) & (cmax > EXP_UNDERFLOW)

        def body(state, rows=rows):
            kb, _ = state
            carry = carry_ref[rows, :]
            w, sum_k = weights(q_ref[rows, :], kb, carry)
            acc_ref[rows, :] += pv(w, kb)
            carry = carry + sum_k
            carry_ref[rows, :] = carry
            return kb - 1, jnp.max(carry)

        lax.while_loop(cond, body, (kb0, cmax0))

    g = g_ref[...].astype(F32)
    o_ref[...] = (acc_ref[...] * (g * jax.nn.sigmoid(g))).astype(o_ref.dtype)


def _dl_kernel(q_ref, k_ref, v_ref, g_ref, bias_ref, o_ref, qf_ref, kf_ref, vf_ref, acc_ref, m_ref, l_ref,
               *, chunk):
    ci = pl.program_id(2)

    @pl.when(ci == 0)
    def _():
        kf_ref[...] = k_ref[...].astype(F32)
        vf_ref[...] = v_ref[...].astype(F32)

    qf_ref[...] = q_ref[...].astype(F32)
    base = ci * chunk

    def body(idx, _):
        for pi, (_, dil) in enumerate(DIL_PATTERNS):
            unit = DL_TILE * dil
            q0 = (idx // dil) * unit + idx % dil
            rows = pl.ds(q0, DL_TILE, stride=dil)
            cur0 = base + q0
            has_prev = cur0 >= unit
            prev0 = jnp.maximum(cur0 - unit, 0)
            prev_rows = pl.ds(prev0, DL_TILE, stride=dil)
            cur_rows = pl.ds(cur0, DL_TILE, stride=dil)
            kk = jnp.concatenate([kf_ref[prev_rows, :], kf_ref[cur_rows, :]], axis=0).astype(BF16)
            vv = jnp.concatenate([vf_ref[prev_rows, :], vf_ref[cur_rows, :]], axis=0).astype(BF16)
            s = _qk(qf_ref[rows, :].astype(BF16), kk) + bias_ref[pi, has_prev.astype(jnp.int32)]
            m = jnp.max(s, axis=-1, keepdims=True)
            p = jnp.exp(s - m)
            l = jnp.sum(p, axis=-1, keepdims=True)
            acc_ref[pi, rows, :] = jnp.dot(p.astype(BF16), vv, preferred_element_type=F32)
            m_ref[pi, rows, :] = jnp.broadcast_to(m, (DL_TILE, LANES))
            l_ref[pi, rows, :] = jnp.broadcast_to(l, (DL_TILE, LANES))
        return 0

    lax.fori_loop(0, chunk // DL_TILE, body, 0, unroll=8)

    n_pat = len(DIL_PATTERNS)
    m_all = functools.reduce(jnp.maximum, [m_ref[pi] for pi in range(n_pat)])
    num = jnp.zeros(acc_ref.shape[1:], F32)
    den = jnp.zeros(acc_ref.shape[1:], F32)
    for pi in range(n_pat):
        w = jnp.exp(m_ref[pi] - m_all)
        num = num + w * acc_ref[pi]
        den = den + w * l_ref[pi]
    g = g_ref[...].astype(F32)
    o_ref[...] = (num / den * (g * jax.nn.sigmoid(g))).astype(o_ref.dtype)


def _softmax_update(s_ref, v, acc_ref, m_ref, l_ref, masked):
    tq, tk = s_ref.shape
    cols = [s_ref[:, c * LANES:(c + 1) * LANES] for c in range(tk // LANES)]
    if masked:
        row = lax.broadcasted_iota(jnp.int32, (tq, LANES), 0)
        lane = lax.broadcasted_iota(jnp.int32, (tq, LANES), 1)
        cols = [jnp.where(lane + c * LANES <= row, col, -jnp.inf) for c, col in enumerate(cols)]
    m_old = m_ref[...]
    tile_max = jnp.max(functools.reduce(jnp.maximum, cols), axis=-1, keepdims=True)
    m_new = jnp.maximum(m_old, tile_max)
    ps = [jnp.exp(col - m_new) for col in cols]
    alpha = jnp.exp(m_old - m_new)
    l_ref[...] = alpha * l_ref[...] + jnp.sum(functools.reduce(jnp.add, ps), axis=-1, keepdims=True)
    p = jnp.concatenate([x.astype(BF16) for x in ps], axis=1)
    acc_ref[...] = alpha * acc_ref[...] + jnp.dot(p, v, preferred_element_type=F32)
    m_ref[...] = m_new


def _mla_kernel(q_ref, k_ref, v_ref, g_ref, o_ref, sa_ref, sb_ref, acc_ref, m_ref, l_ref, *, tile):
    qi = pl.program_id(2)
    acc_ref[...] = jnp.zeros_like(acc_ref)
    m_ref[...] = jnp.full_like(m_ref, -jnp.inf)
    l_ref[...] = jnp.zeros_like(l_ref)

    def rows(kb):
        return pl.ds(pl.multiple_of(kb * tile, tile), tile)

    def scores(kb, dst_ref):
        dst_ref[...] = _qk(q_ref[...], k_ref[rows(kb), :])

    def update(kb, src_ref, masked):
        _softmax_update(src_ref, v_ref[rows(kb), :], acc_ref, m_ref, l_ref, masked)

    scores(0, sa_ref)

    def pair(jj, _):
        j = 2 * jj
        scores(j + 1, sb_ref)
        update(j, sa_ref, False)
        scores(j + 2, sa_ref)
        update(j + 1, sb_ref, False)
        return 0

    lax.fori_loop(0, qi // 2, pair, 0)

    @pl.when(qi % 2 == 1)
    def _():
        scores(qi, sb_ref)
        update(qi - 1, sa_ref, False)
        update(qi, sb_ref, True)

    @pl.when(qi % 2 == 0)
    def _():
        update(qi, sa_ref, True)

    g = g_ref[...].astype(F32)
    o_ref[...] = (acc_ref[...] / l_ref[...] * (g * jax.nn.sigmoid(g))).astype(o_ref.dtype)


def _attn_call(kern, arrays, col_blocks, widths, consts, out_width_total, out_col0, n_heads, tile,
               scratch, name):
    q, k, v, g = arrays
    b, s, _ = q.shape
    cq, ck, cv, cg = col_blocks
    wq, wk, wv, wg = widths
    in_specs = [
        pl.BlockSpec((None, tile, wq), lambda bi, h, i: (bi, i, cq + h)),
        pl.BlockSpec((None, s, wk), lambda bi, h, i: (bi, 0, ck + h)),
        pl.BlockSpec((None, s, wv), lambda bi, h, i: (bi, 0, cv + h)),
        pl.BlockSpec((None, tile, wg), lambda bi, h, i: (bi, i, cg + h)),
    ]
    for cst in consts:
        in_specs.append(pl.BlockSpec(cst.shape, lambda bi, h, i, nd=cst.ndim: (0,) * nd))
    return pl.pallas_call(
        kern,
        grid=(b, n_heads, s // tile),
        in_specs=in_specs,
        out_specs=pl.BlockSpec((None, tile, wv), lambda bi, h, i: (bi, i, out_col0 + h)),
        out_shape=jax.ShapeDtypeStruct((b, s, out_width_total), BF16),
        scratch_shapes=scratch,
        compiler_params=_cparams(("parallel", "parallel", "arbitrary")),
        name=name,
    )(q, k, v, g, *consts)


def _dl_band_bias():
    i = np.arange(DL_TILE)[:, None]
    c = np.arange(2 * DL_TILE)[None, :]
    out = []
    for window, dil in DIL_PATTERNS:
        span = window // dil
        assert span <= DL_TILE
        dist = i + DL_TILE - c
        band = (dist >= 0) & (dist <= span)
        out.append(np.stack([band & (c >= DL_TILE), band]))
    return np.where(np.stack(out), 0.0, -np.inf).astype(np.float32)


def _rope_angles(positions, dim):
    inv = ROPE_THETA ** (-jnp.arange(0, dim, 2, dtype=F32) / dim)
    ang = positions.astype(F32)[..., None] * inv
    return jnp.cos(ang), jnp.sin(ang)


def _even_layer(x, h, gate, w_in, qn, kn, w_out, cosf, sinf, sb_u, dl_bias):
    b, s, d = x.shape
    m = b * s
    n_in = w_in.shape[1]
    w_part = n_in // 8
    n_heads = w_part // HEAD_DIM
    tm, tn = _tile(s, MM_TM), _tile(w_part, MM_TN)
    q_scale = 1.0 / math.sqrt(HEAD_DIM)
    proj = _matmul_ws(
        [h.reshape(m, d)], w_in, tm=tm, tn=tn,
        extras=(qn.reshape(1, HEAD_DIM), kn.reshape(1, HEAD_DIM), cosf, sinf),
        extra_specs=[pl.BlockSpec((1, HEAD_DIM), lambda j, i: (0, 0)),
                     pl.BlockSpec((1, HEAD_DIM), lambda j, i: (0, 0)),
                     pl.BlockSpec((tm, HEAD_DIM), lambda j, i: (i, 0)),
                     pl.BlockSpec((tm, HEAD_DIM), lambda j, i: (i, 0))],
        out_shapes=[jax.ShapeDtypeStruct((m, n_in), BF16)],
        out_specs=[pl.BlockSpec((tm, tn), lambda j, i: (i, j))],
        epilogue=functools.partial(_epi_even, tn=tn, part_tiles=w_part // tn, q_scale=q_scale),
        name="even_in_proj")[0].reshape(b, s, n_in)

    hb = n_heads
    w4 = (HEAD_DIM,) * 4
    sb_rows = SB_TILE * SB_SUBTILES
    o_sb = _attn_call(
        functools.partial(_sb_kernel, tile=SB_TILE), (proj,) * 4, (0, hb, 2 * hb, 3 * hb), w4,
        (sb_u,), w_part, 0, n_heads, sb_rows,
        [pltpu.VMEM((sb_rows, HEAD_DIM), F32), pltpu.VMEM((sb_rows, 1), F32)], "stick_breaking")
    assert s % DL_CHUNK == 0
    n_pat = len(DIL_PATTERNS)
    o_dl = _attn_call(
        functools.partial(_dl_kernel, chunk=DL_CHUNK), (proj,) * 4,
        (4 * hb, 5 * hb, 6 * hb, 7 * hb), w4, (dl_bias,), w_part, 0, n_heads, DL_CHUNK,
        [pltpu.VMEM((DL_CHUNK, HEAD_DIM), F32), pltpu.VMEM((s, HEAD_DIM), F32),
         pltpu.VMEM((s, HEAD_DIM), F32), pltpu.VMEM((n_pat, DL_CHUNK, HEAD_DIM), F32),
         pltpu.VMEM((n_pat, DL_CHUNK, LANES), F32), pltpu.VMEM((n_pat, DL_CHUNK, LANES), F32)],
        "dilated_window")
    return _out_proj(x, [o_sb.reshape(m, w_part), o_dl.reshape(m, w_part)], w_out, gate)


def _out_proj(x, mixed_parts, w_out, gate):
    b, s, d = x.shape
    m = b * s
    tm, tn = _tile(s, MM_TM), _tile(d, MM_TN)
    rows_per_batch = s // tm
    return _matmul_ws(
        mixed_parts, w_out, tm=tm, tn=tn,
        extras=(x.reshape(m, d), gate),
        extra_specs=[pl.BlockSpec((tm, tn), lambda j, i: (i, j)),
                     pl.BlockSpec((None, 1, tn), lambda j, i: (i // rows_per_batch, 0, j))],
        out_shapes=[jax.ShapeDtypeStruct((m, d), F32)],
        out_specs=[pl.BlockSpec((tm, tn), lambda j, i: (i, j))],
        epilogue=_epi_resid, name="out_proj")[0].reshape(b, s, d)


def _odd_layer(x, h, gate, w_in, q_lat_norm, kv_lat_norm, w_uq, w_ukv, q_norm, k_norm, w_out,
               rc, rs1, rs2):
    b, s, d = x.shape
    m = b * s
    q_lora = q_lat_norm.shape[0]
    kv_lora = kv_lat_norm.shape[0]
    n_heads = w_uq.shape[1] // QK_HEAD
    lat_raw = q_lora + kv_lora + QK_ROPE
    n_lat = -(-lat_raw // LANES) * LANES
    h2 = h.reshape(m, d)
    tm, tk = _tile(s, 1024), _tile(d, 1024)

    w_lat = jnp.pad(w_in[:, :lat_raw], ((0, 0), (0, n_lat - lat_raw))).astype(BF16)
    tm_lat = _tile(s, 512)
    lat = _matmul(h2, w_lat, tm=tm_lat, tn=n_lat, tk=tk, extras=(), extra_specs=[],
                  out_shapes=[jax.ShapeDtypeStruct((m, n_lat), F32)],
                  out_specs=[pl.BlockSpec((tm_lat, n_lat), lambda i, j, k: (i, j))],
                  epilogue=_epi_plain, name="odd_latent_proj")[0]
    n_g = w_in.shape[1] - lat_raw
    tn_g = _tile(n_g, MM_TN)
    g = _matmul_ws([h2], w_in[:, lat_raw:].astype(BF16), tm=tm, tn=tn_g, extras=(), extra_specs=[],
                   out_shapes=[jax.ShapeDtypeStruct((m, n_g), BF16)],
                   out_specs=[pl.BlockSpec((tm, tn_g), lambda j, i: (i, j))],
                   epilogue=_epi_plain, name="odd_gate_proj")[0]

    pad = MLA_QK_PAD - QK_HEAD
    w_uq_p = jnp.pad(w_uq.reshape(q_lora, n_heads, QK_HEAD), ((0, 0), (0, 0), (0, pad)))
    w_uq_p = w_uq_p.reshape(q_lora, n_heads * MLA_QK_PAD).astype(BF16)
    qw = jnp.pad(q_norm, (0, pad)).reshape(1, MLA_QK_PAD)
    tn_q = _tile(n_heads * MLA_QK_PAD, 1024)
    rope_specs = [pl.BlockSpec((tm, LANES), lambda i, j, k: (i, 0))] * 3
    q = _matmul(
        lat, w_uq_p, tm=tm, tn=tn_q, tk=q_lora,
        extras=(q_lat_norm.reshape(1, q_lora), qw, rc, rs1, rs2),
        extra_specs=[pl.BlockSpec((1, q_lora), lambda i, j, k: (0, 0)),
                     pl.BlockSpec((1, MLA_QK_PAD), lambda i, j, k: (0, 0))] + rope_specs,
        out_shapes=[jax.ShapeDtypeStruct((m, n_heads * MLA_QK_PAD), BF16)],
        out_specs=[pl.BlockSpec((tm, tn_q), lambda i, j, k: (i, j))],
        epilogue=functools.partial(_epi_mla_q, tn=tn_q, q_scale=1.0 / math.sqrt(QK_HEAD)),
        prologue=_pro_rms, name="mla_q_proj")[0]

    assert q_lora % kv_lora == 0 and (q_lora + kv_lora) % LANES == 0
    kvw = QK_NOPE + V_HEAD
    tn_kv = _tile(n_heads * kvw, 1024)
    heads_per_tile = tn_kv // kvw
    kwn = k_norm[:QK_NOPE].reshape(1, QK_NOPE)
    kwp = jnp.pad(k_norm[QK_NOPE:], (0, LANES - QK_ROPE)).reshape(1, LANES)
    kpe_block = (q_lora + kv_lora) // LANES
    k, v = _matmul(
        lat, w_ukv.astype(BF16), tm=tm, tn=tn_kv, tk=kv_lora,
        extras=(kv_lat_norm.reshape(1, kv_lora), lat, kwn, kwp, rc, rs1, rs2),
        extra_specs=[pl.BlockSpec((1, kv_lora), lambda i, j, k: (0, 0)),
                     pl.BlockSpec((tm, LANES), lambda i, j, k: (i, kpe_block)),
                     pl.BlockSpec((1, QK_NOPE), lambda i, j, k: (0, 0)),
                     pl.BlockSpec((1, LANES), lambda i, j, k: (0, 0))] + rope_specs,
        out_shapes=[jax.ShapeDtypeStruct((m, n_heads * MLA_QK_PAD), BF16),
                    jax.ShapeDtypeStruct((m, n_heads * V_HEAD), BF16)],
        out_specs=[pl.BlockSpec((tm, heads_per_tile * MLA_QK_PAD), lambda i, j, k: (i, j)),
                   pl.BlockSpec((tm, heads_per_tile * V_HEAD), lambda i, j, k: (i, j))],
        epilogue=functools.partial(_epi_mla_kv, tn=tn_kv),
        prologue=_pro_rms, x_col_block=q_lora // kv_lora, name="mla_kv_proj")
    mixed = _attn_call(
        functools.partial(_mla_kernel, tile=MLA_TILE),
        (q.reshape(b, s, -1), k.reshape(b, s, -1), v.reshape(b, s, -1), g.reshape(b, s, -1)),
        (0, 0, 0, 0), (MLA_QK_PAD, MLA_QK_PAD, V_HEAD, V_HEAD), (), n_heads * V_HEAD, 0, n_heads,
        MLA_TILE,
        [pltpu.VMEM((MLA_TILE, MLA_TILE), F32), pltpu.VMEM((MLA_TILE, MLA_TILE), F32),
         pltpu.VMEM((MLA_TILE, V_HEAD), F32), pltpu.VMEM((MLA_TILE, LANES), F32),
         pltpu.VMEM((MLA_TILE, LANES), F32)], "mla_attention")
    return _out_proj(x, [mixed.reshape(m, -1)], w_out, gate)


def kernel(x, c, positions, ada_w, ada_b, norm_w, ev_w_in, ev_q_norm, ev_k_norm, ev_w_out, od_w_in,
           od_q_lat_norm, od_kv_lat_norm, od_w_uq, od_w_ukv, od_q_norm, od_k_norm, od_w_out):
    b, s, d = x.shape
    depth = ada_w.shape[0]
    m = b * s

    cos_f, sin_f = _rope_angles(positions, HEAD_DIM)
    cosf = jnp.concatenate([cos_f, cos_f], axis=-1).reshape(m, HEAD_DIM)
    sinf = jnp.concatenate([-sin_f, sin_f], axis=-1).reshape(m, HEAD_DIM)
    cos_m, sin_m = _rope_angles(positions, QK_ROPE)
    z32 = jnp.zeros_like(cos_m)
    z64 = jnp.concatenate([z32, z32], axis=-1)
    rc = jnp.concatenate([cos_m, cos_m, z64], axis=-1).reshape(m, LANES)
    rs1 = jnp.concatenate([z32, sin_m, z64], axis=-1).reshape(m, LANES)
    rs2 = jnp.concatenate([-sin_m, z32, z64], axis=-1).reshape(m, LANES)

    sb_u = jnp.asarray(np.tril(np.ones((SB_TILE, SB_TILE), np.float32), -1), BF16)
    dl_bias = jnp.asarray(_dl_band_bias())

    mod = _modulation(c, ada_w, ada_b)
    shift, scale, gate = (mod[:, :, i * d:(i + 1) * d].reshape(depth, b, 1, d) for i in range(3))

    for layer in range(depth):
        h = _norm_mod(x, norm_w[layer], scale[layer], shift[layer])
        i = layer // 2
        if layer % 2 == 0:
            x = _even_layer(x, h, gate[layer], ev_w_in[i], ev_q_norm[i], ev_k_norm[i], ev_w_out[i],
                            cosf, sinf, sb_u, dl_bias)
        else:
            x = _odd_layer(x, h, gate[layer], od_w_in[i], od_q_lat_norm[i], od_kv_lat_norm[i],
                           od_w_uq[i], od_w_ukv[i], od_q_norm[i], od_k_norm[i], od_w_out[i],
                           rc, rs1, rs2)
    return x
```

```python
import functools
import math

import numpy as np
import jax
import jax.numpy as jnp
from jax import lax
from jax.experimental import pallas as pl
from jax.experimental.pallas import tpu as pltpu

F32 = jnp.float32
BF16 = jnp.bfloat16

HEAD_DIM = 128
DIL_PATTERNS = ((128, 1), (512, 4), (2048, 16))
KV_LORA = 512
QK_NOPE = 128
QK_ROPE = 64
QK_HEAD = QK_NOPE + QK_ROPE
V_HEAD = 128
MLA_QK_PAD = 256
ROPE_THETA = 10000.0
EPS = 1e-6

LANES = 128
SUBLANES = 8
MXU_COLS = 256
VMEM_LIMIT = 56 * 1024 * 1024
EXP_UNDERFLOW = -104.0
LOG2_E = math.log2(math.e)
SB_TILE = 256
SB_SUBTILES = 2
DL_TILE = 128
DL_CHUNK = DL_TILE * max(d for _, d in DIL_PATTERNS)
MLA_TILE = 512
MLA_ROW_BLOCK = 64
MM_TM = 1024
MM_TN = 512


def _cparams(sem):
    return pltpu.CompilerParams(dimension_semantics=sem, vmem_limit_bytes=VMEM_LIMIT)


def _tile(n, want):
    if n <= want:
        return n
    t = (want // LANES) * LANES
    while n % t:
        t -= LANES
    return t


def _mod_kernel(c_ref, w_ref, b_ref, o_ref):
    a = c_ref[...]
    a = (a * jax.nn.sigmoid(a)).astype(BF16)
    o_ref[...] = jnp.dot(a, w_ref[...].astype(BF16), preferred_element_type=F32) + b_ref[...]


def _modulation(c, ada_w, ada_b):
    depth, d, n3 = ada_w.shape
    b = c.shape[0]
    rows = -(-b // SUBLANES) * SUBLANES
    c_pad = jnp.pad(c, ((0, rows - b), (0, 0)))
    tn = _tile(n3, 512)
    out = pl.pallas_call(
        _mod_kernel,
        grid=(depth, n3 // tn),
        in_specs=[pl.BlockSpec((rows, d), lambda l, j: (0, 0)),
                  pl.BlockSpec((None, d, tn), lambda l, j: (l, 0, j)),
                  pl.BlockSpec((None, 1, tn), lambda l, j: (l, 0, j))],
        out_specs=pl.BlockSpec((None, rows, tn), lambda l, j: (l, 0, j)),
        out_shape=jax.ShapeDtypeStruct((depth, rows, n3), F32),
        compiler_params=_cparams(("parallel", "parallel")),
        name="modulation",
    )(c_pad, ada_w, ada_b.reshape(depth, 1, n3))
    return out[:, :b]


def _norm_mod_kernel(x_ref, w_ref, scale_ref, shift_ref, o_ref):
    x = x_ref[...]
    y = x * lax.rsqrt(jnp.mean(x * x, axis=-1, keepdims=True) + EPS)
    y = y * w_ref[...]
    o_ref[...] = (y * (1.0 + scale_ref[...]) + shift_ref[...]).astype(o_ref.dtype)


def _norm_mod(x, w, scale, shift):
    b, s, d = x.shape
    ts = _tile(s, 256)
    return pl.pallas_call(
        _norm_mod_kernel,
        grid=(b, s // ts),
        in_specs=[pl.BlockSpec((None, ts, d), lambda bi, i: (bi, i, 0)),
                  pl.BlockSpec((1, d), lambda bi, i: (0, 0)),
                  pl.BlockSpec((None, 1, d), lambda bi, i: (bi, 0, 0)),
                  pl.BlockSpec((None, 1, d), lambda bi, i: (bi, 0, 0))],
        out_specs=pl.BlockSpec((None, ts, d), lambda bi, i: (bi, i, 0)),
        out_shape=jax.ShapeDtypeStruct((b, s, d), BF16),
        compiler_params=_cparams(("parallel", "parallel")),
        name="norm_mod",
    )(x, w.reshape(1, d), scale, shift)


def _run_epilogue(variants, chunk_of, tn, acc_ref):
    chunks = [(c0, min(MXU_COLS, tn - c0)) for c0 in range(0, tn, MXU_COLS)]

    def fused(fn):
        for c0, w in chunks:
            fn(chunk_of(c0, w), c0)

    def matmul_only():
        acc_ref[...] = chunk_of(0, tn)

    def epilogue_only(fn):
        for c0, w in chunks:
            fn(acc_ref.at[:, c0:c0 + w], c0)

    split = [(cond, fn) for cond, fn in variants if getattr(fn, "via_ref", False) and cond is not None]
    if split:
        pl.when(functools.reduce(jnp.logical_or, [cond for cond, _ in split]))(matmul_only)
    for cond, fn in variants:
        if getattr(fn, "via_ref", False):
            if cond is None:
                matmul_only()
                epilogue_only(fn)
            else:
                pl.when(cond)(functools.partial(epilogue_only, fn))
        elif cond is None:
            fused(fn)
        else:
            pl.when(cond)(functools.partial(fused, fn))


def _via_ref(fn):
    fn.via_ref = True
    return fn


def _mm_kernel(*refs, n_extra, n_out, nk, tn, epilogue, prologue):
    x_ref, w_ref = refs[0], refs[1]
    extras = refs[2:2 + n_extra]
    outs = refs[2 + n_extra:2 + n_extra + n_out]
    scratch = refs[2 + n_extra + n_out:]
    acc_ref = scratch[0]
    if prologue is not None:
        h_ref = scratch[1]

        @pl.when(pl.program_id(1) == 0)
        def _():
            h_ref[...] = prologue(x_ref, extras)

        _run_epilogue(epilogue(extras, outs),
                      lambda c0, w: jnp.dot(h_ref[...], w_ref[:, c0:c0 + w], preferred_element_type=F32),
                      tn, acc_ref)
        return
    k = pl.program_id(2)

    @pl.when(k == 0)
    def _():
        acc_ref[...] = jnp.zeros_like(acc_ref)

    acc_ref[...] += jnp.dot(x_ref[...], w_ref[...], preferred_element_type=F32)

    @pl.when(k == nk - 1)
    def _():
        _run_epilogue(epilogue(extras, outs), lambda c0, w: acc_ref[:, c0:c0 + w], tn, acc_ref)


def _matmul(x, w, *, tm, tn, tk, extras, extra_specs, out_shapes, out_specs, epilogue,
            prologue=None, x_col_block=0, name):
    m = x.shape[0]
    kdim, n = w.shape
    nk = kdim // tk
    assert m % tm == 0 and n % tn == 0 and kdim % tk == 0 and x.shape[1] >= (x_col_block + nk) * tk
    scratch = [pltpu.VMEM((tm, tn), F32)]
    if prologue is not None:
        assert nk == 1
        scratch.append(pltpu.VMEM((tm, kdim), BF16))
    kern = functools.partial(_mm_kernel, n_extra=len(extras), n_out=len(out_shapes), nk=nk, tn=tn,
                             epilogue=epilogue, prologue=prologue)
    return pl.pallas_call(
        kern,
        grid=(m // tm, n // tn, nk),
        in_specs=[pl.BlockSpec((tm, tk), lambda i, j, k: (i, x_col_block + k)),
                  pl.BlockSpec((tk, tn), lambda i, j, k: (k, j))] + list(extra_specs),
        out_specs=list(out_specs),
        out_shape=list(out_shapes),
        scratch_shapes=scratch,
        compiler_params=_cparams(("parallel", "arbitrary", "arbitrary")),
        name=name,
    )(x, w, *extras)


def _mm_ws_kernel(*refs, k_parts, n_extra, n_out, tn, cast, epilogue):
    n_x = len(k_parts)
    x_refs = refs[:n_x]
    w_ref = refs[n_x]
    extras = refs[n_x + 1:n_x + 1 + n_extra]
    outs = refs[n_x + 1 + n_extra:n_x + 1 + n_extra + n_out]
    scratch = refs[n_x + 1 + n_extra + n_out:]
    acc_ref = scratch[0]
    if cast:
        wb_ref = scratch[1]

        @pl.when(pl.program_id(1) == 0)
        def _():
            wb_ref[...] = w_ref[...].astype(BF16)
    else:
        wb_ref = w_ref

    def chunk_of(c0, w):
        acc = None
        k0 = 0
        for x_ref, kp in zip(x_refs, k_parts):
            part = jnp.dot(x_ref[...], wb_ref[k0:k0 + kp, c0:c0 + w], preferred_element_type=F32)
            acc = part if acc is None else acc + part
            k0 += kp
        return acc

    _run_epilogue(epilogue(extras, outs), chunk_of, tn, acc_ref)


def _matmul_ws(xs, w, *, tm, tn, extras, extra_specs, out_shapes, out_specs, epilogue, name, layer=None):
    m = xs[0].shape[0]
    kdim, n = w.shape[-2:]
    if layer is None:
        w_spec = pl.BlockSpec((kdim, tn), lambda j, i: (0, j))
    else:
        w_spec = pl.BlockSpec((None, kdim, tn), lambda j, i: (layer, 0, j))
    k_parts = tuple(x.shape[1] for x in xs)
    assert m % tm == 0 and n % tn == 0 and sum(k_parts) == kdim
    cast = w.dtype != BF16
    scratch = [pltpu.VMEM((tm, tn), F32)]
    if cast:
        scratch.append(pltpu.VMEM((kdim, tn), BF16))
    kern = functools.partial(_mm_ws_kernel, k_parts=k_parts, n_extra=len(extras),
                             n_out=len(out_shapes), tn=tn, cast=cast, epilogue=epilogue)
    x_specs = [pl.BlockSpec((tm, kp), lambda j, i: (i, 0)) for kp in k_parts]
    return pl.pallas_call(
        kern,
        grid=(n // tn, m // tm),
        in_specs=x_specs + [w_spec] + list(extra_specs),
        out_specs=list(out_specs),
        out_shape=list(out_shapes),
        scratch_shapes=scratch,
        compiler_params=_cparams(("parallel", "arbitrary")),
        name=name,
    )(*xs, w, *extras)


def _epi_plain(extras, outs):
    o_ref = outs[0]

    def fn(a, c0):
        o_ref[:, c0:c0 + a.shape[1]] = a.astype(o_ref.dtype)

    return [(None, fn)]


def _epi_resid(extras, outs):
    xres_ref, gate_ref = extras
    o_ref = outs[0]

    def fn(a, c0):
        sl = slice(c0, c0 + a.shape[1])
        o_ref[:, sl] = xres_ref[:, sl] + gate_ref[:, sl] * a

    return [(None, fn)]


def _rope_full(y, cos_ref, sin_ref):
    return y * cos_ref[...] + pltpu.roll(y, HEAD_DIM // 2, 1) * sin_ref[...]


def _rope_mla(r, c_ref, s_ref):
    return r * c_ref[...] + pltpu.roll(r, QK_ROPE // 2, 1) * s_ref[...]


def _epi_even(extras, outs, *, part_tiles, q_scale):
    qn_ref, kn_ref, cos_ref, sin_ref = extras
    o_ref = outs[0]
    part = pl.program_id(0) // part_tiles

    def scaled(scale):
        def fn(a, c0):
            y = a if scale == 1.0 else a * scale
            o_ref[:, c0:c0 + a.shape[1]] = y.astype(o_ref.dtype)
        return fn

    def headnorm_rope(w_ref, scale):
        @_via_ref
        def fn(a, c0):
            for hh in range(a.shape[1] // HEAD_DIM):
                ah = a[:, hh * HEAD_DIM:(hh + 1) * HEAD_DIM]
                y = ah * lax.rsqrt(jnp.mean(ah * ah, axis=-1, keepdims=True) + EPS) * w_ref[...]
                y = _rope_full(y, cos_ref, sin_ref)
                if scale != 1.0:
                    y = y * scale
                o_ref[:, c0 + hh * HEAD_DIM:c0 + (hh + 1) * HEAD_DIM] = y.astype(o_ref.dtype)
        return fn

    return [(part == 0, scaled(q_scale)),
            (part == 4, headnorm_rope(qn_ref, q_scale * LOG2_E)),
            (part == 5, headnorm_rope(kn_ref, 1.0)),
            ((part != 0) & (part != 4) & (part != 5), scaled(1.0))]


def _pro_rms(x_ref, extras):
    x = x_ref[...]
    w_ref = extras[0]
    y = x * lax.rsqrt(jnp.mean(x * x, axis=-1, keepdims=True) + EPS) * w_ref[...]
    return y.astype(BF16)


def _epi_mla_q(extras, outs):
    _, qw_ref, first_ref, c_ref, s_ref = extras
    o_ref = outs[0]
    assert MXU_COLS == MLA_QK_PAD

    @_via_ref
    def fn(a, c0):
        nope, rope = a[:, :QK_NOPE], a[:, QK_NOPE:]
        sq = nope * nope + rope * rope * first_ref[...]
        rstd = lax.rsqrt(jnp.sum(sq, axis=-1, keepdims=True) * (1.0 / QK_HEAD) + EPS)
        o_ref[:, c0:c0 + QK_NOPE] = (nope * rstd * qw_ref[:, :QK_NOPE]).astype(o_ref.dtype)
        r = _rope_mla(rope * rstd * qw_ref[:, QK_NOPE:], c_ref, s_ref)
        o_ref[:, c0 + QK_NOPE:c0 + MLA_QK_PAD] = r.astype(o_ref.dtype)

    return [(None, fn)]


def _epi_mla_kv(extras, outs):
    _, kpe_ref, kwn_ref, kwp_ref, first_ref, c_ref, s_ref = extras
    k_ref, v_ref = outs
    assert MXU_COLS == QK_NOPE + V_HEAD
    kpe = kpe_ref[...]
    ss_pe = jnp.sum(kpe * kpe * first_ref[...], axis=-1, keepdims=True)
    kr = _rope_mla(kpe * kwp_ref[...], c_ref, s_ref)

    @_via_ref
    def fn(a, c0):
        head = c0 // (QK_NOPE + V_HEAD)
        kn = a[:, :QK_NOPE]
        rstd = lax.rsqrt((jnp.sum(kn * kn, axis=-1, keepdims=True) + ss_pe) * (1.0 / QK_HEAD) + EPS)
        kb = head * MLA_QK_PAD
        k_ref[:, kb:kb + QK_NOPE] = (kn * rstd * kwn_ref[...]).astype(k_ref.dtype)
        k_ref[:, kb + QK_NOPE:kb + MLA_QK_PAD] = (kr * rstd).astype(k_ref.dtype)
        v_ref[:, head * V_HEAD:(head + 1) * V_HEAD] = a[:, QK_NOPE:].astype(v_ref.dtype)

    return [(None, fn)]


def _qk(q, k):
    return lax.dot_general(q, k, (((1,), (1,)), ((), ())), preferred_element_type=F32)


def _sb_kernel(q_ref, k_ref, v_ref, g_ref, u_ref, o_ref, acc_ref, carry_ref, *, tile):
    qi = pl.program_id(2)
    n_sub = q_ref.shape[0] // tile

    def tile_rows(kb):
        return pl.ds(pl.multiple_of(kb * tile, tile), tile)

    def weights(q, kb, carry, diagonal=False, valid=None):
        z = _qk(q, k_ref[tile_rows(kb), :])
        sp = jnp.log(1.0 + jnp.exp(-jnp.abs(z)))
        log_beta = jnp.minimum(z, 0.0) - sp
        log_keep = jnp.minimum(-z, 0.0) - sp
        keep = valid
        if diagonal:
            row = lax.broadcasted_iota(jnp.int32, z.shape, 0)
            col = lax.broadcasted_iota(jnp.int32, z.shape, 1)
            keep = col < row
        if keep is not None:
            log_keep = jnp.where(keep, log_keep, 0.0)
        hi = log_keep.astype(BF16)
        lo = (log_keep - hi.astype(F32)).astype(BF16)
        u = u_ref[...]
        excl = jnp.dot(hi, u, preferred_element_type=F32) + jnp.dot(lo, u, preferred_element_type=F32)
        if carry is not None:
            excl = excl + carry
        w = jnp.exp(log_beta + excl)
        if keep is not None:
            w = jnp.where(keep, w, 0.0)
        return w.astype(BF16), jnp.sum(log_keep, axis=-1, keepdims=True)

    def pv(w, kb):
        return jnp.dot(w, v_ref[tile_rows(kb), :], preferred_element_type=F32)

    tails = []
    for sub in range(n_sub):
        rows = slice(sub * tile, (sub + 1) * tile)
        q = q_ref[rows, :]
        diag = qi * n_sub + sub
        prev = jnp.maximum(diag - 1, 0)
        w_d, sum_d = weights(q, diag, None, diagonal=True)
        w_p, sum_p = weights(q, prev, sum_d, valid=diag > 0)
        carry = sum_d + sum_p
        acc_ref[rows, :] = pv(w_d, diag) + pv(w_p, prev)
        carry_ref[rows, :] = carry
        tails.append((rows, diag - 2, jnp.max(carry)))

    for rows, kb0, cmax0 in tails:
        def cond(state):
            kb, cmax = state
            return (kb >= 0) & (cmax > EXP_UNDERFLOW)

        def body(state, rows=rows):
            kb, _ = state
            carry = carry_ref[rows, :]
            w, sum_k = weights(q_ref[rows, :], kb, carry)
            acc_ref[rows, :] += pv(w, kb)
            carry = carry + sum_k
            carry_ref[rows, :] = carry
            return kb - 1, jnp.max(carry)

        lax.while_loop(cond, body, (kb0, cmax0))

    g = g_ref[...].astype(F32)
    o_ref[...] = (acc_ref[...] * (g * jax.nn.sigmoid(g))).astype(o_ref.dtype)


def _dl_kernel(q_ref, k_ref, v_ref, g_ref, bias_ref, o_ref, qf_ref, kf_ref, vf_ref, acc_ref, m_ref, l_ref,
               *, chunk):
    ci = pl.program_id(2)

    @pl.when(ci == 0)
    def _():
        kf_ref[...] = k_ref[...].astype(F32)
        vf_ref[...] = v_ref[...].astype(F32)

    qf_ref[...] = q_ref[...].astype(F32)
    base = ci * chunk

    def body(idx, _):
        for pi, (_, dil) in enumerate(DIL_PATTERNS):
            unit = DL_TILE * dil
            q0 = (idx // dil) * unit + idx % dil
            rows = pl.ds(q0, DL_TILE, stride=dil)
            cur0 = base + q0
            has_prev = cur0 >= unit
            prev0 = jnp.maximum(cur0 - unit, 0)
            prev_rows = pl.ds(prev0, DL_TILE, stride=dil)
            cur_rows = pl.ds(cur0, DL_TILE, stride=dil)
            kk = jnp.concatenate([kf_ref[prev_rows, :], kf_ref[cur_rows, :]], axis=0).astype(BF16)
            vv = jnp.concatenate([vf_ref[prev_rows, :], vf_ref[cur_rows, :]], axis=0).astype(BF16)
            s = _qk(qf_ref[rows, :].astype(BF16), kk) + bias_ref[pi, has_prev.astype(jnp.int32)]
            m = jnp.max(s, axis=-1, keepdims=True)
            p = jnp.exp2(s - m)
            l = jnp.sum(p, axis=-1, keepdims=True)
            acc_ref[pi, rows, :] = jnp.dot(p.astype(BF16), vv, preferred_element_type=F32)
            m_ref[pi, rows, :] = jnp.broadcast_to(m, (DL_TILE, LANES))
            l_ref[pi, rows, :] = jnp.broadcast_to(l, (DL_TILE, LANES))
        return 0

    lax.fori_loop(0, chunk // DL_TILE, body, 0, unroll=8)

    n_pat = len(DIL_PATTERNS)
    m_all = functools.reduce(jnp.maximum, [m_ref[pi] for pi in range(n_pat)])
    num = jnp.zeros(acc_ref.shape[1:], F32)
    den = jnp.zeros(acc_ref.shape[1:], F32)
    for pi in range(n_pat):
        w = jnp.exp2(m_ref[pi] - m_all)
        num = num + w * acc_ref[pi]
        den = den + w * l_ref[pi]
    g = g_ref[...].astype(F32)
    o_ref[...] = (num / den * (g * jax.nn.sigmoid(g))).astype(o_ref.dtype)


def _softmax_update(s_ref, bias_ref, v, p_ref, acc_ref, m_ref, l_ref):
    tq, tk = s_ref.shape
    for r0 in range(0, tq, MLA_ROW_BLOCK):
        rs = slice(r0, r0 + MLA_ROW_BLOCK)
        cols = [s_ref[rs, c * LANES:(c + 1) * LANES] for c in range(tk // LANES)]
        if bias_ref is not None:
            cols = [col + bias_ref[rs, c * LANES:(c + 1) * LANES] for c, col in enumerate(cols)]
        m_old = m_ref[rs, :]
        tile_max = jnp.max(functools.reduce(jnp.maximum, cols), axis=-1, keepdims=True)
        m_new = jnp.maximum(m_old, tile_max)
        ps = [jnp.exp2(col - m_new) for col in cols]
        alpha = jnp.exp2(m_old - m_new)
        l_ref[rs, :] = alpha * l_ref[rs, :] + jnp.sum(functools.reduce(jnp.add, ps), axis=-1, keepdims=True)
        m_ref[rs, :] = m_new
        acc_ref[rs, :] = alpha * acc_ref[rs, :]
        for c, x in enumerate(ps):
            p_ref[rs, c * LANES:(c + 1) * LANES] = x.astype(BF16)
    acc_ref[...] += jnp.dot(p_ref[...], v, preferred_element_type=F32)


def _mla_kernel(q_ref, k_ref, v_ref, g_ref, bias_ref, o_ref, sa_ref, sb_ref, pa_ref, pb_ref, acc_ref, m_ref,
                l_ref, *, tile):
    n_q = q_ref.shape[0] // tile
    acc_ref[...] = jnp.zeros_like(acc_ref)
    m_ref[...] = jnp.full_like(m_ref, -jnp.inf)
    l_ref[...] = jnp.zeros_like(l_ref)

    def rows(t):
        return pl.ds(pl.multiple_of(t * tile, tile), tile)

    def scores(qi, kb, dst_ref):
        dst_ref[...] = _qk(q_ref[rows(qi), :], k_ref[rows(kb), :])

    def pipeline(n_steps, start, advance, bias):
        def update(qi, kb, src_ref, p_ref):
            r = rows(qi)
            _softmax_update(src_ref, bias, v_ref[rows(kb), :], p_ref, acc_ref.at[r, :], m_ref.at[r, :],
                            l_ref.at[r, :])

        scores(*start, sa_ref)

        def pair(_, state):
            qa, ka = state
            qb, kb = advance(qa, ka)
            scores(qb, kb, sb_ref)
            update(qa, ka, sa_ref, pa_ref)
            qa, ka = advance(qb, kb)
            scores(qa, ka, sa_ref)
            update(qb, kb, sb_ref, pb_ref)
            return qa, ka

        qa, ka = lax.fori_loop(0, n_steps // 2, pair, start, unroll=2)
        if n_steps % 2:
            update(qa, ka, sa_ref, pa_ref)

    def next_lower(qi, kb):
        wrap = kb == qi - 1
        return jnp.where(wrap, jnp.minimum(qi + 1, n_q - 1), qi), jnp.where(wrap, 0, kb + 1)

    def next_diagonal(qi, kb):
        t = jnp.minimum(qi + 1, n_q - 1)
        return t, t

    zero, one = jnp.int32(0), jnp.int32(1)
    if n_q > 1:
        pipeline(n_q * (n_q - 1) // 2, (one, zero), next_lower, None)
    pipeline(n_q, (zero, zero), next_diagonal, bias_ref)

    g = g_ref[...].astype(F32)
    o_ref[...] = (acc_ref[...] * g / (l_ref[...] * (1.0 + jnp.exp(-g)))).astype(o_ref.dtype)


def _attn_call(kern, arrays, col_blocks, widths, consts, out_width_total, out_col0, n_heads, tile,
               scratch, name):
    q, k, v, g = arrays
    b, s, _ = q.shape
    cq, ck, cv, cg = col_blocks
    wq, wk, wv, wg = widths
    in_specs = [
        pl.BlockSpec((None, tile, wq), lambda bi, h, i: (bi, i, cq + h)),
        pl.BlockSpec((None, s, wk), lambda bi, h, i: (bi, 0, ck + h)),
        pl.BlockSpec((None, s, wv), lambda bi, h, i: (bi, 0, cv + h)),
        pl.BlockSpec((None, tile, wg), lambda bi, h, i: (bi, i, cg + h)),
    ]
    for cst in consts:
        in_specs.append(pl.BlockSpec(cst.shape, lambda bi, h, i, nd=cst.ndim: (0,) * nd))
    return pl.pallas_call(
        kern,
        grid=(b, n_heads, s // tile),
        in_specs=in_specs,
        out_specs=pl.BlockSpec((None, tile, wv), lambda bi, h, i: (bi, i, out_col0 + h)),
        out_shape=jax.ShapeDtypeStruct((b, s, out_width_total), BF16),
        scratch_shapes=scratch,
        compiler_params=_cparams(("parallel", "parallel", "arbitrary")),
        name=name,
    )(q, k, v, g, *consts)


def _dl_band_bias():
    i = np.arange(DL_TILE)[:, None]
    c = np.arange(2 * DL_TILE)[None, :]
    out = []
    for window, dil in DIL_PATTERNS:
        span = window // dil
        assert span <= DL_TILE
        dist = i + DL_TILE - c
        band = (dist >= 0) & (dist <= span)
        out.append(np.stack([band & (c >= DL_TILE), band]))
    return np.where(np.stack(out), 0.0, -np.inf).astype(np.float32)


def _rope_angles(positions, dim):
    inv = ROPE_THETA ** (-jnp.arange(0, dim, 2, dtype=F32) / dim)
    ang = positions.astype(F32)[..., None] * inv
    return jnp.cos(ang), jnp.sin(ang)


def _even_layer(x, h, gate, li, w_in, qn, kn, w_out, cosf, sinf, sb_u, dl_bias):
    b, s, d = x.shape
    m = b * s
    n_in = w_in.shape[-1]
    w_part = n_in // 8
    n_heads = w_part // HEAD_DIM
    tm, tn = _tile(s, MM_TM), _tile(w_part, MM_TN)
    q_scale = 1.0 / math.sqrt(HEAD_DIM)
    proj = _matmul_ws(
        [h.reshape(m, d)], w_in, layer=li, tm=tm, tn=tn,
        extras=(qn.reshape(1, HEAD_DIM), kn.reshape(1, HEAD_DIM), cosf, sinf),
        extra_specs=[pl.BlockSpec((1, HEAD_DIM), lambda j, i: (0, 0)),
                     pl.BlockSpec((1, HEAD_DIM), lambda j, i: (0, 0)),
                     pl.BlockSpec((tm, HEAD_DIM), lambda j, i: (i, 0)),
                     pl.BlockSpec((tm, HEAD_DIM), lambda j, i: (i, 0))],
        out_shapes=[jax.ShapeDtypeStruct((m, n_in), BF16)],
        out_specs=[pl.BlockSpec((tm, tn), lambda j, i: (i, j))],
        epilogue=functools.partial(_epi_even, part_tiles=w_part // tn, q_scale=q_scale),
        name="even_in_proj")[0].reshape(b, s, n_in)

    hb = n_heads
    w4 = (HEAD_DIM,) * 4
    sb_rows = SB_TILE * SB_SUBTILES
    o_sb = _attn_call(
        functools.partial(_sb_kernel, tile=SB_TILE), (proj,) * 4, (0, hb, 2 * hb, 3 * hb), w4,
        (sb_u,), w_part, 0, n_heads, sb_rows,
        [pltpu.VMEM((sb_rows, HEAD_DIM), F32), pltpu.VMEM((sb_rows, 1), F32)], "stick_breaking")
    assert s % DL_CHUNK == 0
    n_pat = len(DIL_PATTERNS)
    o_dl = _attn_call(
        functools.partial(_dl_kernel, chunk=DL_CHUNK), (proj,) * 4,
        (4 * hb, 5 * hb, 6 * hb, 7 * hb), w4, (dl_bias,), w_part, 0, n_heads, DL_CHUNK,
        [pltpu.VMEM((DL_CHUNK, HEAD_DIM), F32), pltpu.VMEM((s, HEAD_DIM), F32),
         pltpu.VMEM((s, HEAD_DIM), F32), pltpu.VMEM((n_pat, DL_CHUNK, HEAD_DIM), F32),
         pltpu.VMEM((n_pat, DL_CHUNK, LANES), F32), pltpu.VMEM((n_pat, DL_CHUNK, LANES), F32)],
        "dilated_window")
    return _out_proj(x, [o_sb.reshape(m, w_part), o_dl.reshape(m, w_part)], w_out, li, gate)


def _out_proj(x, mixed_parts, w_out, li, gate):
    b, s, d = x.shape
    m = b * s
    tm, tn = _tile(s, MM_TM), _tile(d, MM_TN)
    rows_per_batch = s // tm
    return _matmul_ws(
        mixed_parts, w_out, layer=li, tm=tm, tn=tn,
        extras=(x.reshape(m, d), gate),
        extra_specs=[pl.BlockSpec((tm, tn), lambda j, i: (i, j)),
                     pl.BlockSpec((None, 1, tn), lambda j, i: (i // rows_per_batch, 0, j))],
        out_shapes=[jax.ShapeDtypeStruct((m, d), F32)],
        out_specs=[pl.BlockSpec((tm, tn), lambda j, i: (i, j))],
        epilogue=_epi_resid, name="out_proj")[0].reshape(b, s, d)


def _odd_layer(x, h, gate, li, w_in, q_lat_norm, kv_lat_norm, w_uq, w_ukv, q_norm, k_norm, w_out,
               rc, rs, mla_bias):
    b, s, d = x.shape
    m = b * s
    q_lora = q_lat_norm.shape[0]
    kv_lora = kv_lat_norm.shape[0]
    n_heads = w_uq.shape[1] // QK_HEAD
    lat_raw = q_lora + kv_lora + QK_ROPE
    n_lat = lat_raw + QK_ROPE
    assert n_lat % LANES == 0 and 2 * QK_ROPE == LANES
    h2 = h.reshape(m, d)
    tm, tk = _tile(s, 1024), _tile(d, 1024)
    first = (jnp.arange(LANES) < QK_ROPE).astype(F32).reshape(1, LANES)

    w_lat = jnp.concatenate([w_in[:, :lat_raw], w_in[:, lat_raw - QK_ROPE:lat_raw]], axis=1).astype(BF16)
    tm_lat = _tile(s, 512)
    lat = _matmul(h2, w_lat, tm=tm_lat, tn=n_lat, tk=tk, extras=(), extra_specs=[],
                  out_shapes=[jax.ShapeDtypeStruct((m, n_lat), F32)],
                  out_specs=[pl.BlockSpec((tm_lat, n_lat), lambda i, j, k: (i, j))],
                  epilogue=_epi_plain, name="odd_latent_proj")[0]
    n_g = w_in.shape[1] - lat_raw
    tn_g = _tile(n_g, MM_TN)
    g = _matmul_ws([h2], w_in[:, lat_raw:].astype(BF16), tm=tm, tn=tn_g, extras=(), extra_specs=[],
                   out_shapes=[jax.ShapeDtypeStruct((m, n_g), BF16)],
                   out_specs=[pl.BlockSpec((tm, tn_g), lambda j, i: (i, j))],
                   epilogue=_epi_plain, name="odd_gate_proj")[0]

    w_uq3 = w_uq.reshape(q_lora, n_heads, QK_HEAD)
    w_uq_p = jnp.concatenate([w_uq3, w_uq3[:, :, QK_NOPE:]], axis=-1)
    w_uq_p = w_uq_p.reshape(q_lora, n_heads * MLA_QK_PAD).astype(BF16)
    qw = (jnp.concatenate([q_norm, q_norm[QK_NOPE:]]) * (LOG2_E / math.sqrt(QK_HEAD))).reshape(1, MLA_QK_PAD)
    tn_q = _tile(n_heads * MLA_QK_PAD, 1024)
    rope_specs = [pl.BlockSpec((1, LANES), lambda i, j, k: (0, 0))] + \
        [pl.BlockSpec((tm, LANES), lambda i, j, k: (i, 0))] * 2
    q = _matmul(
        lat, w_uq_p, tm=tm, tn=tn_q, tk=q_lora,
        extras=(q_lat_norm.reshape(1, q_lora), qw, first, rc, rs),
        extra_specs=[pl.BlockSpec((1, q_lora), lambda i, j, k: (0, 0)),
                     pl.BlockSpec((1, MLA_QK_PAD), lambda i, j, k: (0, 0))] + rope_specs,
        out_shapes=[jax.ShapeDtypeStruct((m, n_heads * MLA_QK_PAD), BF16)],
        out_specs=[pl.BlockSpec((tm, tn_q), lambda i, j, k: (i, j))],
        epilogue=_epi_mla_q,
        prologue=_pro_rms, name="mla_q_proj")[0]

    assert q_lora % kv_lora == 0 and (q_lora + kv_lora) % LANES == 0
    kvw = QK_NOPE + V_HEAD
    tn_kv = _tile(n_heads * kvw, 1024)
    heads_per_tile = tn_kv // kvw
    kwn = k_norm[:QK_NOPE].reshape(1, QK_NOPE)
    kwp = jnp.concatenate([k_norm[QK_NOPE:], k_norm[QK_NOPE:]]).reshape(1, LANES)
    kpe_block = (q_lora + kv_lora) // LANES
    k, v = _matmul(
        lat, w_ukv.astype(BF16), tm=tm, tn=tn_kv, tk=kv_lora,
        extras=(kv_lat_norm.reshape(1, kv_lora), lat, kwn, kwp, first, rc, rs),
        extra_specs=[pl.BlockSpec((1, kv_lora), lambda i, j, k: (0, 0)),
                     pl.BlockSpec((tm, LANES), lambda i, j, k: (i, kpe_block)),
                     pl.BlockSpec((1, QK_NOPE), lambda i, j, k: (0, 0)),
                     pl.BlockSpec((1, LANES), lambda i, j, k: (0, 0))] + rope_specs,
        out_shapes=[jax.ShapeDtypeStruct((m, n_heads * MLA_QK_PAD), BF16),
                    jax.ShapeDtypeStruct((m, n_heads * V_HEAD), BF16)],
        out_specs=[pl.BlockSpec((tm, heads_per_tile * MLA_QK_PAD), lambda i, j, k: (i, j)),
                   pl.BlockSpec((tm, heads_per_tile * V_HEAD), lambda i, j, k: (i, j))],
        epilogue=_epi_mla_kv,
        prologue=_pro_rms, x_col_block=q_lora // kv_lora, name="mla_kv_proj")
    mixed = _attn_call(
        functools.partial(_mla_kernel, tile=MLA_TILE),
        (q.reshape(b, s, -1), k.reshape(b, s, -1), v.reshape(b, s, -1), g.reshape(b, s, -1)),
        (0, 0, 0, 0), (MLA_QK_PAD, MLA_QK_PAD, V_HEAD, V_HEAD), (mla_bias,), n_heads * V_HEAD, 0,
        n_heads, s,
        [pltpu.VMEM((MLA_TILE, MLA_TILE), F32), pltpu.VMEM((MLA_TILE, MLA_TILE), F32),
         pltpu.VMEM((MLA_TILE, MLA_TILE), BF16), pltpu.VMEM((MLA_TILE, MLA_TILE), BF16),
         pltpu.VMEM((s, V_HEAD), F32), pltpu.VMEM((s, LANES), F32),
         pltpu.VMEM((s, LANES), F32)], "mla_attention")
    return _out_proj(x, [mixed.reshape(m, -1)], w_out, li, gate)


def kernel(x, c, positions, ada_w, ada_b, norm_w, ev_w_in, ev_q_norm, ev_k_norm, ev_w_out, od_w_in,
           od_q_lat_norm, od_kv_lat_norm, od_w_uq, od_w_ukv, od_q_norm, od_k_norm, od_w_out):
    b, s, d = x.shape
    depth = ada_w.shape[0]
    m = b * s

    cos_f, sin_f = _rope_angles(positions, HEAD_DIM)
    cosf = jnp.concatenate([cos_f, cos_f], axis=-1).reshape(m, HEAD_DIM)
    sinf = jnp.concatenate([-sin_f, sin_f], axis=-1).reshape(m, HEAD_DIM)
    cos_m, sin_m = _rope_angles(positions, QK_ROPE)
    z32 = jnp.zeros_like(cos_m)
    z64 = jnp.concatenate([z32, z32], axis=-1)
    rc = jnp.concatenate([cos_m, cos_m, z64], axis=-1).reshape(m, LANES)
    rs = jnp.concatenate([-sin_m, sin_m, z64], axis=-1).reshape(m, LANES)

    sb_u = jnp.asarray(np.tril(np.ones((SB_TILE, SB_TILE), np.float32), -1), BF16)
    dl_bias = jnp.asarray(_dl_band_bias())
    causal = np.tril(np.ones((MLA_TILE, MLA_TILE), bool))
    mla_bias = jnp.asarray(np.where(causal, 0.0, -np.inf).astype(np.float32))

    mod = _modulation(c, ada_w, ada_b)
    shift, scale, gate = (mod[:, :, i * d:(i + 1) * d].reshape(depth, b, 1, d) for i in range(3))

    for layer in range(depth):
        h = _norm_mod(x, norm_w[layer], scale[layer], shift[layer])
        i = layer // 2
        if layer % 2 == 0:
            x = _even_layer(x, h, gate[layer], i, ev_w_in, ev_q_norm[i], ev_k_norm[i], ev_w_out,
                            cosf, sinf, sb_u, dl_bias)
        else:
            x = _odd_layer(x, h, gate[layer], i, od_w_in[i], od_q_lat_norm[i], od_kv_lat_norm[i],
                           od_w_uq[i], od_w_ukv[i], od_q_norm[i], od_k_norm[i], od_w_out,
                           rc, rs, mla_bias)
    return x
```

```python
import functools
import math

import numpy as np
import jax
import jax.numpy as jnp
from jax import lax
from jax.experimental import pallas as pl
from jax.experimental.pallas import tpu as pltpu

F32 = jnp.float32
BF16 = jnp.bfloat16

HEAD_DIM = 128
DIL_PATTERNS = ((128, 1), (512, 4), (2048, 16))
KV_LORA = 512
QK_NOPE = 128
QK_ROPE = 64
QK_HEAD = QK_NOPE + QK_ROPE
V_HEAD = 128
MLA_QK_PAD = 256
ROPE_THETA = 10000.0
EPS = 1e-6

LANES = 128
SUBLANES = 8
MXU_COLS = 256
VMEM_LIMIT = 56 * 1024 * 1024
EXP_UNDERFLOW = -104.0
LOG2_E = math.log2(math.e)
SB_TILE = 256
SB_SUBTILES = 4
DL_TILE = 128
DL_CHUNK = DL_TILE * max(d for _, d in DIL_PATTERNS)
MLA_TILE = 512
MLA_ROW_BLOCK = 64
MM_TM = 1024
MM_TN = 512


def _cparams(sem):
    return pltpu.CompilerParams(dimension_semantics=sem, vmem_limit_bytes=VMEM_LIMIT)


def _tile(n, want):
    if n <= want:
        return n
    t = (want // LANES) * LANES
    while n % t:
        t -= LANES
    return t


def _mod_kernel(c_ref, w_ref, b_ref, o_ref):
    a = c_ref[...]
    a = (a * jax.nn.sigmoid(a)).astype(BF16)
    o_ref[...] = jnp.dot(a, w_ref[...].astype(BF16), preferred_element_type=F32) + b_ref[...]


def _modulation(c, ada_w, ada_b):
    depth, d, n3 = ada_w.shape
    b = c.shape[0]
    rows = -(-b // SUBLANES) * SUBLANES
    c_pad = jnp.pad(c, ((0, rows - b), (0, 0)))
    tn = _tile(n3, 1024)
    out = pl.pallas_call(
        _mod_kernel,
        grid=(depth, n3 // tn),
        in_specs=[pl.BlockSpec((rows, d), lambda l, j: (0, 0)),
                  pl.BlockSpec((None, d, tn), lambda l, j: (l, 0, j)),
                  pl.BlockSpec((None, 1, tn), lambda l, j: (l, 0, j))],
        out_specs=pl.BlockSpec((None, rows, tn), lambda l, j: (l, 0, j)),
        out_shape=jax.ShapeDtypeStruct((depth, rows, n3), F32),
        compiler_params=_cparams(("parallel", "parallel")),
        name="modulation",
    )(c_pad, ada_w, ada_b.reshape(depth, 1, n3))
    return out[:, :b]


def _norm_mod_kernel(x_ref, w_ref, scale_ref, shift_ref, o_ref):
    x = x_ref[...]
    y = x * lax.rsqrt(jnp.mean(x * x, axis=-1, keepdims=True) + EPS)
    y = y * w_ref[...]
    o_ref[...] = (y * (1.0 + scale_ref[...]) + shift_ref[...]).astype(o_ref.dtype)


def _norm_mod(x, w, scale, shift):
    b, s, d = x.shape
    ts = _tile(s, 512)
    return pl.pallas_call(
        _norm_mod_kernel,
        grid=(b, s // ts),
        in_specs=[pl.BlockSpec((None, ts, d), lambda bi, i: (bi, i, 0)),
                  pl.BlockSpec((1, d), lambda bi, i: (0, 0)),
                  pl.BlockSpec((None, 1, d), lambda bi, i: (bi, 0, 0)),
                  pl.BlockSpec((None, 1, d), lambda bi, i: (bi, 0, 0))],
        out_specs=pl.BlockSpec((None, ts, d), lambda bi, i: (bi, i, 0)),
        out_shape=jax.ShapeDtypeStruct((b, s, d), BF16),
        compiler_params=_cparams(("parallel", "parallel")),
        name="norm_mod",
    )(x, w.reshape(1, d), scale, shift)


def _run_epilogue(variants, chunk_of, tn, acc_ref):
    chunks = [(c0, min(MXU_COLS, tn - c0)) for c0 in range(0, tn, MXU_COLS)]

    def fused(fn):
        for c0, w in chunks:
            fn(chunk_of(c0, w), c0)

    def matmul_only():
        acc_ref[...] = chunk_of(0, tn)

    def epilogue_only(fn):
        for c0, w in chunks:
            fn(acc_ref.at[:, c0:c0 + w], c0)

    split = [(cond, fn) for cond, fn in variants if getattr(fn, "via_ref", False) and cond is not None]
    if split:
        pl.when(functools.reduce(jnp.logical_or, [cond for cond, _ in split]))(matmul_only)
    for cond, fn in variants:
        if getattr(fn, "via_ref", False):
            if cond is None:
                matmul_only()
                epilogue_only(fn)
            else:
                pl.when(cond)(functools.partial(epilogue_only, fn))
        elif cond is None:
            fused(fn)
        else:
            pl.when(cond)(functools.partial(fused, fn))


def _via_ref(fn):
    fn.via_ref = True
    return fn


def _mm_kernel(*refs, n_extra, n_out, nk, tn, epilogue, prologue):
    x_ref, w_ref = refs[0], refs[1]
    extras = refs[2:2 + n_extra]
    outs = refs[2 + n_extra:2 + n_extra + n_out]
    scratch = refs[2 + n_extra + n_out:]
    acc_ref = scratch[0]
    if prologue is not None:
        h_ref = scratch[1]

        @pl.when(pl.program_id(1) == 0)
        def _():
            h_ref[...] = prologue(x_ref, extras)

        _run_epilogue(epilogue(extras, outs),
                      lambda c0, w: jnp.dot(h_ref[...], w_ref[:, c0:c0 + w], preferred_element_type=F32),
                      tn, acc_ref)
        return
    k = pl.program_id(2)

    @pl.when(k == 0)
    def _():
        acc_ref[...] = jnp.zeros_like(acc_ref)

    acc_ref[...] += jnp.dot(x_ref[...], w_ref[...], preferred_element_type=F32)

    @pl.when(k == nk - 1)
    def _():
        _run_epilogue(epilogue(extras, outs), lambda c0, w: acc_ref[:, c0:c0 + w], tn, acc_ref)


def _matmul(x, w, *, tm, tn, tk, extras, extra_specs, out_shapes, out_specs, epilogue,
            prologue=None, x_col_block=0, name):
    m = x.shape[0]
    kdim, n = w.shape
    nk = kdim // tk
    assert m % tm == 0 and n % tn == 0 and kdim % tk == 0 and x.shape[1] >= (x_col_block + nk) * tk
    scratch = [pltpu.VMEM((tm, tn), F32)]
    if prologue is not None:
        assert nk == 1
        scratch.append(pltpu.VMEM((tm, kdim), BF16))
    kern = functools.partial(_mm_kernel, n_extra=len(extras), n_out=len(out_shapes), nk=nk, tn=tn,
                             epilogue=epilogue, prologue=prologue)
    return pl.pallas_call(
        kern,
        grid=(m // tm, n // tn, nk),
        in_specs=[pl.BlockSpec((tm, tk), lambda i, j, k: (i, x_col_block + k)),
                  pl.BlockSpec((tk, tn), lambda i, j, k: (k, j))] + list(extra_specs),
        out_specs=list(out_specs),
        out_shape=list(out_shapes),
        scratch_shapes=scratch,
        compiler_params=_cparams(("parallel", "arbitrary", "arbitrary")),
        name=name,
    )(x, w, *extras)


def _mm_ws_kernel(*refs, k_parts, n_extra, n_out, tn, cast, epilogue):
    n_x = len(k_parts)
    x_refs = refs[:n_x]
    w_ref = refs[n_x]
    extras = refs[n_x + 1:n_x + 1 + n_extra]
    outs = refs[n_x + 1 + n_extra:n_x + 1 + n_extra + n_out]
    scratch = refs[n_x + 1 + n_extra + n_out:]
    acc_ref = scratch[0]
    if cast:
        wb_ref = scratch[1]

        @pl.when(pl.program_id(1) == 0)
        def _():
            wb_ref[...] = w_ref[...].astype(BF16)
    else:
        wb_ref = w_ref

    def chunk_of(c0, w):
        acc = None
        k0 = 0
        for x_ref, kp in zip(x_refs, k_parts):
            part = jnp.dot(x_ref[...], wb_ref[k0:k0 + kp, c0:c0 + w], preferred_element_type=F32)
            acc = part if acc is None else acc + part
            k0 += kp
        return acc

    _run_epilogue(epilogue(extras, outs), chunk_of, tn, acc_ref)


def _matmul_ws(xs, w, *, tm, tn, extras, extra_specs, out_shapes, out_specs, epilogue, name, layer=None):
    m = xs[0].shape[0]
    kdim, n = w.shape[-2:]
    if layer is None:
        w_spec = pl.BlockSpec((kdim, tn), lambda j, i: (0, j))
    else:
        w_spec = pl.BlockSpec((None, kdim, tn), lambda j, i: (layer, 0, j))
    k_parts = tuple(x.shape[1] for x in xs)
    assert m % tm == 0 and n % tn == 0 and sum(k_parts) == kdim
    cast = w.dtype != BF16
    scratch = [pltpu.VMEM((tm, tn), F32)]
    if cast:
        scratch.append(pltpu.VMEM((kdim, tn), BF16))
    kern = functools.partial(_mm_ws_kernel, k_parts=k_parts, n_extra=len(extras),
                             n_out=len(out_shapes), tn=tn, cast=cast, epilogue=epilogue)
    x_specs = [pl.BlockSpec((tm, kp), lambda j, i: (i, 0)) for kp in k_parts]
    return pl.pallas_call(
        kern,
        grid=(n // tn, m // tm),
        in_specs=x_specs + [w_spec] + list(extra_specs),
        out_specs=list(out_specs),
        out_shape=list(out_shapes),
        scratch_shapes=scratch,
        compiler_params=_cparams(("parallel", "arbitrary")),
        name=name,
    )(*xs, w, *extras)


def _epi_plain(extras, outs):
    o_ref = outs[0]

    def fn(a, c0):
        o_ref[:, c0:c0 + a.shape[1]] = a.astype(o_ref.dtype)

    return [(None, fn)]


def _epi_resid(extras, outs):
    xres_ref, gate_ref = extras
    o_ref = outs[0]

    def fn(a, c0):
        sl = slice(c0, c0 + a.shape[1])
        o_ref[:, sl] = xres_ref[:, sl] + gate_ref[:, sl] * a

    return [(None, fn)]


def _rope_full(y, cos_ref, sin_ref):
    return y * cos_ref[...] + pltpu.roll(y, HEAD_DIM // 2, 1) * sin_ref[...]


def _rope_mla(r, c_ref, s_ref):
    return r * c_ref[...] + pltpu.roll(r, QK_ROPE // 2, 1) * s_ref[...]


def _epi_even(extras, outs, *, part_tiles, q_scale):
    qn_ref, kn_ref, cos_ref, sin_ref = extras
    o_ref = outs[0]
    part = pl.program_id(0) // part_tiles

    def scaled(scale):
        def fn(a, c0):
            y = a if scale == 1.0 else a * scale
            o_ref[:, c0:c0 + a.shape[1]] = y.astype(o_ref.dtype)
        return fn

    def headnorm_rope(w_ref, scale):
        @_via_ref
        def fn(a, c0):
            for hh in range(a.shape[1] // HEAD_DIM):
                ah = a[:, hh * HEAD_DIM:(hh + 1) * HEAD_DIM]
                y = ah * lax.rsqrt(jnp.mean(ah * ah, axis=-1, keepdims=True) + EPS) * w_ref[...]
                y = _rope_full(y, cos_ref, sin_ref)
                if scale != 1.0:
                    y = y * scale
                o_ref[:, c0 + hh * HEAD_DIM:c0 + (hh + 1) * HEAD_DIM] = y.astype(o_ref.dtype)
        return fn

    return [(part == 0, scaled(q_scale)),
            (part == 4, headnorm_rope(qn_ref, q_scale * LOG2_E)),
            (part == 5, headnorm_rope(kn_ref, 1.0)),
            ((part != 0) & (part != 4) & (part != 5), scaled(1.0))]


def _pro_rms(x_ref, extras):
    x = x_ref[...]
    w_ref = extras[0]
    y = x * lax.rsqrt(jnp.mean(x * x, axis=-1, keepdims=True) + EPS) * w_ref[...]
    return y.astype(BF16)


def _epi_mla_q(extras, outs):
    _, qw_ref, first_ref, c_ref, s_ref = extras
    o_ref = outs[0]
    assert MXU_COLS == MLA_QK_PAD

    @_via_ref
    def fn(a, c0):
        nope, rope = a[:, :QK_NOPE], a[:, QK_NOPE:]
        sq = nope * nope + rope * rope * first_ref[...]
        rstd = lax.rsqrt(jnp.sum(sq, axis=-1, keepdims=True) * (1.0 / QK_HEAD) + EPS)
        o_ref[:, c0:c0 + QK_NOPE] = (nope * rstd * qw_ref[:, :QK_NOPE]).astype(o_ref.dtype)
        r = _rope_mla(rope * rstd * qw_ref[:, QK_NOPE:], c_ref, s_ref)
        o_ref[:, c0 + QK_NOPE:c0 + MLA_QK_PAD] = r.astype(o_ref.dtype)

    return [(None, fn)]


def _epi_mla_kv(extras, outs):
    _, kpe_ref, kwn_ref, kwp_ref, first_ref, c_ref, s_ref = extras
    k_ref, v_ref = outs
    assert MXU_COLS == QK_NOPE + V_HEAD
    kpe = kpe_ref[...]
    ss_pe = jnp.sum(kpe * kpe * first_ref[...], axis=-1, keepdims=True)
    kr = _rope_mla(kpe * kwp_ref[...], c_ref, s_ref)

    @_via_ref
    def fn(a, c0):
        head = c0 // (QK_NOPE + V_HEAD)
        kn = a[:, :QK_NOPE]
        rstd = lax.rsqrt((jnp.sum(kn * kn, axis=-1, keepdims=True) + ss_pe) * (1.0 / QK_HEAD) + EPS)
        kb = head * MLA_QK_PAD
        k_ref[:, kb:kb + QK_NOPE] = (kn * rstd * kwn_ref[...]).astype(k_ref.dtype)
        k_ref[:, kb + QK_NOPE:kb + MLA_QK_PAD] = (kr * rstd).astype(k_ref.dtype)
        v_ref[:, head * V_HEAD:(head + 1) * V_HEAD] = a[:, QK_NOPE:].astype(v_ref.dtype)

    return [(None, fn)]


def _qk(q, k):
    return lax.dot_general(q, k, (((1,), (1,)), ((), ())), preferred_element_type=F32)


def _sb_kernel(q_ref, k_ref, v_ref, g_ref, u_ref, o_ref, acc_ref, carry_ref, *, tile):
    qi = pl.program_id(2)
    n_sub = q_ref.shape[0] // tile

    def tile_rows(kb):
        return pl.ds(pl.multiple_of(kb * tile, tile), tile)

    def weights(q, kb, carry, diagonal=False, valid=None):
        z = _qk(q, k_ref[tile_rows(kb), :])
        sp = jnp.log(1.0 + jnp.exp(-jnp.abs(z)))
        log_beta = jnp.minimum(z, 0.0) - sp
        log_keep = jnp.minimum(-z, 0.0) - sp
        keep = valid
        if diagonal:
            row = lax.broadcasted_iota(jnp.int32, z.shape, 0)
            col = lax.broadcasted_iota(jnp.int32, z.shape, 1)
            keep = col < row
        if keep is not None:
            log_keep = jnp.where(keep, log_keep, 0.0)
        hi = log_keep.astype(BF16)
        lo = (log_keep - hi.astype(F32)).astype(BF16)
        u = u_ref[...]
        excl = jnp.dot(hi, u, preferred_element_type=F32) + jnp.dot(lo, u, preferred_element_type=F32)
        if carry is not None:
            excl = excl + carry
        w = jnp.exp(log_beta + excl)
        if keep is not None:
            w = jnp.where(keep, w, 0.0)
        return w.astype(BF16), jnp.sum(log_keep, axis=-1, keepdims=True)

    def pv(w, kb):
        return jnp.dot(w, v_ref[tile_rows(kb), :], preferred_element_type=F32)

    tails = []
    for sub in range(n_sub):
        rows = slice(sub * tile, (sub + 1) * tile)
        q = q_ref[rows, :]
        diag = qi * n_sub + sub
        prev = jnp.maximum(diag - 1, 0)
        w_d, sum_d = weights(q, diag, None, diagonal=True)
        w_p, sum_p = weights(q, prev, sum_d, valid=diag > 0)
        carry = sum_d + sum_p
        acc_ref[rows, :] = pv(w_d, diag) + pv(w_p, prev)
        carry_ref[rows, :] = carry
        tails.append((rows, diag - 2, jnp.max(carry)))

    for rows, kb0, cmax0 in tails:
        def cond(state):
            kb, cmax = state
            return (kb >= 0) & (cmax > EXP_UNDERFLOW)

        def body(state, rows=rows):
            kb, _ = state
            carry = carry_ref[rows, :]
            w, sum_k = weights(q_ref[rows, :], kb, carry)
            acc_ref[rows, :] += pv(w, kb)
            carry = carry + sum_k
            carry_ref[rows, :] = carry
            return kb - 1, jnp.max(carry)

        lax.while_loop(cond, body, (kb0, cmax0))

    g = g_ref[...].astype(F32)
    o_ref[...] = (acc_ref[...] * (g * jax.nn.sigmoid(g))).astype(o_ref.dtype)


def _dl_kernel(q_ref, k_ref, v_ref, g_ref, bias_ref, o_ref, qf_ref, kf_ref, vf_ref, acc_ref, m_ref, l_ref,
               *, chunk):
    ci = pl.program_id(2)

    @pl.when(ci == 0)
    def _():
        kf_ref[...] = k_ref[...].astype(F32)
        vf_ref[...] = v_ref[...].astype(F32)

    qf_ref[...] = q_ref[...].astype(F32)
    base = ci * chunk

    def body(idx, _):
        for pi, (_, dil) in enumerate(DIL_PATTERNS):
            unit = DL_TILE * dil
            q0 = (idx // dil) * unit + idx % dil
            rows = pl.ds(q0, DL_TILE, stride=dil)
            cur0 = base + q0
            has_prev = cur0 >= unit
            prev0 = jnp.maximum(cur0 - unit, 0)
            prev_rows = pl.ds(prev0, DL_TILE, stride=dil)
            cur_rows = pl.ds(cur0, DL_TILE, stride=dil)
            kk = jnp.concatenate([kf_ref[prev_rows, :], kf_ref[cur_rows, :]], axis=0).astype(BF16)
            vv = jnp.concatenate([vf_ref[prev_rows, :], vf_ref[cur_rows, :]], axis=0).astype(BF16)
            s = _qk(qf_ref[rows, :].astype(BF16), kk) + bias_ref[pi, has_prev.astype(jnp.int32)]
            m = jnp.max(s, axis=-1, keepdims=True)
            p = jnp.exp2(s - m)
            l = jnp.sum(p, axis=-1, keepdims=True)
            acc_ref[pi, rows, :] = jnp.dot(p.astype(BF16), vv, preferred_element_type=F32)
            m_ref[pi, rows, :] = jnp.broadcast_to(m, (DL_TILE, LANES))
            l_ref[pi, rows, :] = jnp.broadcast_to(l, (DL_TILE, LANES))
        return 0

    lax.fori_loop(0, chunk // DL_TILE, body, 0, unroll=8)

    n_pat = len(DIL_PATTERNS)
    m_all = functools.reduce(jnp.maximum, [m_ref[pi] for pi in range(n_pat)])
    num = jnp.zeros(acc_ref.shape[1:], F32)
    den = jnp.zeros(acc_ref.shape[1:], F32)
    for pi in range(n_pat):
        w = jnp.exp2(m_ref[pi] - m_all)
        num = num + w * acc_ref[pi]
        den = den + w * l_ref[pi]
    g = g_ref[...].astype(F32)
    o_ref[...] = (num / den * (g * jax.nn.sigmoid(g))).astype(o_ref.dtype)


def _softmax_update(s_ref, bias_ref, v, p_ref, acc_ref, m_ref, l_ref):
    tq, tk = s_ref.shape
    for r0 in range(0, tq, MLA_ROW_BLOCK):
        rs = slice(r0, r0 + MLA_ROW_BLOCK)
        cols = [s_ref[rs, c * LANES:(c + 1) * LANES] for c in range(tk // LANES)]
        if bias_ref is not None:
            cols = [col + bias_ref[rs, c * LANES:(c + 1) * LANES] for c, col in enumerate(cols)]
        m_old = m_ref[rs, :]
        tile_max = jnp.max(functools.reduce(jnp.maximum, cols), axis=-1, keepdims=True)
        m_new = jnp.maximum(m_old, tile_max)
        ps = [jnp.exp2(col - m_new) for col in cols]
        alpha = jnp.exp2(m_old - m_new)
        l_ref[rs, :] = alpha * l_ref[rs, :] + jnp.sum(functools.reduce(jnp.add, ps), axis=-1, keepdims=True)
        m_ref[rs, :] = m_new
        acc_ref[rs, :] = alpha * acc_ref[rs, :]
        for c, x in enumerate(ps):
            p_ref[rs, c * LANES:(c + 1) * LANES] = x.astype(BF16)
    acc_ref[...] += jnp.dot(p_ref[...], v, preferred_element_type=F32)


def _mla_kernel(q_ref, k_ref, v_ref, g_ref, bias_ref, o_ref, sa_ref, sb_ref, pa_ref, pb_ref, acc_ref, m_ref,
                l_ref, *, tile):
    n_q = q_ref.shape[0] // tile
    acc_ref[...] = jnp.zeros_like(acc_ref)
    m_ref[...] = jnp.full_like(m_ref, -jnp.inf)
    l_ref[...] = jnp.zeros_like(l_ref)

    def rows(t):
        return pl.ds(pl.multiple_of(t * tile, tile), tile)

    def scores(qi, kb, dst_ref):
        dst_ref[...] = _qk(q_ref[rows(qi), :], k_ref[rows(kb), :])

    def pipeline(n_steps, start, advance, bias):
        def update(qi, kb, src_ref, p_ref):
            r = rows(qi)
            _softmax_update(src_ref, bias, v_ref[rows(kb), :], p_ref, acc_ref.at[r, :], m_ref.at[r, :],
                            l_ref.at[r, :])

        scores(*start, sa_ref)

        def pair(_, state):
            qa, ka = state
            qb, kb = advance(qa, ka)
            scores(qb, kb, sb_ref)
            update(qa, ka, sa_ref, pa_ref)
            qa, ka = advance(qb, kb)
            scores(qa, ka, sa_ref)
            update(qb, kb, sb_ref, pb_ref)
            return qa, ka

        qa, ka = lax.fori_loop(0, n_steps // 2, pair, start, unroll=4)
        if n_steps % 2:
            update(qa, ka, sa_ref, pa_ref)

    def next_lower(qi, kb):
        wrap = kb == qi - 1
        return jnp.where(wrap, jnp.minimum(qi + 1, n_q - 1), qi), jnp.where(wrap, 0, kb + 1)

    def next_diagonal(qi, kb):
        t = jnp.minimum(qi + 1, n_q - 1)
        return t, t

    zero, one = jnp.int32(0), jnp.int32(1)
    if n_q > 1:
        pipeline(n_q * (n_q - 1) // 2, (one, zero), next_lower, None)
    pipeline(n_q, (zero, zero), next_diagonal, bias_ref)

    g = g_ref[...].astype(F32)
    o_ref[...] = (acc_ref[...] * g / (l_ref[...] * (1.0 + jnp.exp(-g)))).astype(o_ref.dtype)


def _attn_call(kern, arrays, col_blocks, widths, consts, out_width_total, out_col0, n_heads, tile,
               scratch, name):
    q, k, v, g = arrays
    b, s, _ = q.shape
    cq, ck, cv, cg = col_blocks
    wq, wk, wv, wg = widths
    in_specs = [
        pl.BlockSpec((None, tile, wq), lambda bi, h, i: (bi, i, cq + h)),
        pl.BlockSpec((None, s, wk), lambda bi, h, i: (bi, 0, ck + h)),
        pl.BlockSpec((None, s, wv), lambda bi, h, i: (bi, 0, cv + h)),
        pl.BlockSpec((None, tile, wg), lambda bi, h, i: (bi, i, cg + h)),
    ]
    for cst in consts:
        in_specs.append(pl.BlockSpec(cst.shape, lambda bi, h, i, nd=cst.ndim: (0,) * nd))
    return pl.pallas_call(
        kern,
        grid=(b, n_heads, s // tile),
        in_specs=in_specs,
        out_specs=pl.BlockSpec((None, tile, wv), lambda bi, h, i: (bi, i, out_col0 + h)),
        out_shape=jax.ShapeDtypeStruct((b, s, out_width_total), BF16),
        scratch_shapes=scratch,
        compiler_params=_cparams(("parallel", "parallel", "arbitrary")),
        name=name,
    )(q, k, v, g, *consts)


def _dl_band_bias():
    i = np.arange(DL_TILE)[:, None]
    c = np.arange(2 * DL_TILE)[None, :]
    out = []
    for window, dil in DIL_PATTERNS:
        span = window // dil
        assert span <= DL_TILE
        dist = i + DL_TILE - c
        band = (dist >= 0) & (dist <= span)
        out.append(np.stack([band & (c >= DL_TILE), band]))
    return np.where(np.stack(out), 0.0, -np.inf).astype(np.float32)


def _rope_angles(positions, dim):
    inv = ROPE_THETA ** (-jnp.arange(0, dim, 2, dtype=F32) / dim)
    ang = positions.astype(F32)[..., None] * inv
    return jnp.cos(ang), jnp.sin(ang)


def _even_layer(x, h, gate, li, w_in, qn, kn, w_out, cosf, sinf, sb_u, dl_bias):
    b, s, d = x.shape
    m = b * s
    n_in = w_in.shape[-1]
    w_part = n_in // 8
    n_heads = w_part // HEAD_DIM
    tm, tn = _tile(s, MM_TM), _tile(w_part, MM_TN)
    q_scale = 1.0 / math.sqrt(HEAD_DIM)
    proj = _matmul_ws(
        [h.reshape(m, d)], w_in, layer=li, tm=tm, tn=tn,
        extras=(qn.reshape(1, HEAD_DIM), kn.reshape(1, HEAD_DIM), cosf, sinf),
        extra_specs=[pl.BlockSpec((1, HEAD_DIM), lambda j, i: (0, 0)),
                     pl.BlockSpec((1, HEAD_DIM), lambda j, i: (0, 0)),
                     pl.BlockSpec((tm, HEAD_DIM), lambda j, i: (i, 0)),
                     pl.BlockSpec((tm, HEAD_DIM), lambda j, i: (i, 0))],
        out_shapes=[jax.ShapeDtypeStruct((m, n_in), BF16)],
        out_specs=[pl.BlockSpec((tm, tn), lambda j, i: (i, j))],
        epilogue=functools.partial(_epi_even, part_tiles=w_part // tn, q_scale=q_scale),
        name="even_in_proj")[0].reshape(b, s, n_in)

    hb = n_heads
    w4 = (HEAD_DIM,) * 4
    sb_rows = SB_TILE * SB_SUBTILES
    o_sb = _attn_call(
        functools.partial(_sb_kernel, tile=SB_TILE), (proj,) * 4, (0, hb, 2 * hb, 3 * hb), w4,
        (sb_u,), w_part, 0, n_heads, sb_rows,
        [pltpu.VMEM((sb_rows, HEAD_DIM), F32), pltpu.VMEM((sb_rows, 1), F32)], "stick_breaking")
    assert s % DL_CHUNK == 0
    n_pat = len(DIL_PATTERNS)
    o_dl = _attn_call(
        functools.partial(_dl_kernel, chunk=DL_CHUNK), (proj,) * 4,
        (4 * hb, 5 * hb, 6 * hb, 7 * hb), w4, (dl_bias,), w_part, 0, n_heads, DL_CHUNK,
        [pltpu.VMEM((DL_CHUNK, HEAD_DIM), F32), pltpu.VMEM((s, HEAD_DIM), F32),
         pltpu.VMEM((s, HEAD_DIM), F32), pltpu.VMEM((n_pat, DL_CHUNK, HEAD_DIM), F32),
         pltpu.VMEM((n_pat, DL_CHUNK, LANES), F32), pltpu.VMEM((n_pat, DL_CHUNK, LANES), F32)],
        "dilated_window")
    return _out_proj(x, [o_sb.reshape(m, w_part), o_dl.reshape(m, w_part)], w_out, li, gate)


def _out_proj(x, mixed_parts, w_out, li, gate):
    b, s, d = x.shape
    m = b * s
    tm, tn = _tile(s, MM_TM), _tile(d, MM_TN)
    rows_per_batch = s // tm
    return _matmul_ws(
        mixed_parts, w_out, layer=li, tm=tm, tn=tn,
        extras=(x.reshape(m, d), gate),
        extra_specs=[pl.BlockSpec((tm, tn), lambda j, i: (i, j)),
                     pl.BlockSpec((None, 1, tn), lambda j, i: (i // rows_per_batch, 0, j))],
        out_shapes=[jax.ShapeDtypeStruct((m, d), F32)],
        out_specs=[pl.BlockSpec((tm, tn), lambda j, i: (i, j))],
        epilogue=_epi_resid, name="out_proj")[0].reshape(b, s, d)


def _odd_layer(x, h, gate, li, w_in, q_lat_norm, kv_lat_norm, w_uq, w_ukv, q_norm, k_norm, w_out,
               rc, rs, mla_bias):
    b, s, d = x.shape
    m = b * s
    q_lora = q_lat_norm.shape[0]
    kv_lora = kv_lat_norm.shape[0]
    n_heads = w_uq.shape[1] // QK_HEAD
    lat_raw = q_lora + kv_lora + QK_ROPE
    n_lat = lat_raw + QK_ROPE
    assert n_lat % LANES == 0 and 2 * QK_ROPE == LANES
    h2 = h.reshape(m, d)
    tm, tk = _tile(s, 1024), _tile(d, 1024)
    first = (jnp.arange(LANES) < QK_ROPE).astype(F32).reshape(1, LANES)

    w_lat = jnp.concatenate([w_in[:, :lat_raw], w_in[:, lat_raw - QK_ROPE:lat_raw]], axis=1).astype(BF16)
    tm_lat = _tile(s, 512)
    lat = _matmul(h2, w_lat, tm=tm_lat, tn=n_lat, tk=tk, extras=(), extra_specs=[],
                  out_shapes=[jax.ShapeDtypeStruct((m, n_lat), F32)],
                  out_specs=[pl.BlockSpec((tm_lat, n_lat), lambda i, j, k: (i, j))],
                  epilogue=_epi_plain, name="odd_latent_proj")[0]
    n_g = w_in.shape[1] - lat_raw
    tn_g = _tile(n_g, MM_TN)
    g = _matmul_ws([h2], w_in[:, lat_raw:].astype(BF16), tm=tm, tn=tn_g, extras=(), extra_specs=[],
                   out_shapes=[jax.ShapeDtypeStruct((m, n_g), BF16)],
                   out_specs=[pl.BlockSpec((tm, tn_g), lambda j, i: (i, j))],
                   epilogue=_epi_plain, name="odd_gate_proj")[0]

    w_uq3 = w_uq.reshape(q_lora, n_heads, QK_HEAD)
    w_uq_p = jnp.concatenate([w_uq3, w_uq3[:, :, QK_NOPE:]], axis=-1)
    w_uq_p = w_uq_p.reshape(q_lora, n_heads * MLA_QK_PAD).astype(BF16)
    qw = (jnp.concatenate([q_norm, q_norm[QK_NOPE:]]) * (LOG2_E / math.sqrt(QK_HEAD))).reshape(1, MLA_QK_PAD)
    tn_q = _tile(n_heads * MLA_QK_PAD, 1024)
    rope_specs = [pl.BlockSpec((1, LANES), lambda i, j, k: (0, 0))] + \
        [pl.BlockSpec((tm, LANES), lambda i, j, k: (i, 0))] * 2
    q = _matmul(
        lat, w_uq_p, tm=tm, tn=tn_q, tk=q_lora,
        extras=(q_lat_norm.reshape(1, q_lora), qw, first, rc, rs),
        extra_specs=[pl.BlockSpec((1, q_lora), lambda i, j, k: (0, 0)),
                     pl.BlockSpec((1, MLA_QK_PAD), lambda i, j, k: (0, 0))] + rope_specs,
        out_shapes=[jax.ShapeDtypeStruct((m, n_heads * MLA_QK_PAD), BF16)],
        out_specs=[pl.BlockSpec((tm, tn_q), lambda i, j, k: (i, j))],
        epilogue=_epi_mla_q,
        prologue=_pro_rms, name="mla_q_proj")[0]

    assert q_lora % kv_lora == 0 and (q_lora + kv_lora) % LANES == 0
    kvw = QK_NOPE + V_HEAD
    tn_kv = _tile(n_heads * kvw, 1024)
    heads_per_tile = tn_kv // kvw
    kwn = k_norm[:QK_NOPE].reshape(1, QK_NOPE)
    kwp = jnp.concatenate([k_norm[QK_NOPE:], k_norm[QK_NOPE:]]).reshape(1, LANES)
    kpe_block = (q_lora + kv_lora) // LANES
    k, v = _matmul(
        lat, w_ukv.astype(BF16), tm=tm, tn=tn_kv, tk=kv_lora,
        extras=(kv_lat_norm.reshape(1, kv_lora), lat, kwn, kwp, first, rc, rs),
        extra_specs=[pl.BlockSpec((1, kv_lora), lambda i, j, k: (0, 0)),
                     pl.BlockSpec((tm, LANES), lambda i, j, k: (i, kpe_block)),
                     pl.BlockSpec((1, QK_NOPE), lambda i, j, k: (0, 0)),
                     pl.BlockSpec((1, LANES), lambda i, j, k: (0, 0))] + rope_specs,
        out_shapes=[jax.ShapeDtypeStruct((m, n_heads * MLA_QK_PAD), BF16),
                    jax.ShapeDtypeStruct((m, n_heads * V_HEAD), BF16)],
        out_specs=[pl.BlockSpec((tm, heads_per_tile * MLA_QK_PAD), lambda i, j, k: (i, j)),
                   pl.BlockSpec((tm, heads_per_tile * V_HEAD), lambda i, j, k: (i, j))],
        epilogue=_epi_mla_kv,
        prologue=_pro_rms, x_col_block=q_lora // kv_lora, name="mla_kv_proj")
    mixed = _attn_call(
        functools.partial(_mla_kernel, tile=MLA_TILE),
        (q.reshape(b, s, -1), k.reshape(b, s, -1), v.reshape(b, s, -1), g.reshape(b, s, -1)),
        (0, 0, 0, 0), (MLA_QK_PAD, MLA_QK_PAD, V_HEAD, V_HEAD), (mla_bias,), n_heads * V_HEAD, 0,
        n_heads, s,
        [pltpu.VMEM((MLA_TILE, MLA_TILE), F32), pltpu.VMEM((MLA_TILE, MLA_TILE), F32),
         pltpu.VMEM((MLA_TILE, MLA_TILE), BF16), pltpu.VMEM((MLA_TILE, MLA_TILE), BF16),
         pltpu.VMEM((s, V_HEAD), F32), pltpu.VMEM((s, LANES), F32),
         pltpu.VMEM((s, LANES), F32)], "mla_attention")
    return _out_proj(x, [mixed.reshape(m, -1)], w_out, li, gate)


def kernel(x, c, positions, ada_w, ada_b, norm_w, ev_w_in, ev_q_norm, ev_k_norm, ev_w_out, od_w_in,
           od_q_lat_norm, od_kv_lat_norm, od_w_uq, od_w_ukv, od_q_norm, od_k_norm, od_w_out):
    b, s, d = x.shape
    depth = ada_w.shape[0]
    m = b * s

    cos_f, sin_f = _rope_angles(positions, HEAD_DIM)
    cosf = jnp.concatenate([cos_f, cos_f], axis=-1).reshape(m, HEAD_DIM)
    sinf = jnp.concatenate([-sin_f, sin_f], axis=-1).reshape(m, HEAD_DIM)
    cos_m, sin_m = _rope_angles(positions, QK_ROPE)
    z32 = jnp.zeros_like(cos_m)
    z64 = jnp.concatenate([z32, z32], axis=-1)
    rc = jnp.concatenate([cos_m, cos_m, z64], axis=-1).reshape(m, LANES)
    rs = jnp.concatenate([-sin_m, sin_m, z64], axis=-1).reshape(m, LANES)

    sb_u = jnp.asarray(np.tril(np.ones((SB_TILE, SB_TILE), np.float32), -1), BF16)
    dl_bias = jnp.asarray(_dl_band_bias())
    causal = np.tril(np.ones((MLA_TILE, MLA_TILE), bool))
    mla_bias = jnp.asarray(np.where(causal, 0.0, -np.inf).astype(np.float32))

    mod = _modulation(c, ada_w, ada_b)
    shift, scale, gate = (mod[:, :, i * d:(i + 1) * d].reshape(depth, b, 1, d) for i in range(3))

    for layer in range(depth):
        h = _norm_mod(x, norm_w[layer], scale[layer], shift[layer])
        i = layer // 2
        if layer % 2 == 0:
            x = _even_layer(x, h, gate[layer], i, ev_w_in, ev_q_norm[i], ev_k_norm[i], ev_w_out,
                            cosf, sinf, sb_u, dl_bias)
        else:
            x = _odd_layer(x, h, gate[layer], i, od_w_in[i], od_q_lat_norm[i], od_kv_lat_norm[i],
                           od_w_uq[i], od_w_ukv[i], od_q_norm[i], od_k_norm[i], od_w_out,
                           rc, rs, mla_bias)
    return x
```

```python
import functools
import math

import numpy as np
import jax
import jax.numpy as jnp
from jax import lax
from jax.experimental import pallas as pl
from jax.experimental.pallas import tpu as pltpu

F32 = jnp.float32
BF16 = jnp.bfloat16

HEAD_DIM = 128
DIL_PATTERNS = ((128, 1), (512, 4), (2048, 16))
KV_LORA = 512
QK_NOPE = 128
QK_ROPE = 64
QK_HEAD = QK_NOPE + QK_ROPE
V_HEAD = 128
MLA_QK_PAD = 256
ROPE_THETA = 10000.0
EPS = 1e-6

LANES = 128
SUBLANES = 8
MXU_COLS = 256
VMEM_LIMIT = 56 * 1024 * 1024
EXP_UNDERFLOW = -104.0
LOG2_E = math.log2(math.e)
SB_TILE = 256
SB_SUBTILES = 8
DL_TILE = 128
DL_CHUNK = DL_TILE * max(d for _, d in DIL_PATTERNS)
MLA_TILE = 512
MLA_ROW_BLOCK = 64
MM_TM = 1024
MM_TN = 512


def _cparams(sem):
    return pltpu.CompilerParams(dimension_semantics=sem, vmem_limit_bytes=VMEM_LIMIT)


def _tile(n, want):
    if n <= want:
        return n
    t = (want // LANES) * LANES
    while n % t:
        t -= LANES
    return t


def _mod_kernel(c_ref, w_ref, b_ref, o_ref):
    a = c_ref[...]
    a = (a * jax.nn.sigmoid(a)).astype(BF16)
    o_ref[...] = jnp.dot(a, w_ref[...].astype(BF16), preferred_element_type=F32) + b_ref[...]


def _modulation(c, ada_w, ada_b):
    depth, d, n3 = ada_w.shape
    b = c.shape[0]
    rows = -(-b // SUBLANES) * SUBLANES
    c_pad = jnp.pad(c, ((0, rows - b), (0, 0)))
    tn = _tile(n3, 1024)
    out = pl.pallas_call(
        _mod_kernel,
        grid=(depth, n3 // tn),
        in_specs=[pl.BlockSpec((rows, d), lambda l, j: (0, 0)),
                  pl.BlockSpec((None, d, tn), lambda l, j: (l, 0, j)),
                  pl.BlockSpec((None, 1, tn), lambda l, j: (l, 0, j))],
        out_specs=pl.BlockSpec((None, rows, tn), lambda l, j: (l, 0, j)),
        out_shape=jax.ShapeDtypeStruct((depth, rows, n3), F32),
        compiler_params=_cparams(("parallel", "parallel")),
        name="modulation",
    )(c_pad, ada_w, ada_b.reshape(depth, 1, n3))
    return out[:, :b]


def _norm_mod_kernel(x_ref, w_ref, scale_ref, shift_ref, o_ref):
    x = x_ref[...]
    y = x * lax.rsqrt(jnp.mean(x * x, axis=-1, keepdims=True) + EPS)
    y = y * w_ref[...]
    o_ref[...] = (y * (1.0 + scale_ref[...]) + shift_ref[...]).astype(o_ref.dtype)


def _norm_mod(x, w, scale, shift):
    b, s, d = x.shape
    ts = _tile(s, 512)
    return pl.pallas_call(
        _norm_mod_kernel,
        grid=(b, s // ts),
        in_specs=[pl.BlockSpec((None, ts, d), lambda bi, i: (bi, i, 0)),
                  pl.BlockSpec((1, d), lambda bi, i: (0, 0)),
                  pl.BlockSpec((None, 1, d), lambda bi, i: (bi, 0, 0)),
                  pl.BlockSpec((None, 1, d), lambda bi, i: (bi, 0, 0))],
        out_specs=pl.BlockSpec((None, ts, d), lambda bi, i: (bi, i, 0)),
        out_shape=jax.ShapeDtypeStruct((b, s, d), BF16),
        compiler_params=_cparams(("parallel", "parallel")),
        name="norm_mod",
    )(x, w.reshape(1, d), scale, shift)


def _run_epilogue(variants, chunk_of, tn, acc_ref):
    chunks = [(c0, min(MXU_COLS, tn - c0)) for c0 in range(0, tn, MXU_COLS)]

    def fused(fn):
        for c0, w in chunks:
            fn(chunk_of(c0, w), c0)

    def matmul_only():
        acc_ref[...] = chunk_of(0, tn)

    def epilogue_only(fn):
        for c0, w in chunks:
            fn(acc_ref.at[:, c0:c0 + w], c0)

    split = [(cond, fn) for cond, fn in variants if getattr(fn, "via_ref", False) and cond is not None]
    if split:
        pl.when(functools.reduce(jnp.logical_or, [cond for cond, _ in split]))(matmul_only)
    for cond, fn in variants:
        if getattr(fn, "via_ref", False):
            if cond is None:
                matmul_only()
                epilogue_only(fn)
            else:
                pl.when(cond)(functools.partial(epilogue_only, fn))
        elif cond is None:
            fused(fn)
        else:
            pl.when(cond)(functools.partial(fused, fn))


def _via_ref(fn):
    fn.via_ref = True
    return fn


def _mm_kernel(*refs, n_extra, n_out, nk, tn, epilogue, prologue):
    x_ref, w_ref = refs[0], refs[1]
    extras = refs[2:2 + n_extra]
    outs = refs[2 + n_extra:2 + n_extra + n_out]
    scratch = refs[2 + n_extra + n_out:]
    acc_ref = scratch[0]
    if prologue is not None:
        h_ref = scratch[1]

        @pl.when(pl.program_id(1) == 0)
        def _():
            h_ref[...] = prologue(x_ref, extras)

        _run_epilogue(epilogue(extras, outs),
                      lambda c0, w: jnp.dot(h_ref[...], w_ref[:, c0:c0 + w], preferred_element_type=F32),
                      tn, acc_ref)
        return
    k = pl.program_id(2)

    @pl.when(k == 0)
    def _():
        acc_ref[...] = jnp.zeros_like(acc_ref)

    acc_ref[...] += jnp.dot(x_ref[...], w_ref[...], preferred_element_type=F32)

    @pl.when(k == nk - 1)
    def _():
        _run_epilogue(epilogue(extras, outs), lambda c0, w: acc_ref[:, c0:c0 + w], tn, acc_ref)


def _matmul(x, w, *, tm, tn, tk, extras, extra_specs, out_shapes, out_specs, epilogue,
            prologue=None, x_col_block=0, name):
    m = x.shape[0]
    kdim, n = w.shape
    nk = kdim // tk
    assert m % tm == 0 and n % tn == 0 and kdim % tk == 0 and x.shape[1] >= (x_col_block + nk) * tk
    scratch = [pltpu.VMEM((tm, tn), F32)]
    if prologue is not None:
        assert nk == 1
        scratch.append(pltpu.VMEM((tm, kdim), BF16))
    kern = functools.partial(_mm_kernel, n_extra=len(extras), n_out=len(out_shapes), nk=nk, tn=tn,
                             epilogue=epilogue, prologue=prologue)
    return pl.pallas_call(
        kern,
        grid=(m // tm, n // tn, nk),
        in_specs=[pl.BlockSpec((tm, tk), lambda i, j, k: (i, x_col_block + k)),
                  pl.BlockSpec((tk, tn), lambda i, j, k: (k, j))] + list(extra_specs),
        out_specs=list(out_specs),
        out_shape=list(out_shapes),
        scratch_shapes=scratch,
        compiler_params=_cparams(("parallel", "arbitrary", "arbitrary")),
        name=name,
    )(x, w, *extras)


def _mm_ws_kernel(*refs, k_parts, n_extra, n_out, tn, cast, epilogue):
    n_x = len(k_parts)
    x_refs = refs[:n_x]
    w_ref = refs[n_x]
    extras = refs[n_x + 1:n_x + 1 + n_extra]
    outs = refs[n_x + 1 + n_extra:n_x + 1 + n_extra + n_out]
    scratch = refs[n_x + 1 + n_extra + n_out:]
    acc_ref = scratch[0]
    if cast:
        wb_ref = scratch[1]

        @pl.when(pl.program_id(1) == 0)
        def _():
            wb_ref[...] = w_ref[...].astype(BF16)
    else:
        wb_ref = w_ref

    def chunk_of(c0, w):
        acc = None
        k0 = 0
        for x_ref, kp in zip(x_refs, k_parts):
            part = jnp.dot(x_ref[...], wb_ref[k0:k0 + kp, c0:c0 + w], preferred_element_type=F32)
            acc = part if acc is None else acc + part
            k0 += kp
        return acc

    _run_epilogue(epilogue(extras, outs), chunk_of, tn, acc_ref)


def _matmul_ws(xs, w, *, tm, tn, extras, extra_specs, out_shapes, out_specs, epilogue, name, layer=None):
    m = xs[0].shape[0]
    kdim, n = w.shape[-2:]
    if layer is None:
        w_spec = pl.BlockSpec((kdim, tn), lambda j, i: (0, j))
    else:
        w_spec = pl.BlockSpec((None, kdim, tn), lambda j, i: (layer, 0, j))
    k_parts = tuple(x.shape[1] for x in xs)
    assert m % tm == 0 and n % tn == 0 and sum(k_parts) == kdim
    cast = w.dtype != BF16
    scratch = [pltpu.VMEM((tm, tn), F32)]
    if cast:
        scratch.append(pltpu.VMEM((kdim, tn), BF16))
    kern = functools.partial(_mm_ws_kernel, k_parts=k_parts, n_extra=len(extras),
                             n_out=len(out_shapes), tn=tn, cast=cast, epilogue=epilogue)
    x_specs = [pl.BlockSpec((tm, kp), lambda j, i: (i, 0)) for kp in k_parts]
    return pl.pallas_call(
        kern,
        grid=(n // tn, m // tm),
        in_specs=x_specs + [w_spec] + list(extra_specs),
        out_specs=list(out_specs),
        out_shape=list(out_shapes),
        scratch_shapes=scratch,
        compiler_params=_cparams(("parallel", "arbitrary")),
        name=name,
    )(*xs, w, *extras)


def _epi_plain(extras, outs):
    o_ref = outs[0]

    def fn(a, c0):
        o_ref[:, c0:c0 + a.shape[1]] = a.astype(o_ref.dtype)

    return [(None, fn)]


def _epi_resid(extras, outs):
    xres_ref, gate_ref = extras
    o_ref = outs[0]

    def fn(a, c0):
        sl = slice(c0, c0 + a.shape[1])
        o_ref[:, sl] = xres_ref[:, sl] + gate_ref[:, sl] * a

    return [(None, fn)]


def _rope_full(y, cos_ref, sin_ref):
    return y * cos_ref[...] + pltpu.roll(y, HEAD_DIM // 2, 1) * sin_ref[...]


def _rope_mla(r, c_ref, s_ref):
    return r * c_ref[...] + pltpu.roll(r, QK_ROPE // 2, 1) * s_ref[...]


def _epi_even(extras, outs, *, part_tiles, q_scale):
    qn_ref, kn_ref, cos_ref, sin_ref = extras
    o_ref = outs[0]
    part = pl.program_id(0) // part_tiles

    def scaled(scale):
        def fn(a, c0):
            y = a if scale == 1.0 else a * scale
            o_ref[:, c0:c0 + a.shape[1]] = y.astype(o_ref.dtype)
        return fn

    def headnorm_rope(w_ref, scale):
        @_via_ref
        def fn(a, c0):
            for hh in range(a.shape[1] // HEAD_DIM):
                ah = a[:, hh * HEAD_DIM:(hh + 1) * HEAD_DIM]
                y = ah * lax.rsqrt(jnp.mean(ah * ah, axis=-1, keepdims=True) + EPS) * w_ref[...]
                y = _rope_full(y, cos_ref, sin_ref)
                if scale != 1.0:
                    y = y * scale
                o_ref[:, c0 + hh * HEAD_DIM:c0 + (hh + 1) * HEAD_DIM] = y.astype(o_ref.dtype)
        return fn

    return [(part == 0, scaled(q_scale)),
            (part == 4, headnorm_rope(qn_ref, q_scale * LOG2_E)),
            (part == 5, headnorm_rope(kn_ref, 1.0)),
            ((part != 0) & (part != 4) & (part != 5), scaled(1.0))]


def _pro_rms(x_ref, extras):
    x = x_ref[...]
    w_ref = extras[0]
    y = x * lax.rsqrt(jnp.mean(x * x, axis=-1, keepdims=True) + EPS) * w_ref[...]
    return y.astype(BF16)


def _epi_mla_q(extras, outs):
    _, qw_ref, first_ref, c_ref, s_ref = extras
    o_ref = outs[0]
    assert MXU_COLS == MLA_QK_PAD

    @_via_ref
    def fn(a, c0):
        nope, rope = a[:, :QK_NOPE], a[:, QK_NOPE:]
        sq = nope * nope + rope * rope * first_ref[...]
        rstd = lax.rsqrt(jnp.sum(sq, axis=-1, keepdims=True) * (1.0 / QK_HEAD) + EPS)
        o_ref[:, c0:c0 + QK_NOPE] = (nope * rstd * qw_ref[:, :QK_NOPE]).astype(o_ref.dtype)
        r = _rope_mla(rope * rstd * qw_ref[:, QK_NOPE:], c_ref, s_ref)
        o_ref[:, c0 + QK_NOPE:c0 + MLA_QK_PAD] = r.astype(o_ref.dtype)

    return [(None, fn)]


def _epi_mla_kv(extras, outs):
    _, kpe_ref, kwn_ref, kwp_ref, first_ref, c_ref, s_ref = extras
    k_ref, v_ref = outs
    assert MXU_COLS == QK_NOPE + V_HEAD
    kpe = kpe_ref[...]
    ss_pe = jnp.sum(kpe * kpe * first_ref[...], axis=-1, keepdims=True)
    kr = _rope_mla(kpe * kwp_ref[...], c_ref, s_ref)

    @_via_ref
    def fn(a, c0):
        head = c0 // (QK_NOPE + V_HEAD)
        kn = a[:, :QK_NOPE]
        rstd = lax.rsqrt((jnp.sum(kn * kn, axis=-1, keepdims=True) + ss_pe) * (1.0 / QK_HEAD) + EPS)
        kb = head * MLA_QK_PAD
        k_ref[:, kb:kb + QK_NOPE] = (kn * rstd * kwn_ref[...]).astype(k_ref.dtype)
        k_ref[:, kb + QK_NOPE:kb + MLA_QK_PAD] = (kr * rstd).astype(k_ref.dtype)
        v_ref[:, head * V_HEAD:(head + 1) * V_HEAD] = a[:, QK_NOPE:].astype(v_ref.dtype)

    return [(None, fn)]


def _qk(q, k):
    return lax.dot_general(q, k, (((1,), (1,)), ((), ())), preferred_element_type=F32)


def _sb_kernel(q_ref, k_ref, v_ref, g_ref, u_ref, o_ref, acc_ref, carry_ref, *, tile):
    qi = pl.program_id(2)
    n_sub = q_ref.shape[0] // tile

    def tile_rows(kb):
        return pl.ds(pl.multiple_of(kb * tile, tile), tile)

    def weights(q, kb, carry, diagonal=False, valid=None):
        z = _qk(q, k_ref[tile_rows(kb), :])
        sp = jnp.log(1.0 + jnp.exp(-jnp.abs(z)))
        log_beta = jnp.minimum(z, 0.0) - sp
        log_keep = jnp.minimum(-z, 0.0) - sp
        keep = valid
        if diagonal:
            row = lax.broadcasted_iota(jnp.int32, z.shape, 0)
            col = lax.broadcasted_iota(jnp.int32, z.shape, 1)
            keep = col < row
        if keep is not None:
            log_keep = jnp.where(keep, log_keep, 0.0)
        hi = log_keep.astype(BF16)
        lo = (log_keep - hi.astype(F32)).astype(BF16)
        u = u_ref[...]
        excl = jnp.dot(hi, u, preferred_element_type=F32) + jnp.dot(lo, u, preferred_element_type=F32)
        if carry is not None:
            excl = excl + carry
        w = jnp.exp(log_beta + excl)
        if keep is not None:
            w = jnp.where(keep, w, 0.0)
        return w.astype(BF16), jnp.sum(log_keep, axis=-1, keepdims=True)

    def pv(w, kb):
        return jnp.dot(w, v_ref[tile_rows(kb), :], preferred_element_type=F32)

    tails = []
    for sub in range(n_sub):
        rows = slice(sub * tile, (sub + 1) * tile)
        q = q_ref[rows, :]
        diag = qi * n_sub + sub
        prev = jnp.maximum(diag - 1, 0)
        w_d, sum_d = weights(q, diag, None, diagonal=True)
        w_p, sum_p = weights(q, prev, sum_d, valid=diag > 0)
        carry = sum_d + sum_p
        acc_ref[rows, :] = pv(w_d, diag) + pv(w_p, prev)
        carry_ref[rows, :] = carry
        tails.append((rows, diag - 2, jnp.max(carry)))

    for rows, kb0, cmax0 in tails:
        def cond(state):
            kb, cmax = state
            return (kb >= 0) & (cmax > EXP_UNDERFLOW)

        def body(state, rows=rows):
            kb, _ = state
            carry = carry_ref[rows, :]
            w, sum_k = weights(q_ref[rows, :], kb, carry)
            acc_ref[rows, :] += pv(w, kb)
            carry = carry + sum_k
            carry_ref[rows, :] = carry
            return kb - 1, jnp.max(carry)

        lax.while_loop(cond, body, (kb0, cmax0))

    g = g_ref[...].astype(F32)
    o_ref[...] = (acc_ref[...] * (g * jax.nn.sigmoid(g))).astype(o_ref.dtype)


def _dl_kernel(q_ref, k_ref, v_ref, g_ref, bias_ref, o_ref, qf_ref, kf_ref, vf_ref, acc_ref, m_ref, l_ref,
               *, chunk):
    ci = pl.program_id(2)

    @pl.when(ci == 0)
    def _():
        kf_ref[...] = k_ref[...].astype(F32)
        vf_ref[...] = v_ref[...].astype(F32)

    qf_ref[...] = q_ref[...].astype(F32)
    base = ci * chunk

    def body(idx, _):
        for pi, (_, dil) in enumerate(DIL_PATTERNS):
            unit = DL_TILE * dil
            q0 = (idx // dil) * unit + idx % dil
            rows = pl.ds(q0, DL_TILE, stride=dil)
            cur0 = base + q0
            has_prev = cur0 >= unit
            prev0 = jnp.maximum(cur0 - unit, 0)
            prev_rows = pl.ds(prev0, DL_TILE, stride=dil)
            cur_rows = pl.ds(cur0, DL_TILE, stride=dil)
            kk = jnp.concatenate([kf_ref[prev_rows, :], kf_ref[cur_rows, :]], axis=0).astype(BF16)
            vv = jnp.concatenate([vf_ref[prev_rows, :], vf_ref[cur_rows, :]], axis=0).astype(BF16)
            s = _qk(qf_ref[rows, :].astype(BF16), kk) + bias_ref[pi, has_prev.astype(jnp.int32)]
            m = jnp.max(s, axis=-1, keepdims=True)
            p = jnp.exp2(s - m)
            l = jnp.sum(p, axis=-1, keepdims=True)
            acc_ref[pi, rows, :] = jnp.dot(p.astype(BF16), vv, preferred_element_type=F32)
            m_ref[pi, rows, :] = jnp.broadcast_to(m, (DL_TILE, LANES))
            l_ref[pi, rows, :] = jnp.broadcast_to(l, (DL_TILE, LANES))
        return 0

    lax.fori_loop(0, chunk // DL_TILE, body, 0, unroll=True)

    n_pat = len(DIL_PATTERNS)
    m_all = functools.reduce(jnp.maximum, [m_ref[pi] for pi in range(n_pat)])
    num = jnp.zeros(acc_ref.shape[1:], F32)
    den = jnp.zeros(acc_ref.shape[1:], F32)
    for pi in range(n_pat):
        w = jnp.exp2(m_ref[pi] - m_all)
        num = num + w * acc_ref[pi]
        den = den + w * l_ref[pi]
    g = g_ref[...].astype(F32)
    o_ref[...] = (num / den * (g * jax.nn.sigmoid(g))).astype(o_ref.dtype)


def _softmax_update(s_ref, bias_ref, v, p_ref, acc_ref, m_ref, l_ref):
    tq, tk = s_ref.shape
    for r0 in range(0, tq, MLA_ROW_BLOCK):
        rs = slice(r0, r0 + MLA_ROW_BLOCK)
        cols = [s_ref[rs, c * LANES:(c + 1) * LANES] for c in range(tk // LANES)]
        if bias_ref is not None:
            cols = [col + bias_ref[rs, c * LANES:(c + 1) * LANES] for c, col in enumerate(cols)]
        m_old = m_ref[rs, :]
        tile_max = jnp.max(functools.reduce(jnp.maximum, cols), axis=-1, keepdims=True)
        m_new = jnp.maximum(m_old, tile_max)
        ps = [jnp.exp2(col - m_new) for col in cols]
        alpha = jnp.exp2(m_old - m_new)
        l_ref[rs, :] = alpha * l_ref[rs, :] + jnp.sum(functools.reduce(jnp.add, ps), axis=-1, keepdims=True)
        m_ref[rs, :] = m_new
        acc_ref[rs, :] = alpha * acc_ref[rs, :]
        for c, x in enumerate(ps):
            p_ref[rs, c * LANES:(c + 1) * LANES] = x.astype(BF16)
    acc_ref[...] += jnp.dot(p_ref[...], v, preferred_element_type=F32)


def _mla_kernel(q_ref, k_ref, v_ref, g_ref, bias_ref, o_ref, sa_ref, sb_ref, pa_ref, pb_ref, acc_ref, m_ref,
                l_ref, *, tile):
    n_q = q_ref.shape[0] // tile
    acc_ref[...] = jnp.zeros_like(acc_ref)
    m_ref[...] = jnp.full_like(m_ref, -jnp.inf)
    l_ref[...] = jnp.zeros_like(l_ref)

    def rows(t):
        return pl.ds(pl.multiple_of(t * tile, tile), tile)

    def scores(qi, kb, dst_ref):
        dst_ref[...] = _qk(q_ref[rows(qi), :], k_ref[rows(kb), :])

    def pipeline(n_steps, start, advance, bias):
        def update(qi, kb, src_ref, p_ref):
            r = rows(qi)
            _softmax_update(src_ref, bias, v_ref[rows(kb), :], p_ref, acc_ref.at[r, :], m_ref.at[r, :],
                            l_ref.at[r, :])

        scores(*start, sa_ref)

        def pair(_, state):
            qa, ka = state
            qb, kb = advance(qa, ka)
            scores(qb, kb, sb_ref)
            update(qa, ka, sa_ref, pa_ref)
            qa, ka = advance(qb, kb)
            scores(qa, ka, sa_ref)
            update(qb, kb, sb_ref, pb_ref)
            return qa, ka

        qa, ka = lax.fori_loop(0, n_steps // 2, pair, start, unroll=4)
        if n_steps % 2:
            update(qa, ka, sa_ref, pa_ref)

    def next_lower(qi, kb):
        wrap = kb == qi - 1
        return jnp.where(wrap, jnp.minimum(qi + 1, n_q - 1), qi), jnp.where(wrap, 0, kb + 1)

    def next_diagonal(qi, kb):
        t = jnp.minimum(qi + 1, n_q - 1)
        return t, t

    zero, one = jnp.int32(0), jnp.int32(1)
    if n_q > 1:
        pipeline(n_q * (n_q - 1) // 2, (one, zero), next_lower, None)
    pipeline(n_q, (zero, zero), next_diagonal, bias_ref)

    g = g_ref[...].astype(F32)
    o_ref[...] = (acc_ref[...] * g / (l_ref[...] * (1.0 + jnp.exp(-g)))).astype(o_ref.dtype)


def _attn_call(kern, arrays, col_blocks, widths, consts, out_width_total, out_col0, n_heads, tile,
               scratch, name):
    q, k, v, g = arrays
    b, s, _ = q.shape
    cq, ck, cv, cg = col_blocks
    wq, wk, wv, wg = widths
    in_specs = [
        pl.BlockSpec((None, tile, wq), lambda bi, h, i: (bi, i, cq + h)),
        pl.BlockSpec((None, s, wk), lambda bi, h, i: (bi, 0, ck + h)),
        pl.BlockSpec((None, s, wv), lambda bi, h, i: (bi, 0, cv + h)),
        pl.BlockSpec((None, tile, wg), lambda bi, h, i: (bi, i, cg + h)),
    ]
    for cst in consts:
        in_specs.append(pl.BlockSpec(cst.shape, lambda bi, h, i, nd=cst.ndim: (0,) * nd))
    return pl.pallas_call(
        kern,
        grid=(b, n_heads, s // tile),
        in_specs=in_specs,
        out_specs=pl.BlockSpec((None, tile, wv), lambda bi, h, i: (bi, i, out_col0 + h)),
        out_shape=jax.ShapeDtypeStruct((b, s, out_width_total), BF16),
        scratch_shapes=scratch,
        compiler_params=_cparams(("parallel", "parallel", "arbitrary")),
        name=name,
    )(q, k, v, g, *consts)


def _dl_band_bias():
    i = np.arange(DL_TILE)[:, None]
    c = np.arange(2 * DL_TILE)[None, :]
    out = []
    for window, dil in DIL_PATTERNS:
        span = window // dil
        assert span <= DL_TILE
        dist = i + DL_TILE - c
        band = (dist >= 0) & (dist <= span)
        out.append(np.stack([band & (c >= DL_TILE), band]))
    return np.where(np.stack(out), 0.0, -np.inf).astype(np.float32)


def _rope_angles(positions, dim):
    inv = ROPE_THETA ** (-jnp.arange(0, dim, 2, dtype=F32) / dim)
    ang = positions.astype(F32)[..., None] * inv
    return jnp.cos(ang), jnp.sin(ang)


def _even_layer(x, h, gate, li, w_in, qn, kn, w_out, cosf, sinf, sb_u, dl_bias):
    b, s, d = x.shape
    m = b * s
    n_in = w_in.shape[-1]
    w_part = n_in // 8
    n_heads = w_part // HEAD_DIM
    tm, tn = _tile(s, MM_TM), _tile(w_part, MM_TN)
    q_scale = 1.0 / math.sqrt(HEAD_DIM)
    proj = _matmul_ws(
        [h.reshape(m, d)], w_in, layer=li, tm=tm, tn=tn,
        extras=(qn.reshape(1, HEAD_DIM), kn.reshape(1, HEAD_DIM), cosf, sinf),
        extra_specs=[pl.BlockSpec((1, HEAD_DIM), lambda j, i: (0, 0)),
                     pl.BlockSpec((1, HEAD_DIM), lambda j, i: (0, 0)),
                     pl.BlockSpec((tm, HEAD_DIM), lambda j, i: (i, 0)),
                     pl.BlockSpec((tm, HEAD_DIM), lambda j, i: (i, 0))],
        out_shapes=[jax.ShapeDtypeStruct((m, n_in), BF16)],
        out_specs=[pl.BlockSpec((tm, tn), lambda j, i: (i, j))],
        epilogue=functools.partial(_epi_even, part_tiles=w_part // tn, q_scale=q_scale),
        name="even_in_proj")[0].reshape(b, s, n_in)

    hb = n_heads
    w4 = (HEAD_DIM,) * 4
    sb_rows = SB_TILE * SB_SUBTILES
    o_sb = _attn_call(
        functools.partial(_sb_kernel, tile=SB_TILE), (proj,) * 4, (0, hb, 2 * hb, 3 * hb), w4,
        (sb_u,), w_part, 0, n_heads, sb_rows,
        [pltpu.VMEM((sb_rows, HEAD_DIM), F32), pltpu.VMEM((sb_rows, 1), F32)], "stick_breaking")
    assert s % DL_CHUNK == 0
    n_pat = len(DIL_PATTERNS)
    o_dl = _attn_call(
        functools.partial(_dl_kernel, chunk=DL_CHUNK), (proj,) * 4,
        (4 * hb, 5 * hb, 6 * hb, 7 * hb), w4, (dl_bias,), w_part, 0, n_heads, DL_CHUNK,
        [pltpu.VMEM((DL_CHUNK, HEAD_DIM), F32), pltpu.VMEM((s, HEAD_DIM), F32),
         pltpu.VMEM((s, HEAD_DIM), F32), pltpu.VMEM((n_pat, DL_CHUNK, HEAD_DIM), F32),
         pltpu.VMEM((n_pat, DL_CHUNK, LANES), F32), pltpu.VMEM((n_pat, DL_CHUNK, LANES), F32)],
        "dilated_window")
    return _out_proj(x, [o_sb.reshape(m, w_part), o_dl.reshape(m, w_part)], w_out, li, gate)


def _out_proj(x, mixed_parts, w_out, li, gate):
    b, s, d = x.shape
    m = b * s
    tm, tn = _tile(s, MM_TM), _tile(d, MM_TN)
    rows_per_batch = s // tm
    return _matmul_ws(
        mixed_parts, w_out, layer=li, tm=tm, tn=tn,
        extras=(x.reshape(m, d), gate),
        extra_specs=[pl.BlockSpec((tm, tn), lambda j, i: (i, j)),
                     pl.BlockSpec((None, 1, tn), lambda j, i: (i // rows_per_batch, 0, j))],
        out_shapes=[jax.ShapeDtypeStruct((m, d), F32)],
        out_specs=[pl.BlockSpec((tm, tn), lambda j, i: (i, j))],
        epilogue=_epi_resid, name="out_proj")[0].reshape(b, s, d)


def _odd_layer(x, h, gate, li, w_in, q_lat_norm, kv_lat_norm, w_uq, w_ukv, q_norm, k_norm, w_out,
               rc, rs, mla_bias):
    b, s, d = x.shape
    m = b * s
    q_lora = q_lat_norm.shape[0]
    kv_lora = kv_lat_norm.shape[0]
    n_heads = w_uq.shape[1] // QK_HEAD
    lat_raw = q_lora + kv_lora + QK_ROPE
    n_lat = lat_raw + QK_ROPE
    assert n_lat % LANES == 0 and 2 * QK_ROPE == LANES
    h2 = h.reshape(m, d)
    tm, tk = _tile(s, 1024), _tile(d, 1024)
    first = (jnp.arange(LANES) < QK_ROPE).astype(F32).reshape(1, LANES)

    w_lat = jnp.concatenate([w_in[:, :lat_raw], w_in[:, lat_raw - QK_ROPE:lat_raw]], axis=1).astype(BF16)
    tm_lat = _tile(s, 512)
    lat = _matmul(h2, w_lat, tm=tm_lat, tn=n_lat, tk=tk, extras=(), extra_specs=[],
                  out_shapes=[jax.ShapeDtypeStruct((m, n_lat), F32)],
                  out_specs=[pl.BlockSpec((tm_lat, n_lat), lambda i, j, k: (i, j))],
                  epilogue=_epi_plain, name="odd_latent_proj")[0]
    n_g = w_in.shape[1] - lat_raw
    tn_g = _tile(n_g, MM_TN)
    g = _matmul_ws([h2], w_in[:, lat_raw:].astype(BF16), tm=tm, tn=tn_g, extras=(), extra_specs=[],
                   out_shapes=[jax.ShapeDtypeStruct((m, n_g), BF16)],
                   out_specs=[pl.BlockSpec((tm, tn_g), lambda j, i: (i, j))],
                   epilogue=_epi_plain, name="odd_gate_proj")[0]

    w_uq3 = w_uq.reshape(q_lora, n_heads, QK_HEAD)
    w_uq_p = jnp.concatenate([w_uq3, w_uq3[:, :, QK_NOPE:]], axis=-1)
    w_uq_p = w_uq_p.reshape(q_lora, n_heads * MLA_QK_PAD).astype(BF16)
    qw = (jnp.concatenate([q_norm, q_norm[QK_NOPE:]]) * (LOG2_E / math.sqrt(QK_HEAD))).reshape(1, MLA_QK_PAD)
    tn_q = _tile(n_heads * MLA_QK_PAD, 1024)
    rope_specs = [pl.BlockSpec((1, LANES), lambda i, j, k: (0, 0))] + \
        [pl.BlockSpec((tm, LANES), lambda i, j, k: (i, 0))] * 2
    q = _matmul(
        lat, w_uq_p, tm=tm, tn=tn_q, tk=q_lora,
        extras=(q_lat_norm.reshape(1, q_lora), qw, first, rc, rs),
        extra_specs=[pl.BlockSpec((1, q_lora), lambda i, j, k: (0, 0)),
                     pl.BlockSpec((1, MLA_QK_PAD), lambda i, j, k: (0, 0))] + rope_specs,
        out_shapes=[jax.ShapeDtypeStruct((m, n_heads * MLA_QK_PAD), BF16)],
        out_specs=[pl.BlockSpec((tm, tn_q), lambda i, j, k: (i, j))],
        epilogue=_epi_mla_q,
        prologue=_pro_rms, name="mla_q_proj")[0]

    assert q_lora % kv_lora == 0 and (q_lora + kv_lora) % LANES == 0
    kvw = QK_NOPE + V_HEAD
    tn_kv = _tile(n_heads * kvw, 1024)
    heads_per_tile = tn_kv // kvw
    kwn = k_norm[:QK_NOPE].reshape(1, QK_NOPE)
    kwp = jnp.concatenate([k_norm[QK_NOPE:], k_norm[QK_NOPE:]]).reshape(1, LANES)
    kpe_block = (q_lora + kv_lora) // LANES
    k, v = _matmul(
        lat, w_ukv.astype(BF16), tm=tm, tn=tn_kv, tk=kv_lora,
        extras=(kv_lat_norm.reshape(1, kv_lora), lat, kwn, kwp, first, rc, rs),
        extra_specs=[pl.BlockSpec((1, kv_lora), lambda i, j, k: (0, 0)),
                     pl.BlockSpec((tm, LANES), lambda i, j, k: (i, kpe_block)),
                     pl.BlockSpec((1, QK_NOPE), lambda i, j, k: (0, 0)),
                     pl.BlockSpec((1, LANES), lambda i, j, k: (0, 0))] + rope_specs,
        out_shapes=[jax.ShapeDtypeStruct((m, n_heads * MLA_QK_PAD), BF16),
                    jax.ShapeDtypeStruct((m, n_heads * V_HEAD), BF16)],
        out_specs=[pl.BlockSpec((tm, heads_per_tile * MLA_QK_PAD), lambda i, j, k: (i, j)),
                   pl.BlockSpec((tm, heads_per_tile * V_HEAD), lambda i, j, k: (i, j))],
        epilogue=_epi_mla_kv,
        prologue=_pro_rms, x_col_block=q_lora // kv_lora, name="mla_kv_proj")
    mixed = _attn_call(
        functools.partial(_mla_kernel, tile=MLA_TILE),
        (q.reshape(b, s, -1), k.reshape(b, s, -1), v.reshape(b, s, -1), g.reshape(b, s, -1)),
        (0, 0, 0, 0), (MLA_QK_PAD, MLA_QK_PAD, V_HEAD, V_HEAD), (mla_bias,), n_heads * V_HEAD, 0,
        n_heads, s,
        [pltpu.VMEM((MLA_TILE, MLA_TILE), F32), pltpu.VMEM((MLA_TILE, MLA_TILE), F32),
         pltpu.VMEM((MLA_TILE, MLA_TILE), BF16), pltpu.VMEM((MLA_TILE, MLA_TILE), BF16),
         pltpu.VMEM((s, V_HEAD), F32), pltpu.VMEM((s, LANES), F32),
         pltpu.VMEM((s, LANES), F32)], "mla_attention")
    return _out_proj(x, [mixed.reshape(m, -1)], w_out, li, gate)


def kernel(x, c, positions, ada_w, ada_b, norm_w, ev_w_in, ev_q_norm, ev_k_norm, ev_w_out, od_w_in,
           od_q_lat_norm, od_kv_lat_norm, od_w_uq, od_w_ukv, od_q_norm, od_k_norm, od_w_out):
    b, s, d = x.shape
    depth = ada_w.shape[0]
    m = b * s

    cos_f, sin_f = _rope_angles(positions, HEAD_DIM)
    cosf = jnp.concatenate([cos_f, cos_f], axis=-1).reshape(m, HEAD_DIM)
    sinf = jnp.concatenate([-sin_f, sin_f], axis=-1).reshape(m, HEAD_DIM)
    cos_m, sin_m = _rope_angles(positions, QK_ROPE)
    z32 = jnp.zeros_like(cos_m)
    z64 = jnp.concatenate([z32, z32], axis=-1)
    rc = jnp.concatenate([cos_m, cos_m, z64], axis=-1).reshape(m, LANES)
    rs = jnp.concatenate([-sin_m, sin_m, z64], axis=-1).reshape(m, LANES)

    sb_u = jnp.asarray(np.tril(np.ones((SB_TILE, SB_TILE), np.float32), -1), BF16)
    dl_bias = jnp.asarray(_dl_band_bias())
    causal = np.tril(np.ones((MLA_TILE, MLA_TILE), bool))
    mla_bias = jnp.asarray(np.where(causal, 0.0, -np.inf).astype(np.float32))

    mod = _modulation(c, ada_w, ada_b)
    shift, scale, gate = (mod[:, :, i * d:(i + 1) * d].reshape(depth, b, 1, d) for i in range(3))

    for layer in range(depth):
        h = _norm_mod(x, norm_w[layer], scale[layer], shift[layer])
        i = layer // 2
        if layer % 2 == 0:
            x = _even_layer(x, h, gate[layer], i, ev_w_in, ev_q_norm[i], ev_k_norm[i], ev_w_out,
                            cosf, sinf, sb_u, dl_bias)
        else:
            x = _odd_layer(x, h, gate[layer], i, od_w_in[i], od_q_lat_norm[i], od_kv_lat_norm[i],
                           od_w_uq[i], od_w_ukv[i], od_q_norm[i], od_k_norm[i], od_w_out,
                           rc, rs, mla_bias)
    return x
```

```python
import functools
import math

import numpy as np
import jax
import jax.numpy as jnp
from jax import lax
from jax.experimental import pallas as pl
from jax.experimental.pallas import tpu as pltpu

F32 = jnp.float32
BF16 = jnp.bfloat16

HEAD_DIM = 128
DIL_PATTERNS = ((128, 1), (512, 4), (2048, 16))
KV_LORA = 512
QK_NOPE = 128
QK_ROPE = 64
QK_HEAD = QK_NOPE + QK_ROPE
V_HEAD = 128
MLA_QK_PAD = 256
ROPE_THETA = 10000.0
EPS = 1e-6

LANES = 128
SUBLANES = 8
MXU_COLS = 256
VMEM_LIMIT = 59 * 1024 * 1024
EXP_UNDERFLOW = -104.0
LOG2_E = math.log2(math.e)
SB_TILE = 256
SB_SUBTILES = 8
DL_TILE = 128
DL_CHUNK = DL_TILE * max(d for _, d in DIL_PATTERNS)
MLA_TILE = 512
MLA_ROW_BLOCK = 64
MM_TM = 512
MM_TN = 1024


def _cparams(sem):
    return pltpu.CompilerParams(dimension_semantics=sem, vmem_limit_bytes=VMEM_LIMIT)


def _tile(n, want):
    if n <= want:
        return n
    t = (want // LANES) * LANES
    while n % t:
        t -= LANES
    return t


def _mod_kernel(c_ref, w_ref, b_ref, o_ref):
    a = c_ref[...]
    a = (a * jax.nn.sigmoid(a)).astype(BF16)
    o_ref[...] = jnp.dot(a, w_ref[...].astype(BF16), preferred_element_type=F32) + b_ref[...]


def _modulation(c, ada_w, ada_b):
    depth, d, n3 = ada_w.shape
    b = c.shape[0]
    rows = -(-b // SUBLANES) * SUBLANES
    c_pad = jnp.pad(c, ((0, rows - b), (0, 0)))
    tn = _tile(n3, 1024)
    out = pl.pallas_call(
        _mod_kernel,
        grid=(depth, n3 // tn),
        in_specs=[pl.BlockSpec((rows, d), lambda l, j: (0, 0)),
                  pl.BlockSpec((None, d, tn), lambda l, j: (l, 0, j)),
                  pl.BlockSpec((None, 1, tn), lambda l, j: (l, 0, j))],
        out_specs=pl.BlockSpec((None, rows, tn), lambda l, j: (l, 0, j)),
        out_shape=jax.ShapeDtypeStruct((depth, rows, n3), F32),
        compiler_params=_cparams(("parallel", "parallel")),
        name="modulation",
    )(c_pad, ada_w, ada_b.reshape(depth, 1, n3))
    return out[:, :b]


def _norm_mod_kernel(x_ref, w_ref, scale_ref, shift_ref, o_ref):
    x = x_ref[...]
    y = x * lax.rsqrt(jnp.mean(x * x, axis=-1, keepdims=True) + EPS)
    y = y * w_ref[...]
    o_ref[...] = (y * (1.0 + scale_ref[...]) + shift_ref[...]).astype(o_ref.dtype)


def _norm_mod(x, w, scale, shift):
    b, s, d = x.shape
    ts = _tile(s, 512)
    return pl.pallas_call(
        _norm_mod_kernel,
        grid=(b, s // ts),
        in_specs=[pl.BlockSpec((None, ts, d), lambda bi, i: (bi, i, 0)),
                  pl.BlockSpec((1, d), lambda bi, i: (0, 0)),
                  pl.BlockSpec((None, 1, d), lambda bi, i: (bi, 0, 0)),
                  pl.BlockSpec((None, 1, d), lambda bi, i: (bi, 0, 0))],
        out_specs=pl.BlockSpec((None, ts, d), lambda bi, i: (bi, i, 0)),
        out_shape=jax.ShapeDtypeStruct((b, s, d), BF16),
        compiler_params=_cparams(("parallel", "parallel")),
        name="norm_mod",
    )(x, w.reshape(1, d), scale, shift)


def _run_epilogue(variants, chunk_of, tn, acc_ref):
    chunks = [(c0, min(MXU_COLS, tn - c0)) for c0 in range(0, tn, MXU_COLS)]

    def fused(fn):
        for c0, w in chunks:
            fn(chunk_of(c0, w), c0)

    def matmul_only():
        acc_ref[...] = chunk_of(0, tn)

    def epilogue_only(fn):
        for c0, w in chunks:
            fn(acc_ref.at[:, c0:c0 + w], c0)

    split = [(cond, fn) for cond, fn in variants if getattr(fn, "via_ref", False) and cond is not None]
    if split:
        pl.when(functools.reduce(jnp.logical_or, [cond for cond, _ in split]))(matmul_only)
    for cond, fn in variants:
        if getattr(fn, "via_ref", False):
            if cond is None:
                matmul_only()
                epilogue_only(fn)
            else:
                pl.when(cond)(functools.partial(epilogue_only, fn))
        elif cond is None:
            fused(fn)
        else:
            pl.when(cond)(functools.partial(fused, fn))


def _via_ref(fn):
    fn.via_ref = True
    return fn


def _mm_kernel(*refs, n_extra, n_out, nk, tn, epilogue, prologue):
    x_ref, w_ref = refs[0], refs[1]
    extras = refs[2:2 + n_extra]
    outs = refs[2 + n_extra:2 + n_extra + n_out]
    scratch = refs[2 + n_extra + n_out:]
    acc_ref = scratch[0]
    if prologue is not None:
        h_ref = scratch[1]

        @pl.when(pl.program_id(1) == 0)
        def _():
            h_ref[...] = prologue(x_ref, extras)

        _run_epilogue(epilogue(extras, outs),
                      lambda c0, w: jnp.dot(h_ref[...], w_ref[:, c0:c0 + w], preferred_element_type=F32),
                      tn, acc_ref)
        return
    k = pl.program_id(2)

    @pl.when(k == 0)
    def _():
        acc_ref[...] = jnp.zeros_like(acc_ref)

    acc_ref[...] += jnp.dot(x_ref[...], w_ref[...], preferred_element_type=F32)

    @pl.when(k == nk - 1)
    def _():
        _run_epilogue(epilogue(extras, outs), lambda c0, w: acc_ref[:, c0:c0 + w], tn, acc_ref)


def _matmul(x, w, *, tm, tn, tk, extras, extra_specs, out_shapes, out_specs, epilogue,
            prologue=None, x_col_block=0, name):
    m = x.shape[0]
    kdim, n = w.shape
    nk = kdim // tk
    assert m % tm == 0 and n % tn == 0 and kdim % tk == 0 and x.shape[1] >= (x_col_block + nk) * tk
    scratch = [pltpu.VMEM((tm, tn), F32)]
    if prologue is not None:
        assert nk == 1
        scratch.append(pltpu.VMEM((tm, kdim), BF16))
    kern = functools.partial(_mm_kernel, n_extra=len(extras), n_out=len(out_shapes), nk=nk, tn=tn,
                             epilogue=epilogue, prologue=prologue)
    return pl.pallas_call(
        kern,
        grid=(m // tm, n // tn, nk),
        in_specs=[pl.BlockSpec((tm, tk), lambda i, j, k: (i, x_col_block + k)),
                  pl.BlockSpec((tk, tn), lambda i, j, k: (k, j))] + list(extra_specs),
        out_specs=list(out_specs),
        out_shape=list(out_shapes),
        scratch_shapes=scratch,
        compiler_params=_cparams(("parallel", "arbitrary", "arbitrary")),
        name=name,
    )(x, w, *extras)


def _mm_ws_kernel(*refs, k_parts, n_extra, n_out, tn, cast, epilogue):
    n_x = len(k_parts)
    x_refs = refs[:n_x]
    w_ref = refs[n_x]
    extras = refs[n_x + 1:n_x + 1 + n_extra]
    outs = refs[n_x + 1 + n_extra:n_x + 1 + n_extra + n_out]
    scratch = refs[n_x + 1 + n_extra + n_out:]
    acc_ref = scratch[0]
    if cast:
        wb_ref = scratch[1]

        @pl.when(pl.program_id(1) == 0)
        def _():
            wb_ref[...] = w_ref[...].astype(BF16)
    else:
        wb_ref = w_ref

    def chunk_of(c0, w):
        acc = None
        k0 = 0
        for x_ref, kp in zip(x_refs, k_parts):
            part = jnp.dot(x_ref[...], wb_ref[k0:k0 + kp, c0:c0 + w], preferred_element_type=F32)
            acc = part if acc is None else acc + part
            k0 += kp
        return acc

    _run_epilogue(epilogue(extras, outs), chunk_of, tn, acc_ref)


def _matmul_ws(xs, w, *, tm, tn, extras, extra_specs, out_shapes, out_specs, epilogue, name, layer=None):
    m = xs[0].shape[0]
    kdim, n = w.shape[-2:]
    if layer is None:
        w_spec = pl.BlockSpec((kdim, tn), lambda j, i: (0, j))
    else:
        w_spec = pl.BlockSpec((None, kdim, tn), lambda j, i: (layer, 0, j))
    k_parts = tuple(x.shape[1] for x in xs)
    assert m % tm == 0 and n % tn == 0 and sum(k_parts) == kdim
    cast = w.dtype != BF16
    scratch = [pltpu.VMEM((tm, tn), F32)]
    if cast:
        scratch.append(pltpu.VMEM((kdim, tn), BF16))
    kern = functools.partial(_mm_ws_kernel, k_parts=k_parts, n_extra=len(extras),
                             n_out=len(out_shapes), tn=tn, cast=cast, epilogue=epilogue)
    x_specs = [pl.BlockSpec((tm, kp), lambda j, i: (i, 0)) for kp in k_parts]
    return pl.pallas_call(
        kern,
        grid=(n // tn, m // tm),
        in_specs=x_specs + [w_spec] + list(extra_specs),
        out_specs=list(out_specs),
        out_shape=list(out_shapes),
        scratch_shapes=scratch,
        compiler_params=_cparams(("parallel", "arbitrary")),
        name=name,
    )(*xs, w, *extras)


def _epi_plain(extras, outs):
    o_ref = outs[0]

    def fn(a, c0):
        o_ref[:, c0:c0 + a.shape[1]] = a.astype(o_ref.dtype)

    return [(None, fn)]


def _epi_resid(extras, outs):
    xres_ref, gate_ref = extras
    o_ref = outs[0]

    def fn(a, c0):
        sl = slice(c0, c0 + a.shape[1])
        o_ref[:, sl] = xres_ref[:, sl] + gate_ref[:, sl] * a

    return [(None, fn)]


def _rope_full(y, cos_ref, sin_ref):
    return y * cos_ref[...] + pltpu.roll(y, HEAD_DIM // 2, 1) * sin_ref[...]


def _rope_mla(r, c_ref, s_ref):
    return r * c_ref[...] + pltpu.roll(r, QK_ROPE // 2, 1) * s_ref[...]


def _epi_even(extras, outs, *, part_tiles, q_scale):
    qn_ref, kn_ref, cos_ref, sin_ref = extras
    o_ref = outs[0]
    part = pl.program_id(0) // part_tiles

    def scaled(scale):
        def fn(a, c0):
            y = a if scale == 1.0 else a * scale
            o_ref[:, c0:c0 + a.shape[1]] = y.astype(o_ref.dtype)
        return fn

    def headnorm_rope(w_ref, scale):
        @_via_ref
        def fn(a, c0):
            for hh in range(a.shape[1] // HEAD_DIM):
                ah = a[:, hh * HEAD_DIM:(hh + 1) * HEAD_DIM]
                y = ah * lax.rsqrt(jnp.mean(ah * ah, axis=-1, keepdims=True) + EPS) * w_ref[...]
                y = _rope_full(y, cos_ref, sin_ref)
                if scale != 1.0:
                    y = y * scale
                o_ref[:, c0 + hh * HEAD_DIM:c0 + (hh + 1) * HEAD_DIM] = y.astype(o_ref.dtype)
        return fn

    return [(part == 0, scaled(q_scale)),
            (part == 4, headnorm_rope(qn_ref, q_scale * LOG2_E)),
            (part == 5, headnorm_rope(kn_ref, 1.0)),
            ((part != 0) & (part != 4) & (part != 5), scaled(1.0))]


def _pro_rms(x_ref, extras):
    x = x_ref[...]
    w_ref = extras[0]
    y = x * lax.rsqrt(jnp.mean(x * x, axis=-1, keepdims=True) + EPS) * w_ref[...]
    return y.astype(BF16)


def _epi_mla_q(extras, outs):
    _, qw_ref, first_ref, c_ref, s_ref = extras
    o_ref = outs[0]
    assert MXU_COLS == MLA_QK_PAD

    @_via_ref
    def fn(a, c0):
        nope, rope = a[:, :QK_NOPE], a[:, QK_NOPE:]
        sq = nope * nope + rope * rope * first_ref[...]
        rstd = lax.rsqrt(jnp.sum(sq, axis=-1, keepdims=True) * (1.0 / QK_HEAD) + EPS)
        o_ref[:, c0:c0 + QK_NOPE] = (nope * rstd * qw_ref[:, :QK_NOPE]).astype(o_ref.dtype)
        r = _rope_mla(rope * rstd * qw_ref[:, QK_NOPE:], c_ref, s_ref)
        o_ref[:, c0 + QK_NOPE:c0 + MLA_QK_PAD] = r.astype(o_ref.dtype)

    return [(None, fn)]


def _epi_mla_kv(extras, outs):
    _, kpe_ref, kwn_ref, kwp_ref, first_ref, c_ref, s_ref = extras
    k_ref, v_ref = outs
    assert MXU_COLS == QK_NOPE + V_HEAD
    kpe = kpe_ref[...]
    ss_pe = jnp.sum(kpe * kpe * first_ref[...], axis=-1, keepdims=True)
    kr = _rope_mla(kpe * kwp_ref[...], c_ref, s_ref)

    @_via_ref
    def fn(a, c0):
        head = c0 // (QK_NOPE + V_HEAD)
        kn = a[:, :QK_NOPE]
        rstd = lax.rsqrt((jnp.sum(kn * kn, axis=-1, keepdims=True) + ss_pe) * (1.0 / QK_HEAD) + EPS)
        kb = head * MLA_QK_PAD
        k_ref[:, kb:kb + QK_NOPE] = (kn * rstd * kwn_ref[...]).astype(k_ref.dtype)
        k_ref[:, kb + QK_NOPE:kb + MLA_QK_PAD] = (kr * rstd).astype(k_ref.dtype)
        v_ref[:, head * V_HEAD:(head + 1) * V_HEAD] = a[:, QK_NOPE:].astype(v_ref.dtype)

    return [(None, fn)]


def _qk(q, k):
    return lax.dot_general(q, k, (((1,), (1,)), ((), ())), preferred_element_type=F32)


def _sb_kernel(q_ref, k_ref, v_ref, g_ref, u_ref, o_ref, acc_ref, carry_ref, *, tile):
    qi = pl.program_id(2)
    n_sub = q_ref.shape[0] // tile

    def tile_rows(kb):
        return pl.ds(pl.multiple_of(kb * tile, tile), tile)

    def weights(q, kb, carry, diagonal=False, valid=None):
        z = _qk(q, k_ref[tile_rows(kb), :])
        sp = jnp.log(1.0 + jnp.exp(-jnp.abs(z)))
        log_beta = jnp.minimum(z, 0.0) - sp
        log_keep = jnp.minimum(-z, 0.0) - sp
        keep = valid
        if diagonal:
            row = lax.broadcasted_iota(jnp.int32, z.shape, 0)
            col = lax.broadcasted_iota(jnp.int32, z.shape, 1)
            keep = col < row
        if keep is not None:
            log_keep = jnp.where(keep, log_keep, 0.0)
        hi = log_keep.astype(BF16)
        lo = (log_keep - hi.astype(F32)).astype(BF16)
        u = u_ref[...]
        excl = jnp.dot(hi, u, preferred_element_type=F32) + jnp.dot(lo, u, preferred_element_type=F32)
        if carry is not None:
            excl = excl + carry
        w = jnp.exp(log_beta + excl)
        if keep is not None:
            w = jnp.where(keep, w, 0.0)
        return w.astype(BF16), jnp.sum(log_keep, axis=-1, keepdims=True)

    def pv(w, kb):
        return jnp.dot(w, v_ref[tile_rows(kb), :], preferred_element_type=F32)

    tails = []
    for sub in range(n_sub):
        rows = slice(sub * tile, (sub + 1) * tile)
        q = q_ref[rows, :]
        diag = qi * n_sub + sub
        prev = jnp.maximum(diag - 1, 0)
        w_d, sum_d = weights(q, diag, None, diagonal=True)
        w_p, sum_p = weights(q, prev, sum_d, valid=diag > 0)
        carry = sum_d + sum_p
        acc_ref[rows, :] = pv(w_d, diag) + pv(w_p, prev)
        carry_ref[rows, :] = carry
        tails.append((rows, diag - 2, jnp.max(carry)))

    for rows, kb0, cmax0 in tails:
        def cond(state):
            kb, cmax = state
            return (kb >= 0) & (cmax > EXP_UNDERFLOW)

        def body(state, rows=rows):
            kb, _ = state
            carry = carry_ref[rows, :]
            w, sum_k = weights(q_ref[rows, :], kb, carry)
            acc_ref[rows, :] += pv(w, kb)
            carry = carry + sum_k
            carry_ref[rows, :] = carry
            return kb - 1, jnp.max(carry)

        lax.while_loop(cond, body, (kb0, cmax0))

    g = g_ref[...].astype(F32)
    o_ref[...] = (acc_ref[...] * (g * jax.nn.sigmoid(g))).astype(o_ref.dtype)


def _dl_kernel(q_ref, k_ref, v_ref, g_ref, bias_ref, o_ref, qf_ref, kf_ref, vf_ref, acc_ref, m_ref, l_ref,
               *, chunk):
    ci = pl.program_id(2)

    @pl.when(ci == 0)
    def _():
        kf_ref[...] = k_ref[...].astype(F32)
        vf_ref[...] = v_ref[...].astype(F32)

    qf_ref[...] = q_ref[...].astype(F32)
    base = ci * chunk

    def body(idx, _):
        for pi, (_, dil) in enumerate(DIL_PATTERNS):
            unit = DL_TILE * dil
            q0 = (idx // dil) * unit + idx % dil
            rows = pl.ds(q0, DL_TILE, stride=dil)
            cur0 = base + q0
            has_prev = cur0 >= unit
            prev0 = jnp.maximum(cur0 - unit, 0)
            prev_rows = pl.ds(prev0, DL_TILE, stride=dil)
            cur_rows = pl.ds(cur0, DL_TILE, stride=dil)
            kk = jnp.concatenate([kf_ref[prev_rows, :], kf_ref[cur_rows, :]], axis=0).astype(BF16)
            vv = jnp.concatenate([vf_ref[prev_rows, :], vf_ref[cur_rows, :]], axis=0).astype(BF16)
            s = _qk(qf_ref[rows, :].astype(BF16), kk) + bias_ref[pi, has_prev.astype(jnp.int32)]
            m = jnp.max(s, axis=-1, keepdims=True)
            p = jnp.exp2(s - m)
            l = jnp.sum(p, axis=-1, keepdims=True)
            acc_ref[pi, rows, :] = jnp.dot(p.astype(BF16), vv, preferred_element_type=F32)
            m_ref[pi, rows, :] = jnp.broadcast_to(m, (DL_TILE, LANES))
            l_ref[pi, rows, :] = jnp.broadcast_to(l, (DL_TILE, LANES))
        return 0

    lax.fori_loop(0, chunk // DL_TILE, body, 0, unroll=True)

    n_pat = len(DIL_PATTERNS)
    m_all = functools.reduce(jnp.maximum, [m_ref[pi] for pi in range(n_pat)])
    num = jnp.zeros(acc_ref.shape[1:], F32)
    den = jnp.zeros(acc_ref.shape[1:], F32)
    for pi in range(n_pat):
        w = jnp.exp2(m_ref[pi] - m_all)
        num = num + w * acc_ref[pi]
        den = den + w * l_ref[pi]
    g = g_ref[...].astype(F32)
    o_ref[...] = (num / den * (g * jax.nn.sigmoid(g))).astype(o_ref.dtype)


def _softmax_update(s_ref, bias_ref, v, p_ref, acc_ref, m_ref, l_ref):
    tq, tk = s_ref.shape
    for r0 in range(0, tq, MLA_ROW_BLOCK):
        rs = slice(r0, r0 + MLA_ROW_BLOCK)
        cols = [s_ref[rs, c * LANES:(c + 1) * LANES] for c in range(tk // LANES)]
        if bias_ref is not None:
            cols = [col + bias_ref[rs, c * LANES:(c + 1) * LANES] for c, col in enumerate(cols)]
        m_old = m_ref[rs, :]
        tile_max = jnp.max(functools.reduce(jnp.maximum, cols), axis=-1, keepdims=True)
        m_new = jnp.maximum(m_old, tile_max)
        ps = [jnp.exp2(col - m_new) for col in cols]
        alpha = jnp.exp2(m_old - m_new)
        l_ref[rs, :] = alpha * l_ref[rs, :] + jnp.sum(functools.reduce(jnp.add, ps), axis=-1, keepdims=True)
        m_ref[rs, :] = m_new
        acc_ref[rs, :] = alpha * acc_ref[rs, :]
        for c, x in enumerate(ps):
            p_ref[rs, c * LANES:(c + 1) * LANES] = x.astype(BF16)
    acc_ref[...] += jnp.dot(p_ref[...], v, preferred_element_type=F32)


def _mla_kernel(q_ref, k_ref, v_ref, g_ref, bias_ref, o_ref, sa_ref, sb_ref, pa_ref, pb_ref, acc_ref, m_ref,
                l_ref, *, tile):
    n_q = q_ref.shape[0] // tile
    acc_ref[...] = jnp.zeros_like(acc_ref)
    m_ref[...] = jnp.full_like(m_ref, -jnp.inf)
    l_ref[...] = jnp.zeros_like(l_ref)

    def rows(t):
        return pl.ds(pl.multiple_of(t * tile, tile), tile)

    def scores(qi, kb, dst_ref):
        dst_ref[...] = _qk(q_ref[rows(qi), :], k_ref[rows(kb), :])

    def pipeline(n_steps, start, advance, bias):
        def update(qi, kb, src_ref, p_ref):
            r = rows(qi)
            _softmax_update(src_ref, bias, v_ref[rows(kb), :], p_ref, acc_ref.at[r, :], m_ref.at[r, :],
                            l_ref.at[r, :])

        scores(*start, sa_ref)

        def pair(_, state):
            qa, ka = state
            qb, kb = advance(qa, ka)
            scores(qb, kb, sb_ref)
            update(qa, ka, sa_ref, pa_ref)
            qa, ka = advance(qb, kb)
            scores(qa, ka, sa_ref)
            update(qb, kb, sb_ref, pb_ref)
            return qa, ka

        qa, ka = lax.fori_loop(0, n_steps // 2, pair, start, unroll=4)
        if n_steps % 2:
            update(qa, ka, sa_ref, pa_ref)

    def next_lower(qi, kb):
        wrap = kb == qi - 1
        return jnp.where(wrap, jnp.minimum(qi + 1, n_q - 1), qi), jnp.where(wrap, 0, kb + 1)

    def next_diagonal(qi, kb):
        t = jnp.minimum(qi + 1, n_q - 1)
        return t, t

    zero, one = jnp.int32(0), jnp.int32(1)
    if n_q > 1:
        pipeline(n_q * (n_q - 1) // 2, (one, zero), next_lower, None)
    pipeline(n_q, (zero, zero), next_diagonal, bias_ref)

    g = g_ref[...].astype(F32)
    o_ref[...] = (acc_ref[...] * g / (l_ref[...] * (1.0 + jnp.exp(-g)))).astype(o_ref.dtype)


def _attn_call(kern, arrays, col_blocks, widths, consts, out_width_total, out_col0, n_heads, tile,
               scratch, name):
    q, k, v, g = arrays
    b, s, _ = q.shape
    cq, ck, cv, cg = col_blocks
    wq, wk, wv, wg = widths
    in_specs = [
        pl.BlockSpec((None, tile, wq), lambda bi, h, i: (bi, i, cq + h)),
        pl.BlockSpec((None, s, wk), lambda bi, h, i: (bi, 0, ck + h)),
        pl.BlockSpec((None, s, wv), lambda bi, h, i: (bi, 0, cv + h)),
        pl.BlockSpec((None, tile, wg), lambda bi, h, i: (bi, i, cg + h)),
    ]
    for cst in consts:
        in_specs.append(pl.BlockSpec(cst.shape, lambda bi, h, i, nd=cst.ndim: (0,) * nd))
    return pl.pallas_call(
        kern,
        grid=(b, n_heads, s // tile),
        in_specs=in_specs,
        out_specs=pl.BlockSpec((None, tile, wv), lambda bi, h, i: (bi, i, out_col0 + h)),
        out_shape=jax.ShapeDtypeStruct((b, s, out_width_total), BF16),
        scratch_shapes=scratch,
        compiler_params=_cparams(("parallel", "parallel", "arbitrary")),
        name=name,
    )(q, k, v, g, *consts)


def _dl_band_bias():
    i = np.arange(DL_TILE)[:, None]
    c = np.arange(2 * DL_TILE)[None, :]
    out = []
    for window, dil in DIL_PATTERNS:
        span = window // dil
        assert span <= DL_TILE
        dist = i + DL_TILE - c
        band = (dist >= 0) & (dist <= span)
        out.append(np.stack([band & (c >= DL_TILE), band]))
    return np.where(np.stack(out), 0.0, -np.inf).astype(np.float32)


def _rope_angles(positions, dim):
    inv = ROPE_THETA ** (-jnp.arange(0, dim, 2, dtype=F32) / dim)
    ang = positions.astype(F32)[..., None] * inv
    return jnp.cos(ang), jnp.sin(ang)


def _even_layer(x, h, gate, li, w_in, qn, kn, w_out, cosf, sinf, sb_u, dl_bias):
    b, s, d = x.shape
    m = b * s
    n_in = w_in.shape[-1]
    w_part = n_in // 8
    n_heads = w_part // HEAD_DIM
    tm, tn = _tile(s, MM_TM), _tile(w_part, MM_TN)
    q_scale = 1.0 / math.sqrt(HEAD_DIM)
    proj = _matmul_ws(
        [h.reshape(m, d)], w_in, layer=li, tm=tm, tn=tn,
        extras=(qn.reshape(1, HEAD_DIM), kn.reshape(1, HEAD_DIM), cosf, sinf),
        extra_specs=[pl.BlockSpec((1, HEAD_DIM), lambda j, i: (0, 0)),
                     pl.BlockSpec((1, HEAD_DIM), lambda j, i: (0, 0)),
                     pl.BlockSpec((tm, HEAD_DIM), lambda j, i: (i, 0)),
                     pl.BlockSpec((tm, HEAD_DIM), lambda j, i: (i, 0))],
        out_shapes=[jax.ShapeDtypeStruct((m, n_in), BF16)],
        out_specs=[pl.BlockSpec((tm, tn), lambda j, i: (i, j))],
        epilogue=functools.partial(_epi_even, part_tiles=w_part // tn, q_scale=q_scale),
        name="even_in_proj")[0].reshape(b, s, n_in)

    hb = n_heads
    w4 = (HEAD_DIM,) * 4
    sb_rows = SB_TILE * SB_SUBTILES
    o_sb = _attn_call(
        functools.partial(_sb_kernel, tile=SB_TILE), (proj,) * 4, (0, hb, 2 * hb, 3 * hb), w4,
        (sb_u,), w_part, 0, n_heads, sb_rows,
        [pltpu.VMEM((sb_rows, HEAD_DIM), F32), pltpu.VMEM((sb_rows, 1), F32)], "stick_breaking")
    assert s % DL_CHUNK == 0
    n_pat = len(DIL_PATTERNS)
    o_dl = _attn_call(
        functools.partial(_dl_kernel, chunk=DL_CHUNK), (proj,) * 4,
        (4 * hb, 5 * hb, 6 * hb, 7 * hb), w4, (dl_bias,), w_part, 0, n_heads, DL_CHUNK,
        [pltpu.VMEM((DL_CHUNK, HEAD_DIM), F32), pltpu.VMEM((s, HEAD_DIM), F32),
         pltpu.VMEM((s, HEAD_DIM), F32), pltpu.VMEM((n_pat, DL_CHUNK, HEAD_DIM), F32),
         pltpu.VMEM((n_pat, DL_CHUNK, LANES), F32), pltpu.VMEM((n_pat, DL_CHUNK, LANES), F32)],
        "dilated_window")
    return _out_proj(x, [o_sb.reshape(m, w_part), o_dl.reshape(m, w_part)], w_out, li, gate)


def _out_proj(x, mixed_parts, w_out, li, gate):
    b, s, d = x.shape
    m = b * s
    tm, tn = _tile(s, MM_TM), _tile(d, MM_TN)
    rows_per_batch = s // tm
    return _matmul_ws(
        mixed_parts, w_out, layer=li, tm=tm, tn=tn,
        extras=(x.reshape(m, d), gate),
        extra_specs=[pl.BlockSpec((tm, tn), lambda j, i: (i, j)),
                     pl.BlockSpec((None, 1, tn), lambda j, i: (i // rows_per_batch, 0, j))],
        out_shapes=[jax.ShapeDtypeStruct((m, d), F32)],
        out_specs=[pl.BlockSpec((tm, tn), lambda j, i: (i, j))],
        epilogue=_epi_resid, name="out_proj")[0].reshape(b, s, d)


def _odd_layer(x, h, gate, li, w_in, q_lat_norm, kv_lat_norm, w_uq, w_ukv, q_norm, k_norm, w_out,
               rc, rs, mla_bias):
    b, s, d = x.shape
    m = b * s
    q_lora = q_lat_norm.shape[0]
    kv_lora = kv_lat_norm.shape[0]
    n_heads = w_uq.shape[1] // QK_HEAD
    lat_raw = q_lora + kv_lora + QK_ROPE
    n_lat = lat_raw + QK_ROPE
    assert n_lat % LANES == 0 and 2 * QK_ROPE == LANES
    h2 = h.reshape(m, d)
    tm, tk = _tile(s, 1024), _tile(d, 1024)
    first = (jnp.arange(LANES) < QK_ROPE).astype(F32).reshape(1, LANES)

    w_lat = jnp.concatenate([w_in[:, :lat_raw], w_in[:, lat_raw - QK_ROPE:lat_raw]], axis=1).astype(BF16)
    tm_lat = _tile(s, 512)
    lat = _matmul(h2, w_lat, tm=tm_lat, tn=n_lat, tk=tk, extras=(), extra_specs=[],
                  out_shapes=[jax.ShapeDtypeStruct((m, n_lat), F32)],
                  out_specs=[pl.BlockSpec((tm_lat, n_lat), lambda i, j, k: (i, j))],
                  epilogue=_epi_plain, name="odd_latent_proj")[0]
    n_g = w_in.shape[1] - lat_raw
    tn_g = _tile(n_g, MM_TN)
    g = _matmul_ws([h2], w_in[:, lat_raw:].astype(BF16), tm=tm, tn=tn_g, extras=(), extra_specs=[],
                   out_shapes=[jax.ShapeDtypeStruct((m, n_g), BF16)],
                   out_specs=[pl.BlockSpec((tm, tn_g), lambda j, i: (i, j))],
                   epilogue=_epi_plain, name="odd_gate_proj")[0]

    w_uq3 = w_uq.reshape(q_lora, n_heads, QK_HEAD)
    w_uq_p = jnp.concatenate([w_uq3, w_uq3[:, :, QK_NOPE:]], axis=-1)
    w_uq_p = w_uq_p.reshape(q_lora, n_heads * MLA_QK_PAD).astype(BF16)
    qw = (jnp.concatenate([q_norm, q_norm[QK_NOPE:]]) * (LOG2_E / math.sqrt(QK_HEAD))).reshape(1, MLA_QK_PAD)
    tn_q = _tile(n_heads * MLA_QK_PAD, 1024)
    rope_specs = [pl.BlockSpec((1, LANES), lambda i, j, k: (0, 0))] + \
        [pl.BlockSpec((tm, LANES), lambda i, j, k: (i, 0))] * 2
    q = _matmul(
        lat, w_uq_p, tm=tm, tn=tn_q, tk=q_lora,
        extras=(q_lat_norm.reshape(1, q_lora), qw, first, rc, rs),
        extra_specs=[pl.BlockSpec((1, q_lora), lambda i, j, k: (0, 0)),
                     pl.BlockSpec((1, MLA_QK_PAD), lambda i, j, k: (0, 0))] + rope_specs,
        out_shapes=[jax.ShapeDtypeStruct((m, n_heads * MLA_QK_PAD), BF16)],
        out_specs=[pl.BlockSpec((tm, tn_q), lambda i, j, k: (i, j))],
        epilogue=_epi_mla_q,
        prologue=_pro_rms, name="mla_q_proj")[0]

    assert q_lora % kv_lora == 0 and (q_lora + kv_lora) % LANES == 0
    kvw = QK_NOPE + V_HEAD
    tn_kv = _tile(n_heads * kvw, 1024)
    heads_per_tile = tn_kv // kvw
    kwn = k_norm[:QK_NOPE].reshape(1, QK_NOPE)
    kwp = jnp.concatenate([k_norm[QK_NOPE:], k_norm[QK_NOPE:]]).reshape(1, LANES)
    kpe_block = (q_lora + kv_lora) // LANES
    k, v = _matmul(
        lat, w_ukv.astype(BF16), tm=tm, tn=tn_kv, tk=kv_lora,
        extras=(kv_lat_norm.reshape(1, kv_lora), lat, kwn, kwp, first, rc, rs),
        extra_specs=[pl.BlockSpec((1, kv_lora), lambda i, j, k: (0, 0)),
                     pl.BlockSpec((tm, LANES), lambda i, j, k: (i, kpe_block)),
                     pl.BlockSpec((1, QK_NOPE), lambda i, j, k: (0, 0)),
                     pl.BlockSpec((1, LANES), lambda i, j, k: (0, 0))] + rope_specs,
        out_shapes=[jax.ShapeDtypeStruct((m, n_heads * MLA_QK_PAD), BF16),
                    jax.ShapeDtypeStruct((m, n_heads * V_HEAD), BF16)],
        out_specs=[pl.BlockSpec((tm, heads_per_tile * MLA_QK_PAD), lambda i, j, k: (i, j)),
                   pl.BlockSpec((tm, heads_per_tile * V_HEAD), lambda i, j, k: (i, j))],
        epilogue=_epi_mla_kv,
        prologue=_pro_rms, x_col_block=q_lora // kv_lora, name="mla_kv_proj")
    mixed = _attn_call(
        functools.partial(_mla_kernel, tile=MLA_TILE),
        (q.reshape(b, s, -1), k.reshape(b, s, -1), v.reshape(b, s, -1), g.reshape(b, s, -1)),
        (0, 0, 0, 0), (MLA_QK_PAD, MLA_QK_PAD, V_HEAD, V_HEAD), (mla_bias,), n_heads * V_HEAD, 0,
        n_heads, s,
        [pltpu.VMEM((MLA_TILE, MLA_TILE), F32), pltpu.VMEM((MLA_TILE, MLA_TILE), F32),
         pltpu.VMEM((MLA_TILE, MLA_TILE), BF16), pltpu.VMEM((MLA_TILE, MLA_TILE), BF16),
         pltpu.VMEM((s, V_HEAD), F32), pltpu.VMEM((s, LANES), F32),
         pltpu.VMEM((s, LANES), F32)], "mla_attention")
    return _out_proj(x, [mixed.reshape(m, -1)], w_out, li, gate)


def kernel(x, c, positions, ada_w, ada_b, norm_w, ev_w_in, ev_q_norm, ev_k_norm, ev_w_out, od_w_in,
           od_q_lat_norm, od_kv_lat_norm, od_w_uq, od_w_ukv, od_q_norm, od_k_norm, od_w_out):
    b, s, d = x.shape
    depth = ada_w.shape[0]
    m = b * s

    cos_f, sin_f = _rope_angles(positions, HEAD_DIM)
    cosf = jnp.concatenate([cos_f, cos_f], axis=-1).reshape(m, HEAD_DIM)
    sinf = jnp.concatenate([-sin_f, sin_f], axis=-1).reshape(m, HEAD_DIM)
    cos_m, sin_m = _rope_angles(positions, QK_ROPE)
    z32 = jnp.zeros_like(cos_m)
    z64 = jnp.concatenate([z32, z32], axis=-1)
    rc = jnp.concatenate([cos_m, cos_m, z64], axis=-1).reshape(m, LANES)
    rs = jnp.concatenate([-sin_m, sin_m, z64], axis=-1).reshape(m, LANES)

    sb_u = jnp.asarray(np.tril(np.ones((SB_TILE, SB_TILE), np.float32), -1), BF16)
    dl_bias = jnp.asarray(_dl_band_bias())
    causal = np.tril(np.ones((MLA_TILE, MLA_TILE), bool))
    mla_bias = jnp.asarray(np.where(causal, 0.0, -np.inf).astype(np.float32))

    mod = _modulation(c, ada_w, ada_b)
    shift, scale, gate = (mod[:, :, i * d:(i + 1) * d].reshape(depth, b, 1, d) for i in range(3))

    for layer in range(depth):
        h = _norm_mod(x, norm_w[layer], scale[layer], shift[layer])
        i = layer // 2
        if layer % 2 == 0:
            x = _even_layer(x, h, gate[layer], i, ev_w_in, ev_q_norm[i], ev_k_norm[i], ev_w_out,
                            cosf, sinf, sb_u, dl_bias)
        else:
            x = _odd_layer(x, h, gate[layer], i, od_w_in[i], od_q_lat_norm[i], od_kv_lat_norm[i],
                           od_w_uq[i], od_w_ukv[i], od_q_norm[i], od_k_norm[i], od_w_out,
                           rc, rs, mla_bias)
    return x
```

```python
import functools
import math

import numpy as np
import jax
import jax.numpy as jnp
from jax import lax
from jax.experimental import pallas as pl
from jax.experimental.pallas import tpu as pltpu

F32 = jnp.float32
BF16 = jnp.bfloat16

HEAD_DIM = 128
DIL_PATTERNS = ((128, 1), (512, 4), (2048, 16))
QK_NOPE = 128
QK_ROPE = 64
QK_HEAD = QK_NOPE + QK_ROPE
V_HEAD = 128
MLA_QK_PAD = 256
ROPE_THETA = 10000.0
EPS = 1e-6

LANES = 128
SUBLANES = 8
MXU_COLS = 256
VMEM_LIMIT = 59 * 1024 * 1024
EXP_UNDERFLOW = -104.0
LOG2_E = math.log2(math.e)
SB_TILE = 256
SB_SUBTILES = 8
DL_TILE = 128
DL_CHUNK = DL_TILE * max(d for _, d in DIL_PATTERNS)
MLA_TILE = 512
MLA_ROW_BLOCK = 64
MM_TM = 512
MM_TN = 1024
XS_TILE = 1024
LATENT_TM = 512
MOD_TN = 1024
NORM_ROWS = 512
NORM_ROW_BLOCK = 16


def _cparams(sem):
    return pltpu.CompilerParams(dimension_semantics=sem, vmem_limit_bytes=VMEM_LIMIT)


def _tile(n, want):
    if n <= want:
        return n
    t = (want // LANES) * LANES
    while n % t:
        t -= LANES
    return t


def _mod_kernel(c_ref, w_ref, b_ref, o_ref):
    a = c_ref[...]
    a = (a * jax.nn.sigmoid(a)).astype(BF16)
    o_ref[...] = jnp.dot(a, w_ref[...].astype(BF16), preferred_element_type=F32) + b_ref[...]


def _modulation(c, ada_w, ada_b):
    depth, d, n3 = ada_w.shape
    b = c.shape[0]
    rows = -(-b // SUBLANES) * SUBLANES
    c_pad = jnp.pad(c, ((0, rows - b), (0, 0)))
    tn = _tile(n3, MOD_TN)
    out = pl.pallas_call(
        _mod_kernel,
        grid=(depth, n3 // tn),
        in_specs=[pl.BlockSpec((rows, d), lambda l, j: (0, 0)),
                  pl.BlockSpec((None, d, tn), lambda l, j: (l, 0, j)),
                  pl.BlockSpec((None, 1, tn), lambda l, j: (l, 0, j))],
        out_specs=pl.BlockSpec((None, rows, tn), lambda l, j: (l, 0, j)),
        out_shape=jax.ShapeDtypeStruct((depth, rows, n3), F32),
        compiler_params=_cparams(("parallel", "parallel")),
        name="modulation",
    )(c_pad, ada_w, ada_b.reshape(depth, 1, n3))
    return out[:, :b]


def _norm_mod_kernel(x_ref, w_ref, scale_ref, shift_ref, o_ref):
    gain = w_ref[...] * (1.0 + scale_ref[...])

    def body(r, _):
        rows = pl.ds(pl.multiple_of(r * NORM_ROW_BLOCK, NORM_ROW_BLOCK), NORM_ROW_BLOCK)
        x = x_ref[rows, :]
        rstd = lax.rsqrt(jnp.mean(x * x, axis=-1, keepdims=True) + EPS)
        o_ref[rows, :] = (x_ref[rows, :] * rstd * gain + shift_ref[...]).astype(o_ref.dtype)
        return 0

    lax.fori_loop(0, x_ref.shape[0] // NORM_ROW_BLOCK, body, 0, unroll=2)


def _norm_mod(x, w, scale, shift):
    b, s, d = x.shape
    ts = _tile(s, NORM_ROWS)
    return pl.pallas_call(
        _norm_mod_kernel,
        grid=(b, s // ts),
        in_specs=[pl.BlockSpec((None, ts, d), lambda bi, i: (bi, i, 0)),
                  pl.BlockSpec((1, d), lambda bi, i: (0, 0)),
                  pl.BlockSpec((None, 1, d), lambda bi, i: (bi, 0, 0)),
                  pl.BlockSpec((None, 1, d), lambda bi, i: (bi, 0, 0))],
        out_specs=pl.BlockSpec((None, ts, d), lambda bi, i: (bi, i, 0)),
        out_shape=jax.ShapeDtypeStruct((b, s, d), BF16),
        compiler_params=_cparams(("parallel", "parallel")),
        name="norm_mod",
    )(x, w.reshape(1, d), scale, shift)


def _run_epilogue(variants, chunk_of, tn, acc_ref):
    chunks = [(c0, min(MXU_COLS, tn - c0)) for c0 in range(0, tn, MXU_COLS)]

    def fused(fn):
        for c0, w in chunks:
            fn(chunk_of(c0, w), c0)

    def matmul_only():
        acc_ref[...] = chunk_of(0, tn)

    def epilogue_only(fn):
        for c0, w in chunks:
            fn(acc_ref.at[:, c0:c0 + w], c0)

    split = [(cond, fn) for cond, fn in variants if getattr(fn, "via_ref", False) and cond is not None]
    if split:
        pl.when(functools.reduce(jnp.logical_or, [cond for cond, _ in split]))(matmul_only)
    for cond, fn in variants:
        if getattr(fn, "via_ref", False):
            if cond is None:
                matmul_only()
                epilogue_only(fn)
            else:
                pl.when(cond)(functools.partial(epilogue_only, fn))
        elif cond is None:
            fused(fn)
        else:
            pl.when(cond)(functools.partial(fused, fn))


def _via_ref(fn):
    fn.via_ref = True
    return fn


def _mm_kernel(*refs, n_extra, n_out, nk, tn, epilogue, prologue):
    x_ref, w_ref = refs[0], refs[1]
    extras = refs[2:2 + n_extra]
    outs = refs[2 + n_extra:2 + n_extra + n_out]
    scratch = refs[2 + n_extra + n_out:]
    acc_ref = scratch[0]
    if prologue is not None:
        h_ref = scratch[1]

        @pl.when(pl.program_id(1) == 0)
        def _():
            h_ref[...] = prologue(x_ref, extras)

        _run_epilogue(epilogue(extras, outs),
                      lambda c0, w: jnp.dot(h_ref[...], w_ref[:, c0:c0 + w], preferred_element_type=F32),
                      tn, acc_ref)
        return
    k = pl.program_id(2)

    @pl.when(k == 0)
    def _():
        acc_ref[...] = jnp.zeros_like(acc_ref)

    acc_ref[...] += jnp.dot(x_ref[...], w_ref[...], preferred_element_type=F32)

    @pl.when(k == nk - 1)
    def _():
        _run_epilogue(epilogue(extras, outs), lambda c0, w: acc_ref[:, c0:c0 + w], tn, acc_ref)


def _matmul(x, w, *, tm, tn, tk, extras, extra_specs, out_shapes, out_specs, epilogue,
            prologue=None, x_col_block=0, name):
    m = x.shape[0]
    kdim, n = w.shape
    nk = kdim // tk
    assert m % tm == 0 and n % tn == 0 and kdim % tk == 0 and x.shape[1] >= (x_col_block + nk) * tk
    scratch = [pltpu.VMEM((tm, tn), F32)]
    if prologue is not None:
        assert nk == 1
        scratch.append(pltpu.VMEM((tm, kdim), BF16))
    kern = functools.partial(_mm_kernel, n_extra=len(extras), n_out=len(out_shapes), nk=nk, tn=tn,
                             epilogue=epilogue, prologue=prologue)
    return pl.pallas_call(
        kern,
        grid=(m // tm, n // tn, nk),
        in_specs=[pl.BlockSpec((tm, tk), lambda i, j, k: (i, x_col_block + k)),
                  pl.BlockSpec((tk, tn), lambda i, j, k: (k, j))] + list(extra_specs),
        out_specs=list(out_specs),
        out_shape=list(out_shapes),
        scratch_shapes=scratch,
        compiler_params=_cparams(("parallel", "arbitrary", "arbitrary")),
        name=name,
    )(x, w, *extras)


def _mm_ws_kernel(*refs, k_parts, n_extra, n_out, tn, cast, epilogue):
    n_x = len(k_parts)
    x_refs = refs[:n_x]
    w_ref = refs[n_x]
    extras = refs[n_x + 1:n_x + 1 + n_extra]
    outs = refs[n_x + 1 + n_extra:n_x + 1 + n_extra + n_out]
    scratch = refs[n_x + 1 + n_extra + n_out:]
    acc_ref = scratch[0]
    if cast:
        wb_ref = scratch[1]

        @pl.when(pl.program_id(1) == 0)
        def _():
            wb_ref[...] = w_ref[...].astype(BF16)
    else:
        wb_ref = w_ref

    def chunk_of(c0, w):
        acc = None
        k0 = 0
        for x_ref, kp in zip(x_refs, k_parts):
            part = jnp.dot(x_ref[...], wb_ref[k0:k0 + kp, c0:c0 + w], preferred_element_type=F32)
            acc = part if acc is None else acc + part
            k0 += kp
        return acc

    _run_epilogue(epilogue(extras, outs), chunk_of, tn, acc_ref)


def _matmul_ws(xs, w, *, tm, tn, extras, extra_specs, out_shapes, out_specs, epilogue, name, layer=None):
    m = xs[0].shape[0]
    kdim, n = w.shape[-2:]
    if layer is None:
        w_spec = pl.BlockSpec((kdim, tn), lambda j, i: (0, j))
    else:
        w_spec = pl.BlockSpec((None, kdim, tn), lambda j, i: (layer, 0, j))
    k_parts = tuple(x.shape[1] for x in xs)
    assert m % tm == 0 and n % tn == 0 and sum(k_parts) == kdim
    cast = w.dtype != BF16
    scratch = [pltpu.VMEM((tm, tn), F32)]
    if cast:
        scratch.append(pltpu.VMEM((kdim, tn), BF16))
    kern = functools.partial(_mm_ws_kernel, k_parts=k_parts, n_extra=len(extras),
                             n_out=len(out_shapes), tn=tn, cast=cast, epilogue=epilogue)
    x_specs = [pl.BlockSpec((tm, kp), lambda j, i: (i, 0)) for kp in k_parts]
    return pl.pallas_call(
        kern,
        grid=(n // tn, m // tm),
        in_specs=x_specs + [w_spec] + list(extra_specs),
        out_specs=list(out_specs),
        out_shape=list(out_shapes),
        scratch_shapes=scratch,
        compiler_params=_cparams(("parallel", "arbitrary")),
        name=name,
    )(*xs, w, *extras)


def _epi_plain(extras, outs):
    o_ref = outs[0]

    def fn(a, c0):
        o_ref[:, c0:c0 + a.shape[1]] = a.astype(o_ref.dtype)

    return [(None, fn)]


def _epi_resid(extras, outs):
    xres_ref, gate_ref = extras
    o_ref = outs[0]

    def fn(a, c0):
        sl = slice(c0, c0 + a.shape[1])
        o_ref[:, sl] = xres_ref[:, sl] + gate_ref[:, sl] * a

    return [(None, fn)]


def _rope_full(y, cos_ref, sin_ref):
    return y * cos_ref[...] + pltpu.roll(y, HEAD_DIM // 2, 1) * sin_ref[...]


def _rope_mla(r, c_ref, s_ref):
    return r * c_ref[...] + pltpu.roll(r, QK_ROPE // 2, 1) * s_ref[...]


def _epi_even(extras, outs, *, part_tiles, q_scale):
    qn_ref, kn_ref, cos_ref, sin_ref = extras
    o_ref = outs[0]
    part = pl.program_id(0) // part_tiles

    def scaled(scale):
        def fn(a, c0):
            y = a if scale == 1.0 else a * scale
            o_ref[:, c0:c0 + a.shape[1]] = y.astype(o_ref.dtype)
        return fn

    def headnorm_rope(w_ref, scale):
        @_via_ref
        def fn(a, c0):
            for hh in range(a.shape[1] // HEAD_DIM):
                ah = a[:, hh * HEAD_DIM:(hh + 1) * HEAD_DIM]
                y = ah * lax.rsqrt(jnp.mean(ah * ah, axis=-1, keepdims=True) + EPS) * w_ref[...]
                y = _rope_full(y, cos_ref, sin_ref)
                if scale != 1.0:
                    y = y * scale
                o_ref[:, c0 + hh * HEAD_DIM:c0 + (hh + 1) * HEAD_DIM] = y.astype(o_ref.dtype)
        return fn

    return [(part == 0, scaled(q_scale)),
            (part == 4, headnorm_rope(qn_ref, q_scale * LOG2_E)),
            (part == 5, headnorm_rope(kn_ref, 1.0)),
            ((part != 0) & (part != 4) & (part != 5), scaled(1.0))]


def _pro_rms(x_ref, extras):
    x = x_ref[...]
    w_ref = extras[0]
    y = x * lax.rsqrt(jnp.mean(x * x, axis=-1, keepdims=True) + EPS) * w_ref[...]
    return y.astype(BF16)


def _epi_mla_q(extras, outs):
    _, qw_ref, first_ref, c_ref, s_ref = extras
    o_ref = outs[0]
    assert MXU_COLS == MLA_QK_PAD

    @_via_ref
    def fn(a, c0):
        nope, rope = a[:, :QK_NOPE], a[:, QK_NOPE:]
        sq = nope * nope + rope * rope * first_ref[...]
        rstd = lax.rsqrt(jnp.sum(sq, axis=-1, keepdims=True) * (1.0 / QK_HEAD) + EPS)
        o_ref[:, c0:c0 + QK_NOPE] = (nope * rstd * qw_ref[:, :QK_NOPE]).astype(o_ref.dtype)
        r = _rope_mla(rope * rstd * qw_ref[:, QK_NOPE:], c_ref, s_ref)
        o_ref[:, c0 + QK_NOPE:c0 + MLA_QK_PAD] = r.astype(o_ref.dtype)

    return [(None, fn)]


def _epi_mla_kv(extras, outs):
    _, kpe_ref, kwn_ref, kwp_ref, first_ref, c_ref, s_ref = extras
    k_ref, v_ref = outs
    assert MXU_COLS == QK_NOPE + V_HEAD
    kpe = kpe_ref[...]
    ss_pe = jnp.sum(kpe * kpe * first_ref[...], axis=-1, keepdims=True)
    kr = _rope_mla(kpe * kwp_ref[...], c_ref, s_ref)

    @_via_ref
    def fn(a, c0):
        head = c0 // (QK_NOPE + V_HEAD)
        kn = a[:, :QK_NOPE]
        rstd = lax.rsqrt((jnp.sum(kn * kn, axis=-1, keepdims=True) + ss_pe) * (1.0 / QK_HEAD) + EPS)
        kb = head * MLA_QK_PAD
        k_ref[:, kb:kb + QK_NOPE] = (kn * rstd * kwn_ref[...]).astype(k_ref.dtype)
        k_ref[:, kb + QK_NOPE:kb + MLA_QK_PAD] = (kr * rstd).astype(k_ref.dtype)
        v_ref[:, head * V_HEAD:(head + 1) * V_HEAD] = a[:, QK_NOPE:].astype(v_ref.dtype)

    return [(None, fn)]


def _qk(q, k):
    return lax.dot_general(q, k, (((1,), (1,)), ((), ())), preferred_element_type=F32)


def _sb_kernel(q_ref, k_ref, v_ref, g_ref, u_ref, o_ref, acc_ref, carry_ref, *, tile):
    qi = pl.program_id(2)
    n_sub = q_ref.shape[0] // tile

    def tile_rows(kb):
        return pl.ds(pl.multiple_of(kb * tile, tile), tile)

    def weights(q, kb, carry, diagonal=False, valid=None):
        z = _qk(q, k_ref[tile_rows(kb), :])
        sp = jnp.log(1.0 + jnp.exp(-jnp.abs(z)))
        log_beta = jnp.minimum(z, 0.0) - sp
        log_keep = jnp.minimum(-z, 0.0) - sp
        keep = valid
        if diagonal:
            row = lax.broadcasted_iota(jnp.int32, z.shape, 0)
            col = lax.broadcasted_iota(jnp.int32, z.shape, 1)
            keep = col < row
        if keep is not None:
            log_keep = jnp.where(keep, log_keep, 0.0)
        hi = log_keep.astype(BF16)
        lo = (log_keep - hi.astype(F32)).astype(BF16)
        u = u_ref[...]
        excl = jnp.dot(hi, u, preferred_element_type=F32) + jnp.dot(lo, u, preferred_element_type=F32)
        if carry is not None:
            excl = excl + carry
        w = jnp.exp(log_beta + excl)
        if keep is not None:
            w = jnp.where(keep, w, 0.0)
        return w.astype(BF16), jnp.sum(log_keep, axis=-1, keepdims=True)

    def pv(w, kb):
        return jnp.dot(w, v_ref[tile_rows(kb), :], preferred_element_type=F32)

    tails = []
    for sub in range(n_sub):
        rows = slice(sub * tile, (sub + 1) * tile)
        q = q_ref[rows, :]
        diag = qi * n_sub + sub
        prev = jnp.maximum(diag - 1, 0)
        w_d, sum_d = weights(q, diag, None, diagonal=True)
        w_p, sum_p = weights(q, prev, sum_d, valid=diag > 0)
        carry = sum_d + sum_p
        acc_ref[rows, :] = pv(w_d, diag) + pv(w_p, prev)
        carry_ref[rows, :] = carry
        tails.append((rows, diag - 2, jnp.max(carry)))

    for rows, kb0, cmax0 in tails:
        def cond(state):
            kb, cmax = state
            return (kb >= 0) & (cmax > EXP_UNDERFLOW)

        def body(state, rows=rows):
            kb, _ = state
            carry = carry_ref[rows, :]
            w, sum_k = weights(q_ref[rows, :], kb, carry)
            acc_ref[rows, :] += pv(w, kb)
            carry = carry + sum_k
            carry_ref[rows, :] = carry
            return kb - 1, jnp.max(carry)

        lax.while_loop(cond, body, (kb0, cmax0))

    g = g_ref[...].astype(F32)
    o_ref[...] = (acc_ref[...] * (g * jax.nn.sigmoid(g))).astype(o_ref.dtype)


def _dl_kernel(q_ref, k_ref, v_ref, g_ref, bias_ref, o_ref, qf_ref, kf_ref, vf_ref, acc_ref, m_ref, l_ref,
               *, chunk):
    ci = pl.program_id(2)

    @pl.when(ci == 0)
    def _():
        kf_ref[...] = k_ref[...].astype(F32)
        vf_ref[...] = v_ref[...].astype(F32)

    qf_ref[...] = q_ref[...].astype(F32)
    base = ci * chunk

    def body(idx, _):
        for pi, (_, dil) in enumerate(DIL_PATTERNS):
            unit = DL_TILE * dil
            q0 = (idx // dil) * unit + idx % dil
            rows = pl.ds(q0, DL_TILE, stride=dil)
            cur0 = base + q0
            has_prev = cur0 >= unit
            prev0 = jnp.maximum(cur0 - unit, 0)
            prev_rows = pl.ds(prev0, DL_TILE, stride=dil)
            cur_rows = pl.ds(cur0, DL_TILE, stride=dil)
            kk = jnp.concatenate([kf_ref[prev_rows, :], kf_ref[cur_rows, :]], axis=0).astype(BF16)
            vv = jnp.concatenate([vf_ref[prev_rows, :], vf_ref[cur_rows, :]], axis=0).astype(BF16)
            s = _qk(qf_ref[rows, :].astype(BF16), kk) + bias_ref[pi, has_prev.astype(jnp.int32)]
            m = jnp.max(s, axis=-1, keepdims=True)
            p = jnp.exp2(s - m)
            l = jnp.sum(p, axis=-1, keepdims=True)
            acc_ref[pi, rows, :] = jnp.dot(p.astype(BF16), vv, preferred_element_type=F32)
            m_ref[pi, rows, :] = jnp.broadcast_to(m, (DL_TILE, LANES))
            l_ref[pi, rows, :] = jnp.broadcast_to(l, (DL_TILE, LANES))
        return 0

    lax.fori_loop(0, chunk // DL_TILE, body, 0, unroll=True)

    n_pat = len(DIL_PATTERNS)
    m_all = functools.reduce(jnp.maximum, [m_ref[pi] for pi in range(n_pat)])
    num = jnp.zeros(acc_ref.shape[1:], F32)
    den = jnp.zeros(acc_ref.shape[1:], F32)
    for pi in range(n_pat):
        w = jnp.exp2(m_ref[pi] - m_all)
        num = num + w * acc_ref[pi]
        den = den + w * l_ref[pi]
    g = g_ref[...].astype(F32)
    o_ref[...] = (num / den * (g * jax.nn.sigmoid(g))).astype(o_ref.dtype)


def _softmax_update(s_ref, bias_ref, v, p_ref, acc_ref, m_ref, l_ref):
    tq, tk = s_ref.shape
    for r0 in range(0, tq, MLA_ROW_BLOCK):
        rs = slice(r0, r0 + MLA_ROW_BLOCK)
        cols = [s_ref[rs, c * LANES:(c + 1) * LANES] for c in range(tk // LANES)]
        if bias_ref is not None:
            cols = [col + bias_ref[rs, c * LANES:(c + 1) * LANES] for c, col in enumerate(cols)]
        m_old = m_ref[rs, :]
        tile_max = jnp.max(functools.reduce(jnp.maximum, cols), axis=-1, keepdims=True)
        m_new = jnp.maximum(m_old, tile_max)
        ps = [jnp.exp2(col - m_new) for col in cols]
        alpha = jnp.exp2(m_old - m_new)
        l_ref[rs, :] = alpha * l_ref[rs, :] + jnp.sum(functools.reduce(jnp.add, ps), axis=-1, keepdims=True)
        m_ref[rs, :] = m_new
        acc_ref[rs, :] = alpha * acc_ref[rs, :]
        for c, x in enumerate(ps):
            p_ref[rs, c * LANES:(c + 1) * LANES] = x.astype(BF16)
    acc_ref[...] += jnp.dot(p_ref[...], v, preferred_element_type=F32)


def _mla_kernel(q_ref, k_ref, v_ref, g_ref, bias_ref, o_ref, sa_ref, sb_ref, pa_ref, pb_ref, acc_ref, m_ref,
                l_ref, *, tile):
    n_q = q_ref.shape[0] // tile
    acc_ref[...] = jnp.zeros_like(acc_ref)
    m_ref[...] = jnp.full_like(m_ref, -jnp.inf)
    l_ref[...] = jnp.zeros_like(l_ref)

    def rows(t):
        return pl.ds(pl.multiple_of(t * tile, tile), tile)

    def scores(qi, kb, dst_ref):
        dst_ref[...] = _qk(q_ref[rows(qi), :], k_ref[rows(kb), :])

    def pipeline(n_steps, start, advance, bias):
        def update(qi, kb, src_ref, p_ref):
            r = rows(qi)
            _softmax_update(src_ref, bias, v_ref[rows(kb), :], p_ref, acc_ref.at[r, :], m_ref.at[r, :],
                            l_ref.at[r, :])

        scores(*start, sa_ref)

        def pair(_, state):
            qa, ka = state
            qb, kb = advance(qa, ka)
            scores(qb, kb, sb_ref)
            update(qa, ka, sa_ref, pa_ref)
            qa, ka = advance(qb, kb)
            scores(qa, ka, sa_ref)
            update(qb, kb, sb_ref, pb_ref)
            return qa, ka

        qa, ka = lax.fori_loop(0, n_steps // 2, pair, start, unroll=4)
        if n_steps % 2:
            update(qa, ka, sa_ref, pa_ref)

    def next_lower(qi, kb):
        wrap = kb == qi - 1
        return jnp.where(wrap, jnp.minimum(qi + 1, n_q - 1), qi), jnp.where(wrap, 0, kb + 1)

    def next_diagonal(qi, kb):
        t = jnp.minimum(qi + 1, n_q - 1)
        return t, t

    zero, one = jnp.int32(0), jnp.int32(1)
    if n_q > 1:
        pipeline(n_q * (n_q - 1) // 2, (one, zero), next_lower, None)
    pipeline(n_q, (zero, zero), next_diagonal, bias_ref)

    g = g_ref[...].astype(F32)
    o_ref[...] = (acc_ref[...] * g / (l_ref[...] * (1.0 + jnp.exp(-g)))).astype(o_ref.dtype)


def _attn_call(kern, arrays, col_blocks, widths, consts, out_width_total, out_col0, n_heads, tile,
               scratch, name):
    q, k, v, g = arrays
    b, s, _ = q.shape
    cq, ck, cv, cg = col_blocks
    wq, wk, wv, wg = widths
    in_specs = [
        pl.BlockSpec((None, tile, wq), lambda bi, h, i: (bi, i, cq + h)),
        pl.BlockSpec((None, s, wk), lambda bi, h, i: (bi, 0, ck + h)),
        pl.BlockSpec((None, s, wv), lambda bi, h, i: (bi, 0, cv + h)),
        pl.BlockSpec((None, tile, wg), lambda bi, h, i: (bi, i, cg + h)),
    ]
    for cst in consts:
        in_specs.append(pl.BlockSpec(cst.shape, lambda bi, h, i, nd=cst.ndim: (0,) * nd))
    return pl.pallas_call(
        kern,
        grid=(b, n_heads, s // tile),
        in_specs=in_specs,
        out_specs=pl.BlockSpec((None, tile, wv), lambda bi, h, i: (bi, i, out_col0 + h)),
        out_shape=jax.ShapeDtypeStruct((b, s, out_width_total), BF16),
        scratch_shapes=scratch,
        compiler_params=_cparams(("parallel", "parallel", "arbitrary")),
        name=name,
    )(q, k, v, g, *consts)


def _dl_band_bias():
    i = np.arange(DL_TILE)[:, None]
    c = np.arange(2 * DL_TILE)[None, :]
    out = []
    for window, dil in DIL_PATTERNS:
        span = window // dil
        assert span <= DL_TILE
        dist = i + DL_TILE - c
        band = (dist >= 0) & (dist <= span)
        out.append(np.stack([band & (c >= DL_TILE), band]))
    return np.where(np.stack(out), 0.0, -np.inf).astype(np.float32)


def _rope_angles(positions, dim):
    inv = ROPE_THETA ** (-jnp.arange(0, dim, 2, dtype=F32) / dim)
    ang = positions.astype(F32)[..., None] * inv
    return jnp.cos(ang), jnp.sin(ang)


def _even_layer(x, h, gate, li, w_in, qn, kn, w_out, cosf, sinf, sb_u, dl_bias):
    b, s, d = x.shape
    m = b * s
    n_in = w_in.shape[-1]
    w_part = n_in // 8
    n_heads = w_part // HEAD_DIM
    tm, tn = _tile(s, MM_TM), _tile(w_part, MM_TN)
    q_scale = 1.0 / math.sqrt(HEAD_DIM)
    proj = _matmul_ws(
        [h.reshape(m, d)], w_in, layer=li, tm=tm, tn=tn,
        extras=(qn.reshape(1, HEAD_DIM), kn.reshape(1, HEAD_DIM), cosf, sinf),
        extra_specs=[pl.BlockSpec((1, HEAD_DIM), lambda j, i: (0, 0)),
                     pl.BlockSpec((1, HEAD_DIM), lambda j, i: (0, 0)),
                     pl.BlockSpec((tm, HEAD_DIM), lambda j, i: (i, 0)),
                     pl.BlockSpec((tm, HEAD_DIM), lambda j, i: (i, 0))],
        out_shapes=[jax.ShapeDtypeStruct((m, n_in), BF16)],
        out_specs=[pl.BlockSpec((tm, tn), lambda j, i: (i, j))],
        epilogue=functools.partial(_epi_even, part_tiles=w_part // tn, q_scale=q_scale),
        name="even_in_proj")[0].reshape(b, s, n_in)

    hb = n_heads
    w4 = (HEAD_DIM,) * 4
    sb_rows = SB_TILE * SB_SUBTILES
    o_sb = _attn_call(
        functools.partial(_sb_kernel, tile=SB_TILE), (proj,) * 4, (0, hb, 2 * hb, 3 * hb), w4,
        (sb_u,), w_part, 0, n_heads, sb_rows,
        [pltpu.VMEM((sb_rows, HEAD_DIM), F32), pltpu.VMEM((sb_rows, 1), F32)], "stick_breaking")
    assert s % DL_CHUNK == 0
    n_pat = len(DIL_PATTERNS)
    o_dl = _attn_call(
        functools.partial(_dl_kernel, chunk=DL_CHUNK), (proj,) * 4,
        (4 * hb, 5 * hb, 6 * hb, 7 * hb), w4, (dl_bias,), w_part, 0, n_heads, DL_CHUNK,
        [pltpu.VMEM((DL_CHUNK, HEAD_DIM), F32), pltpu.VMEM((s, HEAD_DIM), F32),
         pltpu.VMEM((s, HEAD_DIM), F32), pltpu.VMEM((n_pat, DL_CHUNK, HEAD_DIM), F32),
         pltpu.VMEM((n_pat, DL_CHUNK, LANES), F32), pltpu.VMEM((n_pat, DL_CHUNK, LANES), F32)],
        "dilated_window")
    return _out_proj(x, [o_sb.reshape(m, w_part), o_dl.reshape(m, w_part)], w_out, li, gate)


def _out_proj(x, mixed_parts, w_out, li, gate):
    b, s, d = x.shape
    m = b * s
    tm, tn = _tile(s, MM_TM), _tile(d, MM_TN)
    rows_per_batch = s // tm
    return _matmul_ws(
        mixed_parts, w_out, layer=li, tm=tm, tn=tn,
        extras=(x.reshape(m, d), gate),
        extra_specs=[pl.BlockSpec((tm, tn), lambda j, i: (i, j)),
                     pl.BlockSpec((None, 1, tn), lambda j, i: (i // rows_per_batch, 0, j))],
        out_shapes=[jax.ShapeDtypeStruct((m, d), F32)],
        out_specs=[pl.BlockSpec((tm, tn), lambda j, i: (i, j))],
        epilogue=_epi_resid, name="out_proj")[0].reshape(b, s, d)


def _odd_layer(x, h, gate, li, w_in, q_lat_norm, kv_lat_norm, w_uq, w_ukv, q_norm, k_norm, w_out,
               rc, rs, mla_bias):
    b, s, d = x.shape
    m = b * s
    q_lora = q_lat_norm.shape[0]
    kv_lora = kv_lat_norm.shape[0]
    n_heads = w_uq.shape[1] // QK_HEAD
    lat_raw = q_lora + kv_lora + QK_ROPE
    n_lat = lat_raw + QK_ROPE
    assert n_lat % LANES == 0 and 2 * QK_ROPE == LANES
    h2 = h.reshape(m, d)
    tm, tk = _tile(s, XS_TILE), _tile(d, XS_TILE)
    first = (jnp.arange(LANES) < QK_ROPE).astype(F32).reshape(1, LANES)

    w_lat = jnp.concatenate([w_in[:, :lat_raw], w_in[:, lat_raw - QK_ROPE:lat_raw]], axis=1).astype(BF16)
    tm_lat = _tile(s, LATENT_TM)
    lat = _matmul(h2, w_lat, tm=tm_lat, tn=n_lat, tk=tk, extras=(), extra_specs=[],
                  out_shapes=[jax.ShapeDtypeStruct((m, n_lat), F32)],
                  out_specs=[pl.BlockSpec((tm_lat, n_lat), lambda i, j, k: (i, j))],
                  epilogue=_epi_plain, name="odd_latent_proj")[0]
    n_g = w_in.shape[1] - lat_raw
    tn_g = _tile(n_g, MM_TN)
    g = _matmul_ws([h2], w_in[:, lat_raw:].astype(BF16), tm=tm, tn=tn_g, extras=(), extra_specs=[],
                   out_shapes=[jax.ShapeDtypeStruct((m, n_g), BF16)],
                   out_specs=[pl.BlockSpec((tm, tn_g), lambda j, i: (i, j))],
                   epilogue=_epi_plain, name="odd_gate_proj")[0]

    w_uq3 = w_uq.reshape(q_lora, n_heads, QK_HEAD)
    w_uq_p = jnp.concatenate([w_uq3, w_uq3[:, :, QK_NOPE:]], axis=-1)
    w_uq_p = w_uq_p.reshape(q_lora, n_heads * MLA_QK_PAD).astype(BF16)
    qw = (jnp.concatenate([q_norm, q_norm[QK_NOPE:]]) * (LOG2_E / math.sqrt(QK_HEAD))).reshape(1, MLA_QK_PAD)
    tn_q = _tile(n_heads * MLA_QK_PAD, XS_TILE)
    rope_specs = [pl.BlockSpec((1, LANES), lambda i, j, k: (0, 0))] + \
        [pl.BlockSpec((tm, LANES), lambda i, j, k: (i, 0))] * 2
    q = _matmul(
        lat, w_uq_p, tm=tm, tn=tn_q, tk=q_lora,
        extras=(q_lat_norm.reshape(1, q_lora), qw, first, rc, rs),
        extra_specs=[pl.BlockSpec((1, q_lora), lambda i, j, k: (0, 0)),
                     pl.BlockSpec((1, MLA_QK_PAD), lambda i, j, k: (0, 0))] + rope_specs,
        out_shapes=[jax.ShapeDtypeStruct((m, n_heads * MLA_QK_PAD), BF16)],
        out_specs=[pl.BlockSpec((tm, tn_q), lambda i, j, k: (i, j))],
        epilogue=_epi_mla_q,
        prologue=_pro_rms, name="mla_q_proj")[0]

    assert q_lora % kv_lora == 0 and (q_lora + kv_lora) % LANES == 0
    kvw = QK_NOPE + V_HEAD
    tn_kv = _tile(n_heads * kvw, XS_TILE)
    heads_per_tile = tn_kv // kvw
    kwn = k_norm[:QK_NOPE].reshape(1, QK_NOPE)
    kwp = jnp.concatenate([k_norm[QK_NOPE:], k_norm[QK_NOPE:]]).reshape(1, LANES)
    kpe_block = (q_lora + kv_lora) // LANES
    k, v = _matmul(
        lat, w_ukv.astype(BF16), tm=tm, tn=tn_kv, tk=kv_lora,
        extras=(kv_lat_norm.reshape(1, kv_lora), lat, kwn, kwp, first, rc, rs),
        extra_specs=[pl.BlockSpec((1, kv_lora), lambda i, j, k: (0, 0)),
                     pl.BlockSpec((tm, LANES), lambda i, j, k: (i, kpe_block)),
                     pl.BlockSpec((1, QK_NOPE), lambda i, j, k: (0, 0)),
                     pl.BlockSpec((1, LANES), lambda i, j, k: (0, 0))] + rope_specs,
        out_shapes=[jax.ShapeDtypeStruct((m, n_heads * MLA_QK_PAD), BF16),
                    jax.ShapeDtypeStruct((m, n_heads * V_HEAD), BF16)],
        out_specs=[pl.BlockSpec((tm, heads_per_tile * MLA_QK_PAD), lambda i, j, k: (i, j)),
                   pl.BlockSpec((tm, heads_per_tile * V_HEAD), lambda i, j, k: (i, j))],
        epilogue=_epi_mla_kv,
        prologue=_pro_rms, x_col_block=q_lora // kv_lora, name="mla_kv_proj")
    mixed = _attn_call(
        functools.partial(_mla_kernel, tile=MLA_TILE),
        (q.reshape(b, s, -1), k.reshape(b, s, -1), v.reshape(b, s, -1), g.reshape(b, s, -1)),
        (0, 0, 0, 0), (MLA_QK_PAD, MLA_QK_PAD, V_HEAD, V_HEAD), (mla_bias,), n_heads * V_HEAD, 0,
        n_heads, s,
        [pltpu.VMEM((MLA_TILE, MLA_TILE), F32), pltpu.VMEM((MLA_TILE, MLA_TILE), F32),
         pltpu.VMEM((MLA_TILE, MLA_TILE), BF16), pltpu.VMEM((MLA_TILE, MLA_TILE), BF16),
         pltpu.VMEM((s, V_HEAD), F32), pltpu.VMEM((s, LANES), F32),
         pltpu.VMEM((s, LANES), F32)], "mla_attention")
    return _out_proj(x, [mixed.reshape(m, -1)], w_out, li, gate)


def kernel(x, c, positions, ada_w, ada_b, norm_w, ev_w_in, ev_q_norm, ev_k_norm, ev_w_out, od_w_in,
           od_q_lat_norm, od_kv_lat_norm, od_w_uq, od_w_ukv, od_q_norm, od_k_norm, od_w_out):
    b, s, d = x.shape
    depth = ada_w.shape[0]
    m = b * s

    cos_f, sin_f = _rope_angles(positions, HEAD_DIM)
    cosf = jnp.concatenate([cos_f, cos_f], axis=-1).reshape(m, HEAD_DIM)
    sinf = jnp.concatenate([-sin_f, sin_f], axis=-1).reshape(m, HEAD_DIM)
    cos_m, sin_m = _rope_angles(positions, QK_ROPE)
    z32 = jnp.zeros_like(cos_m)
    z64 = jnp.concatenate([z32, z32], axis=-1)
    rc = jnp.concatenate([cos_m, cos_m, z64], axis=-1).reshape(m, LANES)
    rs = jnp.concatenate([-sin_m, sin_m, z64], axis=-1).reshape(m, LANES)

    sb_u = jnp.asarray(np.tril(np.ones((SB_TILE, SB_TILE), np.float32), -1), BF16)
    dl_bias = jnp.asarray(_dl_band_bias())
    causal = np.tril(np.ones((MLA_TILE, MLA_TILE), bool))
    mla_bias = jnp.asarray(np.where(causal, 0.0, -np.inf).astype(np.float32))

    mod = _modulation(c, ada_w, ada_b)
    shift, scale, gate = (mod[:, :, i * d:(i + 1) * d].reshape(depth, b, 1, d) for i in range(3))

    for layer in range(depth):
        h = _norm_mod(x, norm_w[layer], scale[layer], shift[layer])
        i = layer // 2
        if layer % 2 == 0:
            x = _even_layer(x, h, gate[layer], i, ev_w_in, ev_q_norm[i], ev_k_norm[i], ev_w_out,
                            cosf, sinf, sb_u, dl_bias)
        else:
            x = _odd_layer(x, h, gate[layer], i, od_w_in[i], od_q_lat_norm[i], od_kv_lat_norm[i],
                           od_w_uq[i], od_w_ukv[i], od_q_norm[i], od_k_norm[i], od_w_out,
                           rc, rs, mla_bias)
    return x
```

```python
import functools
import math

import numpy as np
import jax
import jax.numpy as jnp
from jax import lax
from jax.experimental import pallas as pl
from jax.experimental.pallas import tpu as pltpu

F32 = jnp.float32
BF16 = jnp.bfloat16

HEAD_DIM = 128
DIL_PATTERNS = ((128, 1), (512, 4), (2048, 16))
QK_NOPE = 128
QK_ROPE = 64
QK_HEAD = QK_NOPE + QK_ROPE
V_HEAD = 128
MLA_QK_PAD = 256
ROPE_THETA = 10000.0
EPS = 1e-6

LANES = 128
SUBLANES = 8
MXU_COLS = 256
VMEM_LIMIT = 59 * 1024 * 1024
EXP_UNDERFLOW = -104.0
LOG2_E = math.log2(math.e)
SB_TILE = 256
SB_SUBTILES = 8
DL_TILE = 128
DL_CHUNK = DL_TILE * max(d for _, d in DIL_PATTERNS)
MLA_TILE = 512
MLA_ROW_BLOCK = 64
MM_TM = 512
MM_TN = 1024
XS_TILE = 1024
LATENT_TM = 512
MOD_TN = 1024
NORM_ROWS = 512
NORM_ROW_BLOCK = 16


def _cparams(sem):
    return pltpu.CompilerParams(dimension_semantics=sem, vmem_limit_bytes=VMEM_LIMIT)


def _tile(n, want):
    if n <= want:
        return n
    t = (want // LANES) * LANES
    while n % t:
        t -= LANES
    return t


def _mod_kernel(c_ref, w_ref, b_ref, o_ref):
    a = c_ref[...]
    a = (a * jax.nn.sigmoid(a)).astype(BF16)
    o_ref[...] = jnp.dot(a, w_ref[...].astype(BF16), preferred_element_type=F32) + b_ref[...]


def _modulation(c, ada_w, ada_b):
    depth, d, n3 = ada_w.shape
    b = c.shape[0]
    rows = -(-b // SUBLANES) * SUBLANES
    c_pad = jnp.pad(c, ((0, rows - b), (0, 0)))
    tn = _tile(n3, MOD_TN)
    out = pl.pallas_call(
        _mod_kernel,
        grid=(depth, n3 // tn),
        in_specs=[pl.BlockSpec((rows, d), lambda l, j: (0, 0)),
                  pl.BlockSpec((None, d, tn), lambda l, j: (l, 0, j)),
                  pl.BlockSpec((None, 1, tn), lambda l, j: (l, 0, j))],
        out_specs=pl.BlockSpec((None, rows, tn), lambda l, j: (l, 0, j)),
        out_shape=jax.ShapeDtypeStruct((depth, rows, n3), F32),
        compiler_params=_cparams(("parallel", "parallel")),
        name="modulation",
    )(c_pad, ada_w, ada_b.reshape(depth, 1, n3))
    return out[:, :b]


def _norm_mod_kernel(x_ref, w_ref, scale_ref, shift_ref, o_ref):
    gain = w_ref[...] * (1.0 + scale_ref[...])

    def body(r, _):
        rows = pl.ds(pl.multiple_of(r * NORM_ROW_BLOCK, NORM_ROW_BLOCK), NORM_ROW_BLOCK)
        x = x_ref[rows, :]
        rstd = lax.rsqrt(jnp.mean(x * x, axis=-1, keepdims=True) + EPS)
        o_ref[rows, :] = (x_ref[rows, :] * rstd * gain + shift_ref[...]).astype(o_ref.dtype)
        return 0

    lax.fori_loop(0, x_ref.shape[0] // NORM_ROW_BLOCK, body, 0, unroll=2)


def _norm_mod(x, w, scale, shift):
    b, s, d = x.shape
    ts = _tile(s, NORM_ROWS)
    return pl.pallas_call(
        _norm_mod_kernel,
        grid=(b, s // ts),
        in_specs=[pl.BlockSpec((None, ts, d), lambda bi, i: (bi, i, 0)),
                  pl.BlockSpec((1, d), lambda bi, i: (0, 0)),
                  pl.BlockSpec((None, 1, d), lambda bi, i: (bi, 0, 0)),
                  pl.BlockSpec((None, 1, d), lambda bi, i: (bi, 0, 0))],
        out_specs=pl.BlockSpec((None, ts, d), lambda bi, i: (bi, i, 0)),
        out_shape=jax.ShapeDtypeStruct((b, s, d), BF16),
        compiler_params=_cparams(("parallel", "parallel")),
        name="norm_mod",
    )(x, w.reshape(1, d), scale, shift)


def _run_epilogue(variants, chunk_of, tn, acc_ref):
    chunks = [(c0, min(MXU_COLS, tn - c0)) for c0 in range(0, tn, MXU_COLS)]

    def fused(fn):
        for c0, w in chunks:
            fn(chunk_of(c0, w), c0)

    def matmul_only():
        acc_ref[...] = chunk_of(0, tn)

    def epilogue_only(fn):
        for c0, w in chunks:
            fn(acc_ref.at[:, c0:c0 + w], c0)

    split = [(cond, fn) for cond, fn in variants if getattr(fn, "via_ref", False) and cond is not None]
    if split:
        pl.when(functools.reduce(jnp.logical_or, [cond for cond, _ in split]))(matmul_only)
    for cond, fn in variants:
        if getattr(fn, "via_ref", False):
            if cond is None:
                matmul_only()
                epilogue_only(fn)
            else:
                pl.when(cond)(functools.partial(epilogue_only, fn))
        elif cond is None:
            fused(fn)
        else:
            pl.when(cond)(functools.partial(fused, fn))


def _via_ref(fn):
    fn.via_ref = True
    return fn


def _mm_kernel(*refs, n_extra, n_out, nk, tn, epilogue, prologue):
    x_ref, w_ref = refs[0], refs[1]
    extras = refs[2:2 + n_extra]
    outs = refs[2 + n_extra:2 + n_extra + n_out]
    scratch = refs[2 + n_extra + n_out:]
    acc_ref = scratch[0]
    if prologue is not None:
        h_ref = scratch[1]

        @pl.when(pl.program_id(1) == 0)
        def _():
            h_ref[...] = prologue(x_ref, extras)

        _run_epilogue(epilogue(extras, outs),
                      lambda c0, w: jnp.dot(h_ref[...], w_ref[:, c0:c0 + w], preferred_element_type=F32),
                      tn, acc_ref)
        return
    k = pl.program_id(2)

    @pl.when(k == 0)
    def _():
        acc_ref[...] = jnp.zeros_like(acc_ref)

    acc_ref[...] += jnp.dot(x_ref[...], w_ref[...], preferred_element_type=F32)

    @pl.when(k == nk - 1)
    def _():
        _run_epilogue(epilogue(extras, outs), lambda c0, w: acc_ref[:, c0:c0 + w], tn, acc_ref)


def _matmul(x, w, *, tm, tn, tk, extras, extra_specs, out_shapes, out_specs, epilogue,
            prologue=None, x_col_block=0, name):
    m = x.shape[0]
    kdim, n = w.shape
    nk = kdim // tk
    assert m % tm == 0 and n % tn == 0 and kdim % tk == 0 and x.shape[1] >= (x_col_block + nk) * tk
    scratch = [pltpu.VMEM((tm, tn), F32)]
    if prologue is not None:
        assert nk == 1
        scratch.append(pltpu.VMEM((tm, kdim), BF16))
    kern = functools.partial(_mm_kernel, n_extra=len(extras), n_out=len(out_shapes), nk=nk, tn=tn,
                             epilogue=epilogue, prologue=prologue)
    return pl.pallas_call(
        kern,
        grid=(m // tm, n // tn, nk),
        in_specs=[pl.BlockSpec((tm, tk), lambda i, j, k: (i, x_col_block + k)),
                  pl.BlockSpec((tk, tn), lambda i, j, k: (k, j))] + list(extra_specs),
        out_specs=list(out_specs),
        out_shape=list(out_shapes),
        scratch_shapes=scratch,
        compiler_params=_cparams(("parallel", "arbitrary", "arbitrary")),
        name=name,
    )(x, w, *extras)


def _mm_ws_kernel(*refs, k_parts, n_extra, n_out, tn, cast, epilogue):
    n_x = len(k_parts)
    x_refs = refs[:n_x]
    w_ref = refs[n_x]
    extras = refs[n_x + 1:n_x + 1 + n_extra]
    outs = refs[n_x + 1 + n_extra:n_x + 1 + n_extra + n_out]
    scratch = refs[n_x + 1 + n_extra + n_out:]
    acc_ref = scratch[0]
    if cast:
        wb_ref = scratch[1]

        @pl.when(pl.program_id(1) == 0)
        def _():
            wb_ref[...] = w_ref[...].astype(BF16)
    else:
        wb_ref = w_ref

    def chunk_of(c0, w):
        acc = None
        k0 = 0
        for x_ref, kp in zip(x_refs, k_parts):
            part = jnp.dot(x_ref[...], wb_ref[k0:k0 + kp, c0:c0 + w], preferred_element_type=F32)
            acc = part if acc is None else acc + part
            k0 += kp
        return acc

    _run_epilogue(epilogue(extras, outs), chunk_of, tn, acc_ref)


def _matmul_ws(xs, w, *, tm, tn, extras, extra_specs, out_shapes, out_specs, epilogue, name, layer=None):
    m = xs[0].shape[0]
    kdim, n = w.shape[-2:]
    if layer is None:
        w_spec = pl.BlockSpec((kdim, tn), lambda j, i: (0, j))
    else:
        w_spec = pl.BlockSpec((None, kdim, tn), lambda j, i: (layer, 0, j))
    k_parts = tuple(x.shape[1] for x in xs)
    assert m % tm == 0 and n % tn == 0 and sum(k_parts) == kdim
    cast = w.dtype != BF16
    scratch = [pltpu.VMEM((tm, tn), F32)]
    if cast:
        scratch.append(pltpu.VMEM((kdim, tn), BF16))
    kern = functools.partial(_mm_ws_kernel, k_parts=k_parts, n_extra=len(extras),
                             n_out=len(out_shapes), tn=tn, cast=cast, epilogue=epilogue)
    x_specs = [pl.BlockSpec((tm, kp), lambda j, i: (i, 0)) for kp in k_parts]
    return pl.pallas_call(
        kern,
        grid=(n // tn, m // tm),
        in_specs=x_specs + [w_spec] + list(extra_specs),
        out_specs=list(out_specs),
        out_shape=list(out_shapes),
        scratch_shapes=scratch,
        compiler_params=_cparams(("parallel", "arbitrary")),
        name=name,
    )(*xs, w, *extras)


def _epi_plain(extras, outs):
    o_ref = outs[0]

    def fn(a, c0):
        o_ref[:, c0:c0 + a.shape[1]] = a.astype(o_ref.dtype)

    return [(None, fn)]


def _epi_resid(extras, outs):
    xres_ref, gate_ref = extras
    o_ref = outs[0]

    def fn(a, c0):
        sl = slice(c0, c0 + a.shape[1])
        o_ref[:, sl] = xres_ref[:, sl] + gate_ref[:, sl] * a

    return [(None, fn)]


def _rope_full(y, cos_ref, sin_ref):
    return y * cos_ref[...] + pltpu.roll(y, HEAD_DIM // 2, 1) * sin_ref[...]


def _rope_mla(r, c_ref, s_ref):
    return r * c_ref[...] + pltpu.roll(r, QK_ROPE // 2, 1) * s_ref[...]


def _epi_even(extras, outs, *, part_tiles, q_scale):
    qn_ref, kn_ref, cos_ref, sin_ref = extras
    o_ref = outs[0]
    part = pl.program_id(0) // part_tiles

    def scaled(scale):
        def fn(a, c0):
            y = a if scale == 1.0 else a * scale
            o_ref[:, c0:c0 + a.shape[1]] = y.astype(o_ref.dtype)
        return fn

    def headnorm_rope(w_ref, scale):
        @_via_ref
        def fn(a, c0):
            for hh in range(a.shape[1] // HEAD_DIM):
                ah = a[:, hh * HEAD_DIM:(hh + 1) * HEAD_DIM]
                y = ah * lax.rsqrt(jnp.mean(ah * ah, axis=-1, keepdims=True) + EPS) * w_ref[...]
                y = _rope_full(y, cos_ref, sin_ref)
                if scale != 1.0:
                    y = y * scale
                o_ref[:, c0 + hh * HEAD_DIM:c0 + (hh + 1) * HEAD_DIM] = y.astype(o_ref.dtype)
        return fn

    return [(part == 0, scaled(q_scale)),
            (part == 4, headnorm_rope(qn_ref, q_scale * LOG2_E)),
            (part == 5, headnorm_rope(kn_ref, 1.0)),
            ((part != 0) & (part != 4) & (part != 5), scaled(1.0))]


def _pro_rms(x_ref, extras):
    x = x_ref[...]
    w_ref = extras[0]
    y = x * lax.rsqrt(jnp.mean(x * x, axis=-1, keepdims=True) + EPS) * w_ref[...]
    return y.astype(BF16)


def _epi_mla_q(extras, outs):
    _, qw_ref, first_ref, c_ref, s_ref = extras
    o_ref = outs[0]
    assert MXU_COLS == MLA_QK_PAD

    @_via_ref
    def fn(a, c0):
        nope, rope = a[:, :QK_NOPE], a[:, QK_NOPE:]
        sq = nope * nope + rope * rope * first_ref[...]
        rstd = lax.rsqrt(jnp.sum(sq, axis=-1, keepdims=True) * (1.0 / QK_HEAD) + EPS)
        o_ref[:, c0:c0 + QK_NOPE] = (nope * rstd * qw_ref[:, :QK_NOPE]).astype(o_ref.dtype)
        r = _rope_mla(rope * rstd * qw_ref[:, QK_NOPE:], c_ref, s_ref)
        o_ref[:, c0 + QK_NOPE:c0 + MLA_QK_PAD] = r.astype(o_ref.dtype)

    return [(None, fn)]


def _epi_mla_kv(extras, outs):
    _, kpe_ref, kwn_ref, kwp_ref, first_ref, c_ref, s_ref = extras
    k_ref, v_ref = outs
    assert MXU_COLS == QK_NOPE + V_HEAD
    kpe = kpe_ref[...]
    ss_pe = jnp.sum(kpe * kpe * first_ref[...], axis=-1, keepdims=True)
    kr = _rope_mla(kpe * kwp_ref[...], c_ref, s_ref)

    @_via_ref
    def fn(a, c0):
        head = c0 // (QK_NOPE + V_HEAD)
        kn = a[:, :QK_NOPE]
        rstd = lax.rsqrt((jnp.sum(kn * kn, axis=-1, keepdims=True) + ss_pe) * (1.0 / QK_HEAD) + EPS)
        kb = head * MLA_QK_PAD
        k_ref[:, kb:kb + QK_NOPE] = (kn * rstd * kwn_ref[...]).astype(k_ref.dtype)
        k_ref[:, kb + QK_NOPE:kb + MLA_QK_PAD] = (kr * rstd).astype(k_ref.dtype)
        v_ref[:, head * V_HEAD:(head + 1) * V_HEAD] = a[:, QK_NOPE:].astype(v_ref.dtype)

    return [(None, fn)]


def _qk(q, k):
    return lax.dot_general(q, k, (((1,), (1,)), ((), ())), preferred_element_type=F32)


def _sb_stages(q_ref, k_ref, v_ref, g_ref, u_ref, o_ref, acc_ref, carry_ref, qi, tile):
    n_sub = q_ref.shape[0] // tile

    def tile_rows(kb):
        return pl.ds(pl.multiple_of(kb * tile, tile), tile)

    def weights(q, kb, carry, diagonal=False, valid=None):
        z = _qk(q, k_ref[tile_rows(kb), :])
        sp = jnp.log(1.0 + jnp.exp(-jnp.abs(z)))
        log_beta = jnp.minimum(z, 0.0) - sp
        log_keep = jnp.minimum(-z, 0.0) - sp
        keep = valid
        if diagonal:
            row = lax.broadcasted_iota(jnp.int32, z.shape, 0)
            col = lax.broadcasted_iota(jnp.int32, z.shape, 1)
            keep = col < row
        if keep is not None:
            log_keep = jnp.where(keep, log_keep, 0.0)
        hi = log_keep.astype(BF16)
        lo = (log_keep - hi.astype(F32)).astype(BF16)
        u = u_ref[...]
        excl = jnp.dot(hi, u, preferred_element_type=F32) + jnp.dot(lo, u, preferred_element_type=F32)
        if carry is not None:
            excl = excl + carry
        w = jnp.exp(log_beta + excl)
        if keep is not None:
            w = jnp.where(keep, w, 0.0)
        return w.astype(BF16), jnp.sum(log_keep, axis=-1, keepdims=True)

    def pv(w, kb):
        return jnp.dot(w, v_ref[tile_rows(kb), :], preferred_element_type=F32)

    def front():
        tails = []
        for sub in range(n_sub):
            rows = slice(sub * tile, (sub + 1) * tile)
            q = q_ref[rows, :]
            diag = qi * n_sub + sub
            prev = jnp.maximum(diag - 1, 0)
            w_d, sum_d = weights(q, diag, None, diagonal=True)
            w_p, sum_p = weights(q, prev, sum_d, valid=diag > 0)
            carry = sum_d + sum_p
            acc_ref[rows, :] = pv(w_d, diag) + pv(w_p, prev)
            carry_ref[rows, :] = carry
            tails.append((rows, diag - 2, jnp.max(carry)))
        return tails

    def tail(tails):
        for rows, kb0, cmax0 in tails:
            def cond(state):
                kb, cmax = state
                return (kb >= 0) & (cmax > EXP_UNDERFLOW)

            def body(state, rows=rows):
                kb, _ = state
                carry = carry_ref[rows, :]
                w, sum_k = weights(q_ref[rows, :], kb, carry)
                acc_ref[rows, :] += pv(w, kb)
                carry = carry + sum_k
                carry_ref[rows, :] = carry
                return kb - 1, jnp.max(carry)

            lax.while_loop(cond, body, (kb0, cmax0))

    def finish():
        g = g_ref[...].astype(F32)
        o_ref[...] = (acc_ref[...] * (g * jax.nn.sigmoid(g))).astype(o_ref.dtype)

    return front, tail, finish


def _dl_stages(q_ref, k_ref, v_ref, g_ref, bias_ref, o_ref, qf_ref, kf_ref, vf_ref, acc_ref, m_ref, l_ref,
               ci, chunk):
    def prepare():
        @pl.when(ci == 0)
        def _():
            kf_ref[...] = k_ref[...].astype(F32)
            vf_ref[...] = v_ref[...].astype(F32)

    base = ci * chunk

    def body(idx, _):
        for pi, (_, dil) in enumerate(DIL_PATTERNS):
            unit = DL_TILE * dil
            q0 = (idx // dil) * unit + idx % dil
            rows = pl.ds(q0, DL_TILE, stride=dil)
            cur0 = base + q0
            has_prev = cur0 >= unit
            prev0 = jnp.maximum(cur0 - unit, 0)
            prev_rows = pl.ds(prev0, DL_TILE, stride=dil)
            cur_rows = pl.ds(cur0, DL_TILE, stride=dil)
            kk = jnp.concatenate([kf_ref[prev_rows, :], kf_ref[cur_rows, :]], axis=0).astype(BF16)
            vv = jnp.concatenate([vf_ref[prev_rows, :], vf_ref[cur_rows, :]], axis=0).astype(BF16)
            s = _qk(qf_ref[rows, :].astype(BF16), kk) + bias_ref[pi, has_prev.astype(jnp.int32)]
            m = jnp.max(s, axis=-1, keepdims=True)
            p = jnp.exp2(s - m)
            l = jnp.sum(p, axis=-1, keepdims=True)
            acc_ref[pi, rows, :] = jnp.dot(p.astype(BF16), vv, preferred_element_type=F32)
            m_ref[pi, rows, :] = jnp.broadcast_to(m, (DL_TILE, LANES))
            l_ref[pi, rows, :] = jnp.broadcast_to(l, (DL_TILE, LANES))
        return 0

    def front():
        qf_ref[...] = q_ref[...].astype(F32)
        lax.fori_loop(0, chunk // DL_TILE, body, 0, unroll=True)

    def finish():
        n_pat = len(DIL_PATTERNS)
        m_all = functools.reduce(jnp.maximum, [m_ref[pi] for pi in range(n_pat)])
        num = jnp.zeros(acc_ref.shape[1:], F32)
        den = jnp.zeros(acc_ref.shape[1:], F32)
        for pi in range(n_pat):
            w = jnp.exp2(m_ref[pi] - m_all)
            num = num + w * acc_ref[pi]
            den = den + w * l_ref[pi]
        g = g_ref[...].astype(F32)
        o_ref[...] = (num / den * (g * jax.nn.sigmoid(g))).astype(o_ref.dtype)

    return prepare, front, finish


def _even_attn_kernel(sq_ref, sk_ref, sv_ref, sg_ref, dq_ref, dk_ref, dv_ref, dg_ref, u_ref, bias_ref,
                      o_sb_ref, o_dl_ref, sb_acc_ref, sb_carry_ref, qf_ref, kf_ref, vf_ref, dl_acc_ref,
                      dl_m_ref, dl_l_ref, *, sb_tile, chunk):
    ci = pl.program_id(2)
    sb_front, sb_tail, sb_finish = _sb_stages(sq_ref, sk_ref, sv_ref, sg_ref, u_ref, o_sb_ref, sb_acc_ref,
                                              sb_carry_ref, ci, sb_tile)
    dl_prepare, dl_front, dl_finish = _dl_stages(dq_ref, dk_ref, dv_ref, dg_ref, bias_ref, o_dl_ref, qf_ref,
                                                 kf_ref, vf_ref, dl_acc_ref, dl_m_ref, dl_l_ref, ci, chunk)
    dl_prepare()
    tails = sb_front()
    dl_front()
    sb_tail(tails)
    sb_finish()
    dl_finish()


def _softmax_update(s_ref, bias_ref, v, p_ref, acc_ref, m_ref, l_ref):
    tq, tk = s_ref.shape
    for r0 in range(0, tq, MLA_ROW_BLOCK):
        rs = slice(r0, r0 + MLA_ROW_BLOCK)
        cols = [s_ref[rs, c * LANES:(c + 1) * LANES] for c in range(tk // LANES)]
        if bias_ref is not None:
            cols = [col + bias_ref[rs, c * LANES:(c + 1) * LANES] for c, col in enumerate(cols)]
        m_old = m_ref[rs, :]
        tile_max = jnp.max(functools.reduce(jnp.maximum, cols), axis=-1, keepdims=True)
        m_new = jnp.maximum(m_old, tile_max)
        ps = [jnp.exp2(col - m_new) for col in cols]
        alpha = jnp.exp2(m_old - m_new)
        l_ref[rs, :] = alpha * l_ref[rs, :] + jnp.sum(functools.reduce(jnp.add, ps), axis=-1, keepdims=True)
        m_ref[rs, :] = m_new
        acc_ref[rs, :] = alpha * acc_ref[rs, :]
        for c, x in enumerate(ps):
            p_ref[rs, c * LANES:(c + 1) * LANES] = x.astype(BF16)
    acc_ref[...] += jnp.dot(p_ref[...], v, preferred_element_type=F32)


def _mla_kernel(q_ref, k_ref, v_ref, g_ref, bias_ref, o_ref, sa_ref, sb_ref, pa_ref, pb_ref, acc_ref, m_ref,
                l_ref, *, tile):
    n_q = q_ref.shape[0] // tile
    acc_ref[...] = jnp.zeros_like(acc_ref)
    m_ref[...] = jnp.full_like(m_ref, -jnp.inf)
    l_ref[...] = jnp.zeros_like(l_ref)

    def rows(t):
        return pl.ds(pl.multiple_of(t * tile, tile), tile)

    def scores(qi, kb, dst_ref):
        dst_ref[...] = _qk(q_ref[rows(qi), :], k_ref[rows(kb), :])

    def pipeline(n_steps, start, advance, bias):
        def update(qi, kb, src_ref, p_ref):
            r = rows(qi)
            _softmax_update(src_ref, bias, v_ref[rows(kb), :], p_ref, acc_ref.at[r, :], m_ref.at[r, :],
                            l_ref.at[r, :])

        scores(*start, sa_ref)

        def pair(_, state):
            qa, ka = state
            qb, kb = advance(qa, ka)
            scores(qb, kb, sb_ref)
            update(qa, ka, sa_ref, pa_ref)
            qa, ka = advance(qb, kb)
            scores(qa, ka, sa_ref)
            update(qb, kb, sb_ref, pb_ref)
            return qa, ka

        qa, ka = lax.fori_loop(0, n_steps // 2, pair, start, unroll=4)
        if n_steps % 2:
            update(qa, ka, sa_ref, pa_ref)

    def next_lower(qi, kb):
        wrap = kb == qi - 1
        return jnp.where(wrap, jnp.minimum(qi + 1, n_q - 1), qi), jnp.where(wrap, 0, kb + 1)

    def next_diagonal(qi, kb):
        t = jnp.minimum(qi + 1, n_q - 1)
        return t, t

    zero, one = jnp.int32(0), jnp.int32(1)
    if n_q > 1:
        pipeline(n_q * (n_q - 1) // 2, (one, zero), next_lower, None)
    pipeline(n_q, (zero, zero), next_diagonal, bias_ref)

    g = g_ref[...].astype(F32)
    o_ref[...] = (acc_ref[...] * g / (l_ref[...] * (1.0 + jnp.exp(-g)))).astype(o_ref.dtype)


def _attn_call(kern, arrays, col_blocks, widths, consts, out_width_total, out_col0, n_heads, tile,
               scratch, name):
    q, k, v, g = arrays
    b, s, _ = q.shape
    cq, ck, cv, cg = col_blocks
    wq, wk, wv, wg = widths
    in_specs = [
        pl.BlockSpec((None, tile, wq), lambda bi, h, i: (bi, i, cq + h)),
        pl.BlockSpec((None, s, wk), lambda bi, h, i: (bi, 0, ck + h)),
        pl.BlockSpec((None, s, wv), lambda bi, h, i: (bi, 0, cv + h)),
        pl.BlockSpec((None, tile, wg), lambda bi, h, i: (bi, i, cg + h)),
    ]
    for cst in consts:
        in_specs.append(pl.BlockSpec(cst.shape, lambda bi, h, i, nd=cst.ndim: (0,) * nd))
    return pl.pallas_call(
        kern,
        grid=(b, n_heads, s // tile),
        in_specs=in_specs,
        out_specs=pl.BlockSpec((None, tile, wv), lambda bi, h, i: (bi, i, out_col0 + h)),
        out_shape=jax.ShapeDtypeStruct((b, s, out_width_total), BF16),
        scratch_shapes=scratch,
        compiler_params=_cparams(("parallel", "parallel", "arbitrary")),
        name=name,
    )(q, k, v, g, *consts)


def _dl_band_bias():
    i = np.arange(DL_TILE)[:, None]
    c = np.arange(2 * DL_TILE)[None, :]
    out = []
    for window, dil in DIL_PATTERNS:
        span = window // dil
        assert span <= DL_TILE
        dist = i + DL_TILE - c
        band = (dist >= 0) & (dist <= span)
        out.append(np.stack([band & (c >= DL_TILE), band]))
    return np.where(np.stack(out), 0.0, -np.inf).astype(np.float32)


def _rope_angles(positions, dim):
    inv = ROPE_THETA ** (-jnp.arange(0, dim, 2, dtype=F32) / dim)
    ang = positions.astype(F32)[..., None] * inv
    return jnp.cos(ang), jnp.sin(ang)


def _even_layer(x, h, gate, li, w_in, qn, kn, w_out, cosf, sinf, sb_u, dl_bias):
    b, s, d = x.shape
    m = b * s
    n_in = w_in.shape[-1]
    w_part = n_in // 8
    n_heads = w_part // HEAD_DIM
    tm, tn = _tile(s, MM_TM), _tile(w_part, MM_TN)
    q_scale = 1.0 / math.sqrt(HEAD_DIM)
    proj = _matmul_ws(
        [h.reshape(m, d)], w_in, layer=li, tm=tm, tn=tn,
        extras=(qn.reshape(1, HEAD_DIM), kn.reshape(1, HEAD_DIM), cosf, sinf),
        extra_specs=[pl.BlockSpec((1, HEAD_DIM), lambda j, i: (0, 0)),
                     pl.BlockSpec((1, HEAD_DIM), lambda j, i: (0, 0)),
                     pl.BlockSpec((tm, HEAD_DIM), lambda j, i: (i, 0)),
                     pl.BlockSpec((tm, HEAD_DIM), lambda j, i: (i, 0))],
        out_shapes=[jax.ShapeDtypeStruct((m, n_in), BF16)],
        out_specs=[pl.BlockSpec((tm, tn), lambda j, i: (i, j))],
        epilogue=functools.partial(_epi_even, part_tiles=w_part // tn, q_scale=q_scale),
        name="even_in_proj")[0].reshape(b, s, n_in)

    assert s % DL_CHUNK == 0 and SB_TILE * SB_SUBTILES == DL_CHUNK
    n_pat = len(DIL_PATTERNS)

    def head_block(part, rows):
        if rows == s:
            return pl.BlockSpec((None, s, HEAD_DIM), lambda bi, h, i: (bi, 0, part * n_heads + h))
        return pl.BlockSpec((None, rows, HEAD_DIM), lambda bi, h, i: (bi, i, part * n_heads + h))

    part_rows = (DL_CHUNK, s, s, DL_CHUNK) * 2
    out_spec = pl.BlockSpec((None, DL_CHUNK, HEAD_DIM), lambda bi, h, i: (bi, i, h))
    o_sb, o_dl = pl.pallas_call(
        functools.partial(_even_attn_kernel, sb_tile=SB_TILE, chunk=DL_CHUNK),
        grid=(b, n_heads, s // DL_CHUNK),
        in_specs=[head_block(p, r) for p, r in enumerate(part_rows)]
        + [pl.BlockSpec(sb_u.shape, lambda bi, h, i: (0, 0)),
           pl.BlockSpec(dl_bias.shape, lambda bi, h, i: (0, 0, 0, 0))],
        out_specs=[out_spec, out_spec],
        out_shape=[jax.ShapeDtypeStruct((b, s, w_part), BF16)] * 2,
        scratch_shapes=[pltpu.VMEM((DL_CHUNK, HEAD_DIM), F32), pltpu.VMEM((DL_CHUNK, 1), F32),
                        pltpu.VMEM((DL_CHUNK, HEAD_DIM), F32), pltpu.VMEM((s, HEAD_DIM), F32),
                        pltpu.VMEM((s, HEAD_DIM), F32), pltpu.VMEM((n_pat, DL_CHUNK, HEAD_DIM), F32),
                        pltpu.VMEM((n_pat, DL_CHUNK, LANES), F32), pltpu.VMEM((n_pat, DL_CHUNK, LANES), F32)],
        compiler_params=_cparams(("parallel", "parallel", "arbitrary")),
        name="even_attention",
    )(*([proj] * 8), sb_u, dl_bias)
    return _out_proj(x, [o_sb.reshape(m, w_part), o_dl.reshape(m, w_part)], w_out, li, gate)


def _out_proj(x, mixed_parts, w_out, li, gate):
    b, s, d = x.shape
    m = b * s
    tm, tn = _tile(s, MM_TM), _tile(d, MM_TN)
    rows_per_batch = s // tm
    return _matmul_ws(
        mixed_parts, w_out, layer=li, tm=tm, tn=tn,
        extras=(x.reshape(m, d), gate),
        extra_specs=[pl.BlockSpec((tm, tn), lambda j, i: (i, j)),
                     pl.BlockSpec((None, 1, tn), lambda j, i: (i // rows_per_batch, 0, j))],
        out_shapes=[jax.ShapeDtypeStruct((m, d), F32)],
        out_specs=[pl.BlockSpec((tm, tn), lambda j, i: (i, j))],
        epilogue=_epi_resid, name="out_proj")[0].reshape(b, s, d)


def _odd_layer(x, h, gate, li, w_in, q_lat_norm, kv_lat_norm, w_uq, w_ukv, q_norm, k_norm, w_out,
               rc, rs, mla_bias):
    b, s, d = x.shape
    m = b * s
    q_lora = q_lat_norm.shape[0]
    kv_lora = kv_lat_norm.shape[0]
    n_heads = w_uq.shape[1] // QK_HEAD
    lat_raw = q_lora + kv_lora + QK_ROPE
    n_lat = lat_raw + QK_ROPE
    assert n_lat % LANES == 0 and 2 * QK_ROPE == LANES
    h2 = h.reshape(m, d)
    tm, tk = _tile(s, XS_TILE), _tile(d, XS_TILE)
    first = (jnp.arange(LANES) < QK_ROPE).astype(F32).reshape(1, LANES)

    w_lat = jnp.concatenate([w_in[:, :lat_raw], w_in[:, lat_raw - QK_ROPE:lat_raw]], axis=1).astype(BF16)
    tm_lat = _tile(s, LATENT_TM)
    lat = _matmul(h2, w_lat, tm=tm_lat, tn=n_lat, tk=tk, extras=(), extra_specs=[],
                  out_shapes=[jax.ShapeDtypeStruct((m, n_lat), F32)],
                  out_specs=[pl.BlockSpec((tm_lat, n_lat), lambda i, j, k: (i, j))],
                  epilogue=_epi_plain, name="odd_latent_proj")[0]
    n_g = w_in.shape[1] - lat_raw
    tn_g = _tile(n_g, MM_TN)
    g = _matmul_ws([h2], w_in[:, lat_raw:].astype(BF16), tm=tm, tn=tn_g, extras=(), extra_specs=[],
                   out_shapes=[jax.ShapeDtypeStruct((m, n_g), BF16)],
                   out_specs=[pl.BlockSpec((tm, tn_g), lambda j, i: (i, j))],
                   epilogue=_epi_plain, name="odd_gate_proj")[0]

    w_uq3 = w_uq.reshape(q_lora, n_heads, QK_HEAD)
    w_uq_p = jnp.concatenate([w_uq3, w_uq3[:, :, QK_NOPE:]], axis=-1)
    w_uq_p = w_uq_p.reshape(q_lora, n_heads * MLA_QK_PAD).astype(BF16)
    qw = (jnp.concatenate([q_norm, q_norm[QK_NOPE:]]) * (LOG2_E / math.sqrt(QK_HEAD))).reshape(1, MLA_QK_PAD)
    tn_q = _tile(n_heads * MLA_QK_PAD, XS_TILE)
    rope_specs = [pl.BlockSpec((1, LANES), lambda i, j, k: (0, 0))] + \
        [pl.BlockSpec((tm, LANES), lambda i, j, k: (i, 0))] * 2
    q = _matmul(
        lat, w_uq_p, tm=tm, tn=tn_q, tk=q_lora,
        extras=(q_lat_norm.reshape(1, q_lora), qw, first, rc, rs),
        extra_specs=[pl.BlockSpec((1, q_lora), lambda i, j, k: (0, 0)),
                     pl.BlockSpec((1, MLA_QK_PAD), lambda i, j, k: (0, 0))] + rope_specs,
        out_shapes=[jax.ShapeDtypeStruct((m, n_heads * MLA_QK_PAD), BF16)],
        out_specs=[pl.BlockSpec((tm, tn_q), lambda i, j, k: (i, j))],
        epilogue=_epi_mla_q,
        prologue=_pro_rms, name="mla_q_proj")[0]

    assert q_lora % kv_lora == 0 and (q_lora + kv_lora) % LANES == 0
    kvw = QK_NOPE + V_HEAD
    tn_kv = _tile(n_heads * kvw, XS_TILE)
    heads_per_tile = tn_kv // kvw
    kwn = k_norm[:QK_NOPE].reshape(1, QK_NOPE)
    kwp = jnp.concatenate([k_norm[QK_NOPE:], k_norm[QK_NOPE:]]).reshape(1, LANES)
    kpe_block = (q_lora + kv_lora) // LANES
    k, v = _matmul(
        lat, w_ukv.astype(BF16), tm=tm, tn=tn_kv, tk=kv_lora,
        extras=(kv_lat_norm.reshape(1, kv_lora), lat, kwn, kwp, first, rc, rs),
        extra_specs=[pl.BlockSpec((1, kv_lora), lambda i, j, k: (0, 0)),
                     pl.BlockSpec((tm, LANES), lambda i, j, k: (i, kpe_block)),
                     pl.BlockSpec((1, QK_NOPE), lambda i, j, k: (0, 0)),
                     pl.BlockSpec((1, LANES), lambda i, j, k: (0, 0))] + rope_specs,
        out_shapes=[jax.ShapeDtypeStruct((m, n_heads * MLA_QK_PAD), BF16),
                    jax.ShapeDtypeStruct((m, n_heads * V_HEAD), BF16)],
        out_specs=[pl.BlockSpec((tm, heads_per_tile * MLA_QK_PAD), lambda i, j, k: (i, j)),
                   pl.BlockSpec((tm, heads_per_tile * V_HEAD), lambda i, j, k: (i, j))],
        epilogue=_epi_mla_kv,
        prologue=_pro_rms, x_col_block=q_lora // kv_lora, name="mla_kv_proj")
    mixed = _attn_call(
        functools.partial(_mla_kernel, tile=MLA_TILE),
        (q.reshape(b, s, -1), k.reshape(b, s, -1), v.reshape(b, s, -1), g.reshape(b, s, -1)),
        (0, 0, 0, 0), (MLA_QK_PAD, MLA_QK_PAD, V_HEAD, V_HEAD), (mla_bias,), n_heads * V_HEAD, 0,
        n_heads, s,
        [pltpu.VMEM((MLA_TILE, MLA_TILE), F32), pltpu.VMEM((MLA_TILE, MLA_TILE), F32),
         pltpu.VMEM((MLA_TILE, MLA_TILE), BF16), pltpu.VMEM((MLA_TILE, MLA_TILE), BF16),
         pltpu.VMEM((s, V_HEAD), F32), pltpu.VMEM((s, LANES), F32),
         pltpu.VMEM((s, LANES), F32)], "mla_attention")
    return _out_proj(x, [mixed.reshape(m, -1)], w_out, li, gate)


def kernel(x, c, positions, ada_w, ada_b, norm_w, ev_w_in, ev_q_norm, ev_k_norm, ev_w_out, od_w_in,
           od_q_lat_norm, od_kv_lat_norm, od_w_uq, od_w_ukv, od_q_norm, od_k_norm, od_w_out):
    b, s, d = x.shape
    depth = ada_w.shape[0]
    m = b * s

    cos_f, sin_f = _rope_angles(positions, HEAD_DIM)
    cosf = jnp.concatenate([cos_f, cos_f], axis=-1).reshape(m, HEAD_DIM)
    sinf = jnp.concatenate([-sin_f, sin_f], axis=-1).reshape(m, HEAD_DIM)
    cos_m, sin_m = _rope_angles(positions, QK_ROPE)
    z32 = jnp.zeros_like(cos_m)
    z64 = jnp.concatenate([z32, z32], axis=-1)
    rc = jnp.concatenate([cos_m, cos_m, z64], axis=-1).reshape(m, LANES)
    rs = jnp.concatenate([-sin_m, sin_m, z64], axis=-1).reshape(m, LANES)

    sb_u = jnp.asarray(np.tril(np.ones((SB_TILE, SB_TILE), np.float32), -1), BF16)
    dl_bias = jnp.asarray(_dl_band_bias())
    causal = np.tril(np.ones((MLA_TILE, MLA_TILE), bool))
    mla_bias = jnp.asarray(np.where(causal, 0.0, -np.inf).astype(np.float32))

    mod = _modulation(c, ada_w, ada_b)
    shift, scale, gate = (mod[:, :, i * d:(i + 1) * d].reshape(depth, b, 1, d) for i in range(3))

    for layer in range(depth):
        h = _norm_mod(x, norm_w[layer], scale[layer], shift[layer])
        i = layer // 2
        if layer % 2 == 0:
            x = _even_layer(x, h, gate[layer], i, ev_w_in, ev_q_norm[i], ev_k_norm[i], ev_w_out,
                            cosf, sinf, sb_u, dl_bias)
        else:
            x = _odd_layer(x, h, gate[layer], i, od_w_in[i], od_q_lat_norm[i], od_kv_lat_norm[i],
                           od_w_uq[i], od_w_ukv[i], od_q_norm[i], od_k_norm[i], od_w_out,
                           rc, rs, mla_bias)
    return x
```

```python
import functools
import math

import numpy as np
import jax
import jax.numpy as jnp
from jax import lax
from jax.experimental import pallas as pl
from jax.experimental.pallas import tpu as pltpu

F32 = jnp.float32
BF16 = jnp.bfloat16

HEAD_DIM = 128
DIL_PATTERNS = ((128, 1), (512, 4), (2048, 16))
QK_NOPE = 128
QK_ROPE = 64
QK_HEAD = QK_NOPE + QK_ROPE
V_HEAD = 128
MLA_QK_PAD = 256
ROPE_THETA = 10000.0
EPS = 1e-6

LANES = 128
SUBLANES = 8
MXU_COLS = 256
VMEM_LIMIT = 61 * 1024 * 1024
EXP_UNDERFLOW = -104.0
LOG2_E = math.log2(math.e)
SB_TILE = 256
SB_SUBTILES = 8
DL_TILE = 128
DL_CHUNK = DL_TILE * max(d for _, d in DIL_PATTERNS)
MLA_TILE = 512
MLA_ROW_BLOCK = 64
MM_TM = 512
MM_TN = 1024
XS_TILE = 1024
LATENT_TM = 512
MOD_TN = 1024
MOD_SIDE_TN = 768
NORM_ROWS = 512
NORM_ROW_BLOCK = 16


def _cparams(sem):
    return pltpu.CompilerParams(dimension_semantics=sem, vmem_limit_bytes=VMEM_LIMIT)


def _tile(n, want):
    if n <= want:
        return n
    t = (want // LANES) * LANES
    while n % t:
        t -= LANES
    return t


def _mod_kernel(c_ref, w_ref, b_ref, o_ref):
    a = c_ref[...]
    a = (a * jax.nn.sigmoid(a)).astype(BF16)
    o_ref[...] = jnp.dot(a, w_ref[...].astype(BF16), preferred_element_type=F32) + b_ref[...]


def _pad_rows(c):
    b = c.shape[0]
    return jnp.pad(c, ((0, -b % SUBLANES), (0, 0)))


def _modulation(c, ada_w, ada_b, depth):
    _, d, n3 = ada_w.shape
    b = c.shape[0]
    c_pad = _pad_rows(c)
    rows = c_pad.shape[0]
    tn = _tile(n3, MOD_TN)
    out = pl.pallas_call(
        _mod_kernel,
        grid=(depth, n3 // tn),
        in_specs=[pl.BlockSpec((rows, d), lambda l, j: (0, 0)),
                  pl.BlockSpec((None, d, tn), lambda l, j: (l, 0, j)),
                  pl.BlockSpec((None, 1, tn), lambda l, j: (l, 0, j))],
        out_specs=pl.BlockSpec((None, rows, tn), lambda l, j: (l, 0, j)),
        out_shape=jax.ShapeDtypeStruct((depth, rows, n3), F32),
        compiler_params=_cparams(("parallel", "parallel")),
        name="modulation",
    )(c_pad, ada_w, ada_b.reshape(-1, 1, n3))
    return out[:, :b]


def _norm_mod_kernel(x_ref, w_ref, scale_ref, shift_ref, o_ref):
    gain = w_ref[...] * (1.0 + scale_ref[...])

    def body(r, _):
        rows = pl.ds(pl.multiple_of(r * NORM_ROW_BLOCK, NORM_ROW_BLOCK), NORM_ROW_BLOCK)
        x = x_ref[rows, :]
        rstd = lax.rsqrt(jnp.mean(x * x, axis=-1, keepdims=True) + EPS)
        o_ref[rows, :] = (x_ref[rows, :] * rstd * gain + shift_ref[...]).astype(o_ref.dtype)
        return 0

    lax.fori_loop(0, x_ref.shape[0] // NORM_ROW_BLOCK, body, 0, unroll=2)


def _norm_mod(x, w, scale, shift):
    b, s, d = x.shape
    ts = _tile(s, NORM_ROWS)
    return pl.pallas_call(
        _norm_mod_kernel,
        grid=(b, s // ts),
        in_specs=[pl.BlockSpec((None, ts, d), lambda bi, i: (bi, i, 0)),
                  pl.BlockSpec((1, d), lambda bi, i: (0, 0)),
                  pl.BlockSpec((None, 1, d), lambda bi, i: (bi, 0, 0)),
                  pl.BlockSpec((None, 1, d), lambda bi, i: (bi, 0, 0))],
        out_specs=pl.BlockSpec((None, ts, d), lambda bi, i: (bi, i, 0)),
        out_shape=jax.ShapeDtypeStruct((b, s, d), BF16),
        compiler_params=_cparams(("parallel", "parallel")),
        name="norm_mod",
    )(x, w.reshape(1, d), scale, shift)


def _run_epilogue(variants, chunk_of, tn, acc_ref):
    chunks = [(c0, min(MXU_COLS, tn - c0)) for c0 in range(0, tn, MXU_COLS)]

    def fused(fn):
        for c0, w in chunks:
            fn(chunk_of(c0, w), c0)

    def matmul_only():
        acc_ref[...] = chunk_of(0, tn)

    def epilogue_only(fn):
        for c0, w in chunks:
            fn(acc_ref.at[:, c0:c0 + w], c0)

    split = [(cond, fn) for cond, fn in variants if getattr(fn, "via_ref", False) and cond is not None]
    if split:
        pl.when(functools.reduce(jnp.logical_or, [cond for cond, _ in split]))(matmul_only)
    for cond, fn in variants:
        if getattr(fn, "via_ref", False):
            if cond is None:
                matmul_only()
                epilogue_only(fn)
            else:
                pl.when(cond)(functools.partial(epilogue_only, fn))
        elif cond is None:
            fused(fn)
        else:
            pl.when(cond)(functools.partial(fused, fn))


def _via_ref(fn):
    fn.via_ref = True
    return fn


def _mm_kernel(*refs, n_extra, n_out, nk, tn, epilogue, prologue):
    x_ref, w_ref = refs[0], refs[1]
    extras = refs[2:2 + n_extra]
    outs = refs[2 + n_extra:2 + n_extra + n_out]
    scratch = refs[2 + n_extra + n_out:]
    acc_ref = scratch[0]
    if prologue is not None:
        h_ref = scratch[1]

        @pl.when(pl.program_id(1) == 0)
        def _():
            h_ref[...] = prologue(x_ref, extras)

        _run_epilogue(epilogue(extras, outs),
                      lambda c0, w: jnp.dot(h_ref[...], w_ref[:, c0:c0 + w], preferred_element_type=F32),
                      tn, acc_ref)
        return
    k = pl.program_id(2)

    @pl.when(k == 0)
    def _():
        acc_ref[...] = jnp.zeros_like(acc_ref)

    acc_ref[...] += jnp.dot(x_ref[...], w_ref[...], preferred_element_type=F32)

    @pl.when(k == nk - 1)
    def _():
        _run_epilogue(epilogue(extras, outs), lambda c0, w: acc_ref[:, c0:c0 + w], tn, acc_ref)


def _matmul(x, w, *, tm, tn, tk, extras, extra_specs, out_shapes, out_specs, epilogue,
            prologue=None, x_col_block=0, name):
    m = x.shape[0]
    kdim, n = w.shape
    nk = kdim // tk
    assert m % tm == 0 and n % tn == 0 and kdim % tk == 0 and x.shape[1] >= (x_col_block + nk) * tk
    scratch = [pltpu.VMEM((tm, tn), F32)]
    if prologue is not None:
        assert nk == 1
        scratch.append(pltpu.VMEM((tm, kdim), BF16))
    kern = functools.partial(_mm_kernel, n_extra=len(extras), n_out=len(out_shapes), nk=nk, tn=tn,
                             epilogue=epilogue, prologue=prologue)
    return pl.pallas_call(
        kern,
        grid=(m // tm, n // tn, nk),
        in_specs=[pl.BlockSpec((tm, tk), lambda i, j, k: (i, x_col_block + k)),
                  pl.BlockSpec((tk, tn), lambda i, j, k: (k, j))] + list(extra_specs),
        out_specs=list(out_specs),
        out_shape=list(out_shapes),
        scratch_shapes=scratch,
        compiler_params=_cparams(("parallel", "arbitrary", "arbitrary")),
        name=name,
    )(x, w, *extras)


def _mm_ws_kernel(*refs, k_parts, n_extra, n_out, tn, cast, epilogue):
    n_x = len(k_parts)
    x_refs = refs[:n_x]
    w_ref = refs[n_x]
    extras = refs[n_x + 1:n_x + 1 + n_extra]
    outs = refs[n_x + 1 + n_extra:n_x + 1 + n_extra + n_out]
    scratch = refs[n_x + 1 + n_extra + n_out:]
    acc_ref = scratch[0]
    if cast:
        wb_ref = scratch[1]

        @pl.when(pl.program_id(1) == 0)
        def _():
            wb_ref[...] = w_ref[...].astype(BF16)
    else:
        wb_ref = w_ref

    def chunk_of(c0, w):
        acc = None
        k0 = 0
        for x_ref, kp in zip(x_refs, k_parts):
            part = jnp.dot(x_ref[...], wb_ref[k0:k0 + kp, c0:c0 + w], preferred_element_type=F32)
            acc = part if acc is None else acc + part
            k0 += kp
        return acc

    _run_epilogue(epilogue(extras, outs), chunk_of, tn, acc_ref)


def _matmul_ws(xs, w, *, tm, tn, extras, extra_specs, out_shapes, out_specs, epilogue, name, layer=None):
    m = xs[0].shape[0]
    kdim, n = w.shape[-2:]
    if layer is None:
        w_spec = pl.BlockSpec((kdim, tn), lambda j, i: (0, j))
    else:
        w_spec = pl.BlockSpec((None, kdim, tn), lambda j, i: (layer, 0, j))
    k_parts = tuple(x.shape[1] for x in xs)
    assert m % tm == 0 and n % tn == 0 and sum(k_parts) == kdim
    cast = w.dtype != BF16
    scratch = [pltpu.VMEM((tm, tn), F32)]
    if cast:
        scratch.append(pltpu.VMEM((kdim, tn), BF16))
    kern = functools.partial(_mm_ws_kernel, k_parts=k_parts, n_extra=len(extras),
                             n_out=len(out_shapes), tn=tn, cast=cast, epilogue=epilogue)
    x_specs = [pl.BlockSpec((tm, kp), lambda j, i: (i, 0)) for kp in k_parts]
    return pl.pallas_call(
        kern,
        grid=(n // tn, m // tm),
        in_specs=x_specs + [w_spec] + list(extra_specs),
        out_specs=list(out_specs),
        out_shape=list(out_shapes),
        scratch_shapes=scratch,
        compiler_params=_cparams(("parallel", "arbitrary")),
        name=name,
    )(*xs, w, *extras)


def _epi_plain(extras, outs):
    o_ref = outs[0]

    def fn(a, c0):
        o_ref[:, c0:c0 + a.shape[1]] = a.astype(o_ref.dtype)

    return [(None, fn)]


def _epi_resid(extras, outs):
    xres_ref, gate_ref = extras
    o_ref = outs[0]

    def fn(a, c0):
        sl = slice(c0, c0 + a.shape[1])
        o_ref[:, sl] = xres_ref[:, sl] + gate_ref[:, sl] * a

    return [(None, fn)]


def _rope_full(y, cos_ref, sin_ref):
    return y * cos_ref[...] + pltpu.roll(y, HEAD_DIM // 2, 1) * sin_ref[...]


def _rope_mla(r, c_ref, s_ref):
    return r * c_ref[...] + pltpu.roll(r, QK_ROPE // 2, 1) * s_ref[...]


def _epi_even(extras, outs, *, part_tiles, q_scale):
    qn_ref, kn_ref, cos_ref, sin_ref = extras
    o_ref = outs[0]
    part = pl.program_id(0) // part_tiles

    def scaled(scale):
        def fn(a, c0):
            y = a if scale == 1.0 else a * scale
            o_ref[:, c0:c0 + a.shape[1]] = y.astype(o_ref.dtype)
        return fn

    def headnorm_rope(w_ref, scale):
        @_via_ref
        def fn(a, c0):
            for hh in range(a.shape[1] // HEAD_DIM):
                ah = a[:, hh * HEAD_DIM:(hh + 1) * HEAD_DIM]
                y = ah * lax.rsqrt(jnp.mean(ah * ah, axis=-1, keepdims=True) + EPS) * w_ref[...]
                y = _rope_full(y, cos_ref, sin_ref)
                if scale != 1.0:
                    y = y * scale
                o_ref[:, c0 + hh * HEAD_DIM:c0 + (hh + 1) * HEAD_DIM] = y.astype(o_ref.dtype)
        return fn

    return [(part == 0, scaled(q_scale)),
            (part == 4, headnorm_rope(qn_ref, q_scale * LOG2_E)),
            (part == 5, headnorm_rope(kn_ref, 1.0)),
            ((part != 0) & (part != 4) & (part != 5), scaled(1.0))]


def _pro_rms(x_ref, extras):
    x = x_ref[...]
    w_ref = extras[0]
    y = x * lax.rsqrt(jnp.mean(x * x, axis=-1, keepdims=True) + EPS) * w_ref[...]
    return y.astype(BF16)


def _epi_mla_q(extras, outs):
    _, qw_ref, first_ref, c_ref, s_ref = extras
    o_ref = outs[0]
    assert MXU_COLS == MLA_QK_PAD

    @_via_ref
    def fn(a, c0):
        nope, rope = a[:, :QK_NOPE], a[:, QK_NOPE:]
        sq = nope * nope + rope * rope * first_ref[...]
        rstd = lax.rsqrt(jnp.sum(sq, axis=-1, keepdims=True) * (1.0 / QK_HEAD) + EPS)
        o_ref[:, c0:c0 + QK_NOPE] = (nope * rstd * qw_ref[:, :QK_NOPE]).astype(o_ref.dtype)
        r = _rope_mla(rope * rstd * qw_ref[:, QK_NOPE:], c_ref, s_ref)
        o_ref[:, c0 + QK_NOPE:c0 + MLA_QK_PAD] = r.astype(o_ref.dtype)

    return [(None, fn)]


def _epi_mla_kv(extras, outs):
    _, kpe_ref, kwn_ref, kwp_ref, first_ref, c_ref, s_ref = extras
    k_ref, v_ref = outs
    assert MXU_COLS == QK_NOPE + V_HEAD
    kpe = kpe_ref[...]
    ss_pe = jnp.sum(kpe * kpe * first_ref[...], axis=-1, keepdims=True)
    kr = _rope_mla(kpe * kwp_ref[...], c_ref, s_ref)

    @_via_ref
    def fn(a, c0):
        head = c0 // (QK_NOPE + V_HEAD)
        kn = a[:, :QK_NOPE]
        rstd = lax.rsqrt((jnp.sum(kn * kn, axis=-1, keepdims=True) + ss_pe) * (1.0 / QK_HEAD) + EPS)
        kb = head * MLA_QK_PAD
        k_ref[:, kb:kb + QK_NOPE] = (kn * rstd * kwn_ref[...]).astype(k_ref.dtype)
        k_ref[:, kb + QK_NOPE:kb + MLA_QK_PAD] = (kr * rstd).astype(k_ref.dtype)
        v_ref[:, head * V_HEAD:(head + 1) * V_HEAD] = a[:, QK_NOPE:].astype(v_ref.dtype)

    return [(None, fn)]


def _qk(q, k):
    return lax.dot_general(q, k, (((1,), (1,)), ((), ())), preferred_element_type=F32)


def _sb_stages(q_ref, k_ref, v_ref, g_ref, u_ref, o_ref, acc_ref, carry_ref, qi, tile):
    n_sub = q_ref.shape[0] // tile

    def tile_rows(kb):
        return pl.ds(pl.multiple_of(kb * tile, tile), tile)

    def weights(q, kb, carry, diagonal=False, valid=None):
        z = _qk(q, k_ref[tile_rows(kb), :])
        sp = jnp.log(1.0 + jnp.exp(-jnp.abs(z)))
        log_beta = jnp.minimum(z, 0.0) - sp
        log_keep = jnp.minimum(-z, 0.0) - sp
        keep = valid
        if diagonal:
            row = lax.broadcasted_iota(jnp.int32, z.shape, 0)
            col = lax.broadcasted_iota(jnp.int32, z.shape, 1)
            keep = col < row
        if keep is not None:
            log_keep = jnp.where(keep, log_keep, 0.0)
        hi = log_keep.astype(BF16)
        lo = (log_keep - hi.astype(F32)).astype(BF16)
        u = u_ref[...]
        excl = jnp.dot(hi, u, preferred_element_type=F32) + jnp.dot(lo, u, preferred_element_type=F32)
        if carry is not None:
            excl = excl + carry
        w = jnp.exp(log_beta + excl)
        if keep is not None:
            w = jnp.where(keep, w, 0.0)
        return w.astype(BF16), jnp.sum(log_keep, axis=-1, keepdims=True)

    def pv(w, kb):
        return jnp.dot(w, v_ref[tile_rows(kb), :], preferred_element_type=F32)

    def front():
        tails = []
        for sub in range(n_sub):
            rows = slice(sub * tile, (sub + 1) * tile)
            q = q_ref[rows, :]
            diag = qi * n_sub + sub
            prev = jnp.maximum(diag - 1, 0)
            w_d, sum_d = weights(q, diag, None, diagonal=True)
            w_p, sum_p = weights(q, prev, sum_d, valid=diag > 0)
            carry = sum_d + sum_p
            acc_ref[rows, :] = pv(w_d, diag) + pv(w_p, prev)
            carry_ref[rows, :] = carry
            tails.append((rows, diag - 2, jnp.max(carry)))
        return tails

    def tail(tails):
        for rows, kb0, cmax0 in tails:
            def cond(state):
                kb, cmax = state
                return (kb >= 0) & (cmax > EXP_UNDERFLOW)

            def body(state, rows=rows):
                kb, _ = state
                carry = carry_ref[rows, :]
                w, sum_k = weights(q_ref[rows, :], kb, carry)
                acc_ref[rows, :] += pv(w, kb)
                carry = carry + sum_k
                carry_ref[rows, :] = carry
                return kb - 1, jnp.max(carry)

            lax.while_loop(cond, body, (kb0, cmax0))

    def finish():
        g = g_ref[...].astype(F32)
        o_ref[...] = (acc_ref[...] * (g * jax.nn.sigmoid(g))).astype(o_ref.dtype)

    return front, tail, finish


def _dl_stages(q_ref, k_ref, v_ref, g_ref, bias_ref, o_ref, qf_ref, kf_ref, vf_ref, acc_ref, m_ref, l_ref,
               ci, chunk):
    def prepare():
        @pl.when(ci == 0)
        def _():
            kf_ref[...] = k_ref[...].astype(F32)
            vf_ref[...] = v_ref[...].astype(F32)

    base = ci * chunk

    def body(idx, _):
        for pi, (_, dil) in enumerate(DIL_PATTERNS):
            unit = DL_TILE * dil
            q0 = (idx // dil) * unit + idx % dil
            rows = pl.ds(q0, DL_TILE, stride=dil)
            cur0 = base + q0
            has_prev = cur0 >= unit
            prev0 = jnp.maximum(cur0 - unit, 0)
            prev_rows = pl.ds(prev0, DL_TILE, stride=dil)
            cur_rows = pl.ds(cur0, DL_TILE, stride=dil)
            kk = jnp.concatenate([kf_ref[prev_rows, :], kf_ref[cur_rows, :]], axis=0).astype(BF16)
            vv = jnp.concatenate([vf_ref[prev_rows, :], vf_ref[cur_rows, :]], axis=0).astype(BF16)
            s = _qk(qf_ref[rows, :].astype(BF16), kk) + bias_ref[pi, has_prev.astype(jnp.int32)]
            m = jnp.max(s, axis=-1, keepdims=True)
            p = jnp.exp2(s - m)
            l = jnp.sum(p, axis=-1, keepdims=True)
            acc_ref[pi, rows, :] = jnp.dot(p.astype(BF16), vv, preferred_element_type=F32)
            m_ref[pi, rows, :] = jnp.broadcast_to(m, (DL_TILE, LANES))
            l_ref[pi, rows, :] = jnp.broadcast_to(l, (DL_TILE, LANES))
        return 0

    def front():
        qf_ref[...] = q_ref[...].astype(F32)
        lax.fori_loop(0, chunk // DL_TILE, body, 0, unroll=True)

    def finish():
        n_pat = len(DIL_PATTERNS)
        m_all = functools.reduce(jnp.maximum, [m_ref[pi] for pi in range(n_pat)])
        num = jnp.zeros(acc_ref.shape[1:], F32)
        den = jnp.zeros(acc_ref.shape[1:], F32)
        for pi in range(n_pat):
            w = jnp.exp2(m_ref[pi] - m_all)
            num = num + w * acc_ref[pi]
            den = den + w * l_ref[pi]
        g = g_ref[...].astype(F32)
        o_ref[...] = (num / den * (g * jax.nn.sigmoid(g))).astype(o_ref.dtype)

    return prepare, front, finish


def _even_attn_kernel(sq_ref, sk_ref, sv_ref, sg_ref, dq_ref, dk_ref, dv_ref, dg_ref, u_ref, bias_ref,
                      *rest, sb_tile, chunk, with_mod):
    if with_mod:
        c_ref, adaw_ref, adab_ref, o_sb_ref, o_dl_ref, mod_ref = rest[:6]
        rest = rest[6:]
    else:
        o_sb_ref, o_dl_ref = rest[:2]
        rest = rest[2:]
    sb_acc_ref, sb_carry_ref, qf_ref, kf_ref, vf_ref, dl_acc_ref, dl_m_ref, dl_l_ref = rest
    ci = pl.program_id(2)
    if with_mod:
        _mod_kernel(c_ref, adaw_ref, adab_ref, mod_ref)
    sb_front, sb_tail, sb_finish = _sb_stages(sq_ref, sk_ref, sv_ref, sg_ref, u_ref, o_sb_ref, sb_acc_ref,
                                              sb_carry_ref, ci, sb_tile)
    dl_prepare, dl_front, dl_finish = _dl_stages(dq_ref, dk_ref, dv_ref, dg_ref, bias_ref, o_dl_ref, qf_ref,
                                                 kf_ref, vf_ref, dl_acc_ref, dl_m_ref, dl_l_ref, ci, chunk)
    dl_prepare()
    tails = sb_front()
    dl_front()
    sb_tail(tails)
    sb_finish()
    dl_finish()


def _softmax_update(s_ref, bias_ref, v, p_ref, acc_ref, m_ref, l_ref):
    tq, tk = s_ref.shape
    for r0 in range(0, tq, MLA_ROW_BLOCK):
        rs = slice(r0, r0 + MLA_ROW_BLOCK)
        cols = [s_ref[rs, c * LANES:(c + 1) * LANES] for c in range(tk // LANES)]
        if bias_ref is not None:
            cols = [col + bias_ref[rs, c * LANES:(c + 1) * LANES] for c, col in enumerate(cols)]
        m_old = m_ref[rs, :]
        tile_max = jnp.max(functools.reduce(jnp.maximum, cols), axis=-1, keepdims=True)
        m_new = jnp.maximum(m_old, tile_max)
        ps = [jnp.exp2(col - m_new) for col in cols]
        alpha = jnp.exp2(m_old - m_new)
        l_ref[rs, :] = alpha * l_ref[rs, :] + jnp.sum(functools.reduce(jnp.add, ps), axis=-1, keepdims=True)
        m_ref[rs, :] = m_new
        acc_ref[rs, :] = alpha * acc_ref[rs, :]
        for c, x in enumerate(ps):
            p_ref[rs, c * LANES:(c + 1) * LANES] = x.astype(BF16)
    acc_ref[...] += jnp.dot(p_ref[...], v, preferred_element_type=F32)


def _mla_kernel(q_ref, k_ref, v_ref, g_ref, bias_ref, o_ref, sa_ref, sb_ref, pa_ref, pb_ref, acc_ref, m_ref,
                l_ref, *, tile):
    n_q = q_ref.shape[0] // tile
    acc_ref[...] = jnp.zeros_like(acc_ref)
    m_ref[...] = jnp.full_like(m_ref, -jnp.inf)
    l_ref[...] = jnp.zeros_like(l_ref)

    def rows(t):
        return pl.ds(pl.multiple_of(t * tile, tile), tile)

    def scores(qi, kb, dst_ref):
        dst_ref[...] = _qk(q_ref[rows(qi), :], k_ref[rows(kb), :])

    def pipeline(n_steps, start, advance, bias):
        def update(qi, kb, src_ref, p_ref):
            r = rows(qi)
            _softmax_update(src_ref, bias, v_ref[rows(kb), :], p_ref, acc_ref.at[r, :], m_ref.at[r, :],
                            l_ref.at[r, :])

        scores(*start, sa_ref)

        def pair(_, state):
            qa, ka = state
            qb, kb = advance(qa, ka)
            scores(qb, kb, sb_ref)
            update(qa, ka, sa_ref, pa_ref)
            qa, ka = advance(qb, kb)
            scores(qa, ka, sa_ref)
            update(qb, kb, sb_ref, pb_ref)
            return qa, ka

        qa, ka = lax.fori_loop(0, n_steps // 2, pair, start, unroll=4)
        if n_steps % 2:
            update(qa, ka, sa_ref, pa_ref)

    def next_lower(qi, kb):
        wrap = kb == qi - 1
        return jnp.where(wrap, jnp.minimum(qi + 1, n_q - 1), qi), jnp.where(wrap, 0, kb + 1)

    def next_diagonal(qi, kb):
        t = jnp.minimum(qi + 1, n_q - 1)
        return t, t

    zero, one = jnp.int32(0), jnp.int32(1)
    if n_q > 1:
        pipeline(n_q * (n_q - 1) // 2, (one, zero), next_lower, None)
    pipeline(n_q, (zero, zero), next_diagonal, bias_ref)

    g = g_ref[...].astype(F32)
    o_ref[...] = (acc_ref[...] * g / (l_ref[...] * (1.0 + jnp.exp(-g)))).astype(o_ref.dtype)


def _attn_call(kern, arrays, col_blocks, widths, consts, out_width_total, out_col0, n_heads, tile,
               scratch, name):
    q, k, v, g = arrays
    b, s, _ = q.shape
    cq, ck, cv, cg = col_blocks
    wq, wk, wv, wg = widths
    in_specs = [
        pl.BlockSpec((None, tile, wq), lambda bi, h, i: (bi, i, cq + h)),
        pl.BlockSpec((None, s, wk), lambda bi, h, i: (bi, 0, ck + h)),
        pl.BlockSpec((None, s, wv), lambda bi, h, i: (bi, 0, cv + h)),
        pl.BlockSpec((None, tile, wg), lambda bi, h, i: (bi, i, cg + h)),
    ]
    for cst in consts:
        in_specs.append(pl.BlockSpec(cst.shape, lambda bi, h, i, nd=cst.ndim: (0,) * nd))
    return pl.pallas_call(
        kern,
        grid=(b, n_heads, s // tile),
        in_specs=in_specs,
        out_specs=pl.BlockSpec((None, tile, wv), lambda bi, h, i: (bi, i, out_col0 + h)),
        out_shape=jax.ShapeDtypeStruct((b, s, out_width_total), BF16),
        scratch_shapes=scratch,
        compiler_params=_cparams(("parallel", "parallel", "arbitrary")),
        name=name,
    )(q, k, v, g, *consts)


def _dl_band_bias():
    i = np.arange(DL_TILE)[:, None]
    c = np.arange(2 * DL_TILE)[None, :]
    out = []
    for window, dil in DIL_PATTERNS:
        span = window // dil
        assert span <= DL_TILE
        dist = i + DL_TILE - c
        band = (dist >= 0) & (dist <= span)
        out.append(np.stack([band & (c >= DL_TILE), band]))
    return np.where(np.stack(out), 0.0, -np.inf).astype(np.float32)


def _rope_angles(positions, dim):
    inv = ROPE_THETA ** (-jnp.arange(0, dim, 2, dtype=F32) / dim)
    ang = positions.astype(F32)[..., None] * inv
    return jnp.cos(ang), jnp.sin(ang)


def _even_layer(x, h, gate, li, w_in, qn, kn, w_out, cosf, sinf, sb_u, dl_bias, mod_side=None):
    b, s, d = x.shape
    m = b * s
    n_in = w_in.shape[-1]
    w_part = n_in // 8
    n_heads = w_part // HEAD_DIM
    tm, tn = _tile(s, MM_TM), _tile(w_part, MM_TN)
    q_scale = 1.0 / math.sqrt(HEAD_DIM)
    proj = _matmul_ws(
        [h.reshape(m, d)], w_in, layer=li, tm=tm, tn=tn,
        extras=(qn.reshape(1, HEAD_DIM), kn.reshape(1, HEAD_DIM), cosf, sinf),
        extra_specs=[pl.BlockSpec((1, HEAD_DIM), lambda j, i: (0, 0)),
                     pl.BlockSpec((1, HEAD_DIM), lambda j, i: (0, 0)),
                     pl.BlockSpec((tm, HEAD_DIM), lambda j, i: (i, 0)),
                     pl.BlockSpec((tm, HEAD_DIM), lambda j, i: (i, 0))],
        out_shapes=[jax.ShapeDtypeStruct((m, n_in), BF16)],
        out_specs=[pl.BlockSpec((tm, tn), lambda j, i: (i, j))],
        epilogue=functools.partial(_epi_even, part_tiles=w_part // tn, q_scale=q_scale),
        name="even_in_proj")[0].reshape(b, s, n_in)

    assert s % DL_CHUNK == 0 and SB_TILE * SB_SUBTILES == DL_CHUNK
    n_pat = len(DIL_PATTERNS)

    def head_block(part, rows):
        if rows == s:
            return pl.BlockSpec((None, s, HEAD_DIM), lambda bi, h, i: (bi, 0, part * n_heads + h))
        return pl.BlockSpec((None, rows, HEAD_DIM), lambda bi, h, i: (bi, i, part * n_heads + h))

    part_rows = (DL_CHUNK, s, s, DL_CHUNK) * 2
    out_spec = pl.BlockSpec((None, DL_CHUNK, HEAD_DIM), lambda bi, h, i: (bi, i, h))
    n_chunks = s // DL_CHUNK
    inputs = [proj] * 8 + [sb_u, dl_bias]
    in_specs = [head_block(p, r) for p, r in enumerate(part_rows)] + [
        pl.BlockSpec(sb_u.shape, lambda bi, h, i: (0, 0)),
        pl.BlockSpec(dl_bias.shape, lambda bi, h, i: (0, 0, 0, 0))]
    out_specs = [out_spec, out_spec]
    out_shapes = [jax.ShapeDtypeStruct((b, s, w_part), BF16)] * 2
    if mod_side is not None:
        c_pad, ada_w, ada_b, first_layer = mod_side
        n_layers, _, n3 = ada_w.shape
        tn_mod = _tile(n3, MOD_SIDE_TN)
        per_layer = n3 // tn_mod
        n_blocks = (n_layers - first_layer) * per_layer
        assert n_blocks <= b * n_heads * n_chunks

        def mod_block(bi, h, i):
            blk = jnp.minimum((bi * n_heads + h) * n_chunks + i, n_blocks - 1)
            return blk // per_layer, blk % per_layer

        inputs += [c_pad, ada_w, ada_b.reshape(n_layers, 1, n3)]
        in_specs += [
            pl.BlockSpec(c_pad.shape, lambda bi, h, i: (0, 0)),
            pl.BlockSpec((None, d, tn_mod), lambda bi, h, i: (first_layer + mod_block(bi, h, i)[0], 0,
                                                              mod_block(bi, h, i)[1])),
            pl.BlockSpec((None, 1, tn_mod), lambda bi, h, i: (first_layer + mod_block(bi, h, i)[0], 0,
                                                              mod_block(bi, h, i)[1]))]
        out_specs.append(pl.BlockSpec((None, c_pad.shape[0], tn_mod),
                                      lambda bi, h, i: (mod_block(bi, h, i)[0], 0, mod_block(bi, h, i)[1])))
        out_shapes.append(jax.ShapeDtypeStruct((n_layers - first_layer, c_pad.shape[0], n3), F32))
    outs = pl.pallas_call(
        functools.partial(_even_attn_kernel, sb_tile=SB_TILE, chunk=DL_CHUNK, with_mod=mod_side is not None),
        grid=(b, n_heads, n_chunks),
        in_specs=in_specs,
        out_specs=out_specs,
        out_shape=out_shapes,
        scratch_shapes=[pltpu.VMEM((DL_CHUNK, HEAD_DIM), F32), pltpu.VMEM((DL_CHUNK, 1), F32),
                        pltpu.VMEM((DL_CHUNK, HEAD_DIM), F32), pltpu.VMEM((s, HEAD_DIM), F32),
                        pltpu.VMEM((s, HEAD_DIM), F32), pltpu.VMEM((n_pat, DL_CHUNK, HEAD_DIM), F32),
                        pltpu.VMEM((n_pat, DL_CHUNK, LANES), F32), pltpu.VMEM((n_pat, DL_CHUNK, LANES), F32)],
        compiler_params=_cparams(("arbitrary",) * 3 if mod_side is not None
                                 else ("parallel", "parallel", "arbitrary")),
        name="even_attention",
    )(*inputs)
    o_sb, o_dl = outs[:2]
    mod_rest = outs[2] if mod_side is not None else None
    return _out_proj(x, [o_sb.reshape(m, w_part), o_dl.reshape(m, w_part)], w_out, li, gate), mod_rest


def _out_proj(x, mixed_parts, w_out, li, gate):
    b, s, d = x.shape
    m = b * s
    tm, tn = _tile(s, MM_TM), _tile(d, MM_TN)
    rows_per_batch = s // tm
    return _matmul_ws(
        mixed_parts, w_out, layer=li, tm=tm, tn=tn,
        extras=(x.reshape(m, d), gate),
        extra_specs=[pl.BlockSpec((tm, tn), lambda j, i: (i, j)),
                     pl.BlockSpec((None, 1, tn), lambda j, i: (i // rows_per_batch, 0, j))],
        out_shapes=[jax.ShapeDtypeStruct((m, d), F32)],
        out_specs=[pl.BlockSpec((tm, tn), lambda j, i: (i, j))],
        epilogue=_epi_resid, name="out_proj")[0].reshape(b, s, d)


def _odd_layer(x, h, gate, li, w_in, q_lat_norm, kv_lat_norm, w_uq, w_ukv, q_norm, k_norm, w_out,
               rc, rs, mla_bias):
    b, s, d = x.shape
    m = b * s
    q_lora = q_lat_norm.shape[0]
    kv_lora = kv_lat_norm.shape[0]
    n_heads = w_uq.shape[1] // QK_HEAD
    lat_raw = q_lora + kv_lora + QK_ROPE
    n_lat = lat_raw + QK_ROPE
    assert n_lat % LANES == 0 and 2 * QK_ROPE == LANES
    h2 = h.reshape(m, d)
    tm, tk = _tile(s, XS_TILE), _tile(d, XS_TILE)
    first = (jnp.arange(LANES) < QK_ROPE).astype(F32).reshape(1, LANES)

    w_lat = jnp.concatenate([w_in[:, :lat_raw], w_in[:, lat_raw - QK_ROPE:lat_raw]], axis=1).astype(BF16)
    tm_lat = _tile(s, LATENT_TM)
    lat = _matmul(h2, w_lat, tm=tm_lat, tn=n_lat, tk=tk, extras=(), extra_specs=[],
                  out_shapes=[jax.ShapeDtypeStruct((m, n_lat), F32)],
                  out_specs=[pl.BlockSpec((tm_lat, n_lat), lambda i, j, k: (i, j))],
                  epilogue=_epi_plain, name="odd_latent_proj")[0]
    n_g = w_in.shape[1] - lat_raw
    tn_g = _tile(n_g, MM_TN)
    g = _matmul_ws([h2], w_in[:, lat_raw:].astype(BF16), tm=tm, tn=tn_g, extras=(), extra_specs=[],
                   out_shapes=[jax.ShapeDtypeStruct((m, n_g), BF16)],
                   out_specs=[pl.BlockSpec((tm, tn_g), lambda j, i: (i, j))],
                   epilogue=_epi_plain, name="odd_gate_proj")[0]

    w_uq3 = w_uq.reshape(q_lora, n_heads, QK_HEAD)
    w_uq_p = jnp.concatenate([w_uq3, w_uq3[:, :, QK_NOPE:]], axis=-1)
    w_uq_p = w_uq_p.reshape(q_lora, n_heads * MLA_QK_PAD).astype(BF16)
    qw = (jnp.concatenate([q_norm, q_norm[QK_NOPE:]]) * (LOG2_E / math.sqrt(QK_HEAD))).reshape(1, MLA_QK_PAD)
    tn_q = _tile(n_heads * MLA_QK_PAD, XS_TILE)
    rope_specs = [pl.BlockSpec((1, LANES), lambda i, j, k: (0, 0))] + \
        [pl.BlockSpec((tm, LANES), lambda i, j, k: (i, 0))] * 2
    q = _matmul(
        lat, w_uq_p, tm=tm, tn=tn_q, tk=q_lora,
        extras=(q_lat_norm.reshape(1, q_lora), qw, first, rc, rs),
        extra_specs=[pl.BlockSpec((1, q_lora), lambda i, j, k: (0, 0)),
                     pl.BlockSpec((1, MLA_QK_PAD), lambda i, j, k: (0, 0))] + rope_specs,
        out_shapes=[jax.ShapeDtypeStruct((m, n_heads * MLA_QK_PAD), BF16)],
        out_specs=[pl.BlockSpec((tm, tn_q), lambda i, j, k: (i, j))],
        epilogue=_epi_mla_q,
        prologue=_pro_rms, name="mla_q_proj")[0]

    assert q_lora % kv_lora == 0 and (q_lora + kv_lora) % LANES == 0
    kvw = QK_NOPE + V_HEAD
    tn_kv = _tile(n_heads * kvw, XS_TILE)
    heads_per_tile = tn_kv // kvw
    kwn = k_norm[:QK_NOPE].reshape(1, QK_NOPE)
    kwp = jnp.concatenate([k_norm[QK_NOPE:], k_norm[QK_NOPE:]]).reshape(1, LANES)
    kpe_block = (q_lora + kv_lora) // LANES
    k, v = _matmul(
        lat, w_ukv.astype(BF16), tm=tm, tn=tn_kv, tk=kv_lora,
        extras=(kv_lat_norm.reshape(1, kv_lora), lat, kwn, kwp, first, rc, rs),
        extra_specs=[pl.BlockSpec((1, kv_lora), lambda i, j, k: (0, 0)),
                     pl.BlockSpec((tm, LANES), lambda i, j, k: (i, kpe_block)),
                     pl.BlockSpec((1, QK_NOPE), lambda i, j, k: (0, 0)),
                     pl.BlockSpec((1, LANES), lambda i, j, k: (0, 0))] + rope_specs,
        out_shapes=[jax.ShapeDtypeStruct((m, n_heads * MLA_QK_PAD), BF16),
                    jax.ShapeDtypeStruct((m, n_heads * V_HEAD), BF16)],
        out_specs=[pl.BlockSpec((tm, heads_per_tile * MLA_QK_PAD), lambda i, j, k: (i, j)),
                   pl.BlockSpec((tm, heads_per_tile * V_HEAD), lambda i, j, k: (i, j))],
        epilogue=_epi_mla_kv,
        prologue=_pro_rms, x_col_block=q_lora // kv_lora, name="mla_kv_proj")
    mixed = _attn_call(
        functools.partial(_mla_kernel, tile=MLA_TILE),
        (q.reshape(b, s, -1), k.reshape(b, s, -1), v.reshape(b, s, -1), g.reshape(b, s, -1)),
        (0, 0, 0, 0), (MLA_QK_PAD, MLA_QK_PAD, V_HEAD, V_HEAD), (mla_bias,), n_heads * V_HEAD, 0,
        n_heads, s,
        [pltpu.VMEM((MLA_TILE, MLA_TILE), F32), pltpu.VMEM((MLA_TILE, MLA_TILE), F32),
         pltpu.VMEM((MLA_TILE, MLA_TILE), BF16), pltpu.VMEM((MLA_TILE, MLA_TILE), BF16),
         pltpu.VMEM((s, V_HEAD), F32), pltpu.VMEM((s, LANES), F32),
         pltpu.VMEM((s, LANES), F32)], "mla_attention")
    return _out_proj(x, [mixed.reshape(m, -1)], w_out, li, gate)


def kernel(x, c, positions, ada_w, ada_b, norm_w, ev_w_in, ev_q_norm, ev_k_norm, ev_w_out, od_w_in,
           od_q_lat_norm, od_kv_lat_norm, od_w_uq, od_w_ukv, od_q_norm, od_k_norm, od_w_out):
    b, s, d = x.shape
    depth = ada_w.shape[0]
    m = b * s

    cos_f, sin_f = _rope_angles(positions, HEAD_DIM)
    cosf = jnp.concatenate([cos_f, cos_f], axis=-1).reshape(m, HEAD_DIM)
    sinf = jnp.concatenate([-sin_f, sin_f], axis=-1).reshape(m, HEAD_DIM)
    cos_m, sin_m = _rope_angles(positions, QK_ROPE)
    z32 = jnp.zeros_like(cos_m)
    z64 = jnp.concatenate([z32, z32], axis=-1)
    rc = jnp.concatenate([cos_m, cos_m, z64], axis=-1).reshape(m, LANES)
    rs = jnp.concatenate([-sin_m, sin_m, z64], axis=-1).reshape(m, LANES)

    sb_u = jnp.asarray(np.tril(np.ones((SB_TILE, SB_TILE), np.float32), -1), BF16)
    dl_bias = jnp.asarray(_dl_band_bias())
    causal = np.tril(np.ones((MLA_TILE, MLA_TILE), bool))
    mla_bias = jnp.asarray(np.where(causal, 0.0, -np.inf).astype(np.float32))

    def split_mod(mod):
        return [mod[:, :, i * d:(i + 1) * d].reshape(mod.shape[0], b, 1, d) for i in range(3)]

    n_mod_blocks = (depth - 1) * (ada_w.shape[2] // _tile(ada_w.shape[2], MOD_SIDE_TN))
    ride_along = depth > 1 and n_mod_blocks <= b * (ev_w_in.shape[-1] // (8 * HEAD_DIM)) * (s // DL_CHUNK)
    shift, scale, gate = split_mod(_modulation(c, ada_w, ada_b, 1 if ride_along else depth))

    for layer in range(depth):
        h = _norm_mod(x, norm_w[layer], scale[layer], shift[layer])
        i = layer // 2
        if layer % 2 == 0:
            side = (_pad_rows(c), ada_w, ada_b, 1) if (ride_along and layer == 0) else None
            x, mod_rest = _even_layer(x, h, gate[layer], i, ev_w_in, ev_q_norm[i], ev_k_norm[i], ev_w_out,
                                      cosf, sinf, sb_u, dl_bias, side)
            if mod_rest is not None:
                shift, scale, gate = (jnp.concatenate([first, rest], axis=0) for first, rest in
                                      zip((shift, scale, gate), split_mod(mod_rest[:, :b])))
        else:
            x = _odd_layer(x, h, gate[layer], i, od_w_in[i], od_q_lat_norm[i], od_kv_lat_norm[i],
                           od_w_uq[i], od_w_ukv[i], od_q_norm[i], od_k_norm[i], od_w_out,
                           rc, rs, mla_bias)
    return x
```

```python
import functools
import math

import numpy as np
import jax
import jax.numpy as jnp
from jax import lax
from jax.experimental import pallas as pl
from jax.experimental.pallas import tpu as pltpu

F32 = jnp.float32
BF16 = jnp.bfloat16

HEAD_DIM = 128
DIL_PATTERNS = ((128, 1), (512, 4), (2048, 16))
QK_NOPE = 128
QK_ROPE = 64
QK_HEAD = QK_NOPE + QK_ROPE
V_HEAD = 128
MLA_QK_PAD = 256
ROPE_THETA = 10000.0
EPS = 1e-6

LANES = 128
SUBLANES = 8
MXU_COLS = 256
VMEM_LIMIT = 61 * 1024 * 1024
EXP_UNDERFLOW = -104.0
LOG2_E = math.log2(math.e)
SB_TILE = 256
SB_SUBTILES = 8
DL_TILE = 128
DL_CHUNK = DL_TILE * max(d for _, d in DIL_PATTERNS)
MLA_TILE = 512
MLA_ROW_BLOCK = 64
MM_TM = 512
MM_TN = 1024
XS_TILE = 1024
LATENT_TM = 512
MOD_TN = 1024
MOD_SIDE_TN = 768
NORM_ROWS = 512
NORM_ROW_BLOCK = 16


def _cparams(sem):
    return pltpu.CompilerParams(dimension_semantics=sem, vmem_limit_bytes=VMEM_LIMIT)


def _tile(n, want):
    if n <= want:
        return n
    t = (want // LANES) * LANES
    while n % t:
        t -= LANES
    return t


def _mod_kernel(c_ref, w_ref, b_ref, o_ref):
    a = c_ref[...]
    a = (a * jax.nn.sigmoid(a)).astype(BF16)
    o_ref[...] = jnp.dot(a, w_ref[...].astype(BF16), preferred_element_type=F32) + b_ref[...]


def _pad_rows(c):
    b = c.shape[0]
    return jnp.pad(c, ((0, -b % SUBLANES), (0, 0)))


def _modulation(c, ada_w, ada_b, depth):
    _, d, n3 = ada_w.shape
    b = c.shape[0]
    c_pad = _pad_rows(c)
    rows = c_pad.shape[0]
    tn = _tile(n3, MOD_TN)
    out = pl.pallas_call(
        _mod_kernel,
        grid=(depth, n3 // tn),
        in_specs=[pl.BlockSpec((rows, d), lambda l, j: (0, 0)),
                  pl.BlockSpec((None, d, tn), lambda l, j: (l, 0, j)),
                  pl.BlockSpec((None, 1, tn), lambda l, j: (l, 0, j))],
        out_specs=pl.BlockSpec((None, rows, tn), lambda l, j: (l, 0, j)),
        out_shape=jax.ShapeDtypeStruct((depth, rows, n3), F32),
        compiler_params=_cparams(("parallel", "parallel")),
        name="modulation",
    )(c_pad, ada_w, ada_b.reshape(-1, 1, n3))
    return out[:, :b]


def _norm_mod_kernel(x_ref, w_ref, scale_ref, shift_ref, o_ref):
    gain = w_ref[...] * (1.0 + scale_ref[...])

    def body(r, _):
        rows = pl.ds(pl.multiple_of(r * NORM_ROW_BLOCK, NORM_ROW_BLOCK), NORM_ROW_BLOCK)
        x = x_ref[rows, :]
        rstd = lax.rsqrt(jnp.mean(x * x, axis=-1, keepdims=True) + EPS)
        o_ref[rows, :] = (x_ref[rows, :] * rstd * gain + shift_ref[...]).astype(o_ref.dtype)
        return 0

    lax.fori_loop(0, x_ref.shape[0] // NORM_ROW_BLOCK, body, 0, unroll=2)


def _norm_mod(x, w, scale, shift):
    b, s, d = x.shape
    ts = _tile(s, NORM_ROWS)
    return pl.pallas_call(
        _norm_mod_kernel,
        grid=(b, s // ts),
        in_specs=[pl.BlockSpec((None, ts, d), lambda bi, i: (bi, i, 0)),
                  pl.BlockSpec((1, d), lambda bi, i: (0, 0)),
                  pl.BlockSpec((None, 1, d), lambda bi, i: (bi, 0, 0)),
                  pl.BlockSpec((None, 1, d), lambda bi, i: (bi, 0, 0))],
        out_specs=pl.BlockSpec((None, ts, d), lambda bi, i: (bi, i, 0)),
        out_shape=jax.ShapeDtypeStruct((b, s, d), BF16),
        compiler_params=_cparams(("parallel", "parallel")),
        name="norm_mod",
    )(x, w.reshape(1, d), scale, shift)


def _run_epilogue(variants, chunk_of, tn, acc_ref):
    chunks = [(c0, min(MXU_COLS, tn - c0)) for c0 in range(0, tn, MXU_COLS)]

    def fused(fn):
        for c0, w in chunks:
            fn(chunk_of(c0, w), c0)

    def matmul_only():
        acc_ref[...] = chunk_of(0, tn)

    def epilogue_only(fn):
        for c0, w in chunks:
            fn(acc_ref.at[:, c0:c0 + w], c0)

    split = [(cond, fn) for cond, fn in variants if getattr(fn, "via_ref", False) and cond is not None]
    if split:
        pl.when(functools.reduce(jnp.logical_or, [cond for cond, _ in split]))(matmul_only)
    for cond, fn in variants:
        if getattr(fn, "via_ref", False):
            if cond is None:
                matmul_only()
                epilogue_only(fn)
            else:
                pl.when(cond)(functools.partial(epilogue_only, fn))
        elif cond is None:
            fused(fn)
        else:
            pl.when(cond)(functools.partial(fused, fn))


def _via_ref(fn):
    fn.via_ref = True
    return fn


def _mm_kernel(*refs, n_extra, n_out, nk, tn, epilogue, prologue):
    x_ref, w_ref = refs[0], refs[1]
    extras = refs[2:2 + n_extra]
    outs = refs[2 + n_extra:2 + n_extra + n_out]
    scratch = refs[2 + n_extra + n_out:]
    acc_ref = scratch[0]
    if prologue is not None:
        h_ref = scratch[1]

        @pl.when(pl.program_id(1) == 0)
        def _():
            h_ref[...] = prologue(x_ref, extras)

        _run_epilogue(epilogue(extras, outs),
                      lambda c0, w: jnp.dot(h_ref[...], w_ref[:, c0:c0 + w], preferred_element_type=F32),
                      tn, acc_ref)
        return
    k = pl.program_id(2)

    @pl.when(k == 0)
    def _():
        acc_ref[...] = jnp.zeros_like(acc_ref)

    acc_ref[...] += jnp.dot(x_ref[...], w_ref[...], preferred_element_type=F32)

    @pl.when(k == nk - 1)
    def _():
        _run_epilogue(epilogue(extras, outs), lambda c0, w: acc_ref[:, c0:c0 + w], tn, acc_ref)


def _matmul(x, w, *, tm, tn, tk, extras, extra_specs, out_shapes, out_specs, epilogue,
            prologue=None, x_col_block=0, name):
    m = x.shape[0]
    kdim, n = w.shape
    nk = kdim // tk
    assert m % tm == 0 and n % tn == 0 and kdim % tk == 0 and x.shape[1] >= (x_col_block + nk) * tk
    scratch = [pltpu.VMEM((tm, tn), F32)]
    if prologue is not None:
        assert nk == 1
        scratch.append(pltpu.VMEM((tm, kdim), BF16))
    kern = functools.partial(_mm_kernel, n_extra=len(extras), n_out=len(out_shapes), nk=nk, tn=tn,
                             epilogue=epilogue, prologue=prologue)
    return pl.pallas_call(
        kern,
        grid=(m // tm, n // tn, nk),
        in_specs=[pl.BlockSpec((tm, tk), lambda i, j, k: (i, x_col_block + k)),
                  pl.BlockSpec((tk, tn), lambda i, j, k: (k, j))] + list(extra_specs),
        out_specs=list(out_specs),
        out_shape=list(out_shapes),
        scratch_shapes=scratch,
        compiler_params=_cparams(("parallel", "arbitrary", "arbitrary")),
        name=name,
    )(x, w, *extras)


def _mm_ws_kernel(*refs, k_parts, n_extra, n_out, tn, cast, epilogue):
    n_x = len(k_parts)
    x_refs = refs[:n_x]
    w_ref = refs[n_x]
    extras = refs[n_x + 1:n_x + 1 + n_extra]
    outs = refs[n_x + 1 + n_extra:n_x + 1 + n_extra + n_out]
    scratch = refs[n_x + 1 + n_extra + n_out:]
    acc_ref = scratch[0]
    if cast:
        wb_ref = scratch[1]

        @pl.when(pl.program_id(1) == 0)
        def _():
            wb_ref[...] = w_ref[...].astype(BF16)
    else:
        wb_ref = w_ref

    def chunk_of(c0, w):
        acc = None
        k0 = 0
        for x_ref, kp in zip(x_refs, k_parts):
            part = jnp.dot(x_ref[...], wb_ref[k0:k0 + kp, c0:c0 + w], preferred_element_type=F32)
            acc = part if acc is None else acc + part
            k0 += kp
        return acc

    _run_epilogue(epilogue(extras, outs), chunk_of, tn, acc_ref)


def _matmul_ws(xs, w, *, tm, tn, extras, extra_specs, out_shapes, out_specs, epilogue, name, layer=None):
    m = xs[0].shape[0]
    kdim, n = w.shape[-2:]
    if layer is None:
        w_spec = pl.BlockSpec((kdim, tn), lambda j, i: (0, j))
    else:
        w_spec = pl.BlockSpec((None, kdim, tn), lambda j, i: (layer, 0, j))
    k_parts = tuple(x.shape[1] for x in xs)
    assert m % tm == 0 and n % tn == 0 and sum(k_parts) == kdim
    cast = w.dtype != BF16
    scratch = [pltpu.VMEM((tm, tn), F32)]
    if cast:
        scratch.append(pltpu.VMEM((kdim, tn), BF16))
    kern = functools.partial(_mm_ws_kernel, k_parts=k_parts, n_extra=len(extras),
                             n_out=len(out_shapes), tn=tn, cast=cast, epilogue=epilogue)
    x_specs = [pl.BlockSpec((tm, kp), lambda j, i: (i, 0)) for kp in k_parts]
    return pl.pallas_call(
        kern,
        grid=(n // tn, m // tm),
        in_specs=x_specs + [w_spec] + list(extra_specs),
        out_specs=list(out_specs),
        out_shape=list(out_shapes),
        scratch_shapes=scratch,
        compiler_params=_cparams(("parallel", "arbitrary")),
        name=name,
    )(*xs, w, *extras)


def _epi_plain(extras, outs):
    o_ref = outs[0]

    def fn(a, c0):
        o_ref[:, c0:c0 + a.shape[1]] = a.astype(o_ref.dtype)

    return [(None, fn)]


def _epi_resid(extras, outs):
    xres_ref, gate_ref = extras
    o_ref = outs[0]

    def fn(a, c0):
        sl = slice(c0, c0 + a.shape[1])
        o_ref[:, sl] = xres_ref[:, sl] + gate_ref[:, sl] * a

    return [(None, fn)]


def _rope_full(y, cos_ref, sin_ref):
    return y * cos_ref[...] + pltpu.roll(y, HEAD_DIM // 2, 1) * sin_ref[...]


def _rope_mla(r, c_ref, s_ref):
    return r * c_ref[...] + pltpu.roll(r, QK_ROPE // 2, 1) * s_ref[...]


def _epi_even(extras, outs, *, part_tiles, q_scale):
    qn_ref, kn_ref, cos_ref, sin_ref = extras
    o_ref = outs[0]
    part = pl.program_id(0) // part_tiles

    def scaled(scale):
        def fn(a, c0):
            y = a if scale == 1.0 else a * scale
            o_ref[:, c0:c0 + a.shape[1]] = y.astype(o_ref.dtype)
        return fn

    def headnorm_rope(w_ref, scale):
        @_via_ref
        def fn(a, c0):
            for hh in range(a.shape[1] // HEAD_DIM):
                ah = a[:, hh * HEAD_DIM:(hh + 1) * HEAD_DIM]
                y = ah * lax.rsqrt(jnp.mean(ah * ah, axis=-1, keepdims=True) + EPS) * w_ref[...]
                y = _rope_full(y, cos_ref, sin_ref)
                if scale != 1.0:
                    y = y * scale
                o_ref[:, c0 + hh * HEAD_DIM:c0 + (hh + 1) * HEAD_DIM] = y.astype(o_ref.dtype)
        return fn

    return [(part == 0, scaled(q_scale)),
            (part == 4, headnorm_rope(qn_ref, q_scale * LOG2_E)),
            (part == 5, headnorm_rope(kn_ref, 1.0)),
            ((part != 0) & (part != 4) & (part != 5), scaled(1.0))]


def _pro_rms(x_ref, extras):
    x = x_ref[...]
    w_ref = extras[0]
    y = x * lax.rsqrt(jnp.mean(x * x, axis=-1, keepdims=True) + EPS) * w_ref[...]
    return y.astype(BF16)


def _epi_mla_q(extras, outs):
    _, qw_ref, first_ref, c_ref, s_ref = extras
    o_ref = outs[0]
    assert MXU_COLS == MLA_QK_PAD

    @_via_ref
    def fn(a, c0):
        nope, rope = a[:, :QK_NOPE], a[:, QK_NOPE:]
        sq = nope * nope + rope * rope * first_ref[...]
        rstd = lax.rsqrt(jnp.sum(sq, axis=-1, keepdims=True) * (1.0 / QK_HEAD) + EPS)
        o_ref[:, c0:c0 + QK_NOPE] = (nope * rstd * qw_ref[:, :QK_NOPE]).astype(o_ref.dtype)
        r = _rope_mla(rope * rstd * qw_ref[:, QK_NOPE:], c_ref, s_ref)
        o_ref[:, c0 + QK_NOPE:c0 + MLA_QK_PAD] = r.astype(o_ref.dtype)

    return [(None, fn)]


def _epi_mla_kv(extras, outs):
    _, kpe_ref, kwn_ref, kwp_ref, first_ref, c_ref, s_ref = extras
    k_ref, v_ref = outs
    assert MXU_COLS == QK_NOPE + V_HEAD
    kpe = kpe_ref[...]
    ss_pe = jnp.sum(kpe * kpe * first_ref[...], axis=-1, keepdims=True)
    kr = _rope_mla(kpe * kwp_ref[...], c_ref, s_ref)

    @_via_ref
    def fn(a, c0):
        head = c0 // (QK_NOPE + V_HEAD)
        kn = a[:, :QK_NOPE]
        rstd = lax.rsqrt((jnp.sum(kn * kn, axis=-1, keepdims=True) + ss_pe) * (1.0 / QK_HEAD) + EPS)
        kb = head * MLA_QK_PAD
        k_ref[:, kb:kb + QK_NOPE] = (kn * rstd * kwn_ref[...]).astype(k_ref.dtype)
        k_ref[:, kb + QK_NOPE:kb + MLA_QK_PAD] = (kr * rstd).astype(k_ref.dtype)
        v_ref[:, head * V_HEAD:(head + 1) * V_HEAD] = a[:, QK_NOPE:].astype(v_ref.dtype)

    return [(None, fn)]


def _qk(q, k):
    return lax.dot_general(q, k, (((1,), (1,)), ((), ())), preferred_element_type=F32)


def _sb_stages(q_ref, k_ref, v_ref, g_ref, u_ref, o_ref, acc_ref, carry_ref, qi, tile):
    n_sub = q_ref.shape[0] // tile

    def tile_rows(kb):
        return pl.ds(pl.multiple_of(kb * tile, tile), tile)

    def weights(q, kb, carry, diagonal=False, valid=None):
        z = _qk(q, k_ref[tile_rows(kb), :])
        sp = jnp.log(1.0 + jnp.exp(-jnp.abs(z)))
        log_beta = jnp.minimum(z, 0.0) - sp
        log_keep = jnp.minimum(-z, 0.0) - sp
        keep = valid
        if diagonal:
            row = lax.broadcasted_iota(jnp.int32, z.shape, 0)
            col = lax.broadcasted_iota(jnp.int32, z.shape, 1)
            keep = col < row
        if keep is not None:
            log_keep = jnp.where(keep, log_keep, 0.0)
        hi = log_keep.astype(BF16)
        lo = (log_keep - hi.astype(F32)).astype(BF16)
        u = u_ref[...]
        excl = jnp.dot(hi, u, preferred_element_type=F32) + jnp.dot(lo, u, preferred_element_type=F32)
        if carry is not None:
            excl = excl + carry
        w = jnp.exp(log_beta + excl)
        if keep is not None:
            w = jnp.where(keep, w, 0.0)
        return w.astype(BF16), jnp.sum(log_keep, axis=-1, keepdims=True)

    def pv(w, kb):
        return jnp.dot(w, v_ref[tile_rows(kb), :], preferred_element_type=F32)

    def front():
        tails = []
        for sub in range(n_sub):
            rows = slice(sub * tile, (sub + 1) * tile)
            q = q_ref[rows, :]
            diag = qi * n_sub + sub
            prev = jnp.maximum(diag - 1, 0)
            w_d, sum_d = weights(q, diag, None, diagonal=True)
            w_p, sum_p = weights(q, prev, sum_d, valid=diag > 0)
            carry = sum_d + sum_p
            acc_ref[rows, :] = pv(w_d, diag) + pv(w_p, prev)
            carry_ref[rows, :] = carry
            tails.append((rows, diag - 2, jnp.max(carry)))
        return tails

    def tail(tails):
        for rows, kb0, cmax0 in tails:
            def cond(state):
                kb, cmax = state
                return (kb >= 0) & (cmax > EXP_UNDERFLOW)

            def body(state, rows=rows):
                kb, _ = state
                carry = carry_ref[rows, :]
                w, sum_k = weights(q_ref[rows, :], kb, carry)
                acc_ref[rows, :] += pv(w, kb)
                carry = carry + sum_k
                carry_ref[rows, :] = carry
                return kb - 1, jnp.max(carry)

            lax.while_loop(cond, body, (kb0, cmax0))

    def finish():
        g = g_ref[...].astype(F32)
        o_ref[...] = (acc_ref[...] * (g * jax.nn.sigmoid(g))).astype(o_ref.dtype)

    return front, tail, finish


def _dl_stages(q_ref, k_ref, v_ref, g_ref, bias_ref, o_ref, qf_ref, kf_ref, vf_ref, acc_ref, m_ref, l_ref,
               ci, chunk):
    def prepare():
        @pl.when(ci == 0)
        def _():
            kf_ref[...] = k_ref[...].astype(F32)
            vf_ref[...] = v_ref[...].astype(F32)

    base = ci * chunk

    def body(idx, _):
        for pi, (_, dil) in enumerate(DIL_PATTERNS):
            unit = DL_TILE * dil
            q0 = (idx // dil) * unit + idx % dil
            rows = pl.ds(q0, DL_TILE, stride=dil)
            cur0 = base + q0
            has_prev = cur0 >= unit
            prev0 = jnp.maximum(cur0 - unit, 0)
            prev_rows = pl.ds(prev0, DL_TILE, stride=dil)
            cur_rows = pl.ds(cur0, DL_TILE, stride=dil)
            kk = jnp.concatenate([kf_ref[prev_rows, :], kf_ref[cur_rows, :]], axis=0).astype(BF16)
            vv = jnp.concatenate([vf_ref[prev_rows, :], vf_ref[cur_rows, :]], axis=0).astype(BF16)
            s = _qk(qf_ref[rows, :].astype(BF16), kk) + bias_ref[pi, has_prev.astype(jnp.int32)]
            m = jnp.max(s, axis=-1, keepdims=True)
            p = jnp.exp2(s - m)
            l = jnp.sum(p, axis=-1, keepdims=True)
            acc_ref[pi, rows, :] = jnp.dot(p.astype(BF16), vv, preferred_element_type=F32)
            m_ref[pi, rows, :] = jnp.broadcast_to(m, (DL_TILE, LANES))
            l_ref[pi, rows, :] = jnp.broadcast_to(l, (DL_TILE, LANES))
        return 0

    def front():
        qf_ref[...] = q_ref[...].astype(F32)
        lax.fori_loop(0, chunk // DL_TILE, body, 0, unroll=True)

    def finish():
        n_pat = len(DIL_PATTERNS)
        m_all = functools.reduce(jnp.maximum, [m_ref[pi] for pi in range(n_pat)])
        num = jnp.zeros(acc_ref.shape[1:], F32)
        den = jnp.zeros(acc_ref.shape[1:], F32)
        for pi in range(n_pat):
            w = jnp.exp2(m_ref[pi] - m_all)
            num = num + w * acc_ref[pi]
            den = den + w * l_ref[pi]
        g = g_ref[...].astype(F32)
        o_ref[...] = (num / den * (g * jax.nn.sigmoid(g))).astype(o_ref.dtype)

    return prepare, front, finish


def _even_attn_kernel(sq_ref, sk_ref, sv_ref, sg_ref, dq_ref, dk_ref, dv_ref, dg_ref, u_ref, bias_ref,
                      *rest, sb_tile, chunk, with_mod):
    if with_mod:
        c_ref, adaw_ref, adab_ref, o_sb_ref, o_dl_ref, mod_ref = rest[:6]
        rest = rest[6:]
    else:
        o_sb_ref, o_dl_ref = rest[:2]
        rest = rest[2:]
    sb_acc_ref, sb_carry_ref, qf_ref, kf_ref, vf_ref, dl_acc_ref, dl_m_ref, dl_l_ref = rest
    ci = pl.program_id(2)
    sb_front, sb_tail, sb_finish = _sb_stages(sq_ref, sk_ref, sv_ref, sg_ref, u_ref, o_sb_ref, sb_acc_ref,
                                              sb_carry_ref, ci, sb_tile)
    dl_prepare, dl_front, dl_finish = _dl_stages(dq_ref, dk_ref, dv_ref, dg_ref, bias_ref, o_dl_ref, qf_ref,
                                                 kf_ref, vf_ref, dl_acc_ref, dl_m_ref, dl_l_ref, ci, chunk)
    dl_prepare()
    if with_mod:
        _mod_kernel(c_ref, adaw_ref, adab_ref, mod_ref)
    tails = sb_front()
    dl_front()
    sb_tail(tails)
    sb_finish()
    dl_finish()


def _softmax_update(s_ref, bias_ref, v, p_ref, acc_ref, m_ref, l_ref):
    tq, tk = s_ref.shape
    for r0 in range(0, tq, MLA_ROW_BLOCK):
        rs = slice(r0, r0 + MLA_ROW_BLOCK)
        cols = [s_ref[rs, c * LANES:(c + 1) * LANES] for c in range(tk // LANES)]
        if bias_ref is not None:
            cols = [col + bias_ref[rs, c * LANES:(c + 1) * LANES] for c, col in enumerate(cols)]
        m_old = m_ref[rs, :]
        tile_max = jnp.max(functools.reduce(jnp.maximum, cols), axis=-1, keepdims=True)
        m_new = jnp.maximum(m_old, tile_max)
        ps = [jnp.exp2(col - m_new) for col in cols]
        alpha = jnp.exp2(m_old - m_new)
        l_ref[rs, :] = alpha * l_ref[rs, :] + jnp.sum(functools.reduce(jnp.add, ps), axis=-1, keepdims=True)
        m_ref[rs, :] = m_new
        acc_ref[rs, :] = alpha * acc_ref[rs, :]
        for c, x in enumerate(ps):
            p_ref[rs, c * LANES:(c + 1) * LANES] = x.astype(BF16)
    acc_ref[...] += jnp.dot(p_ref[...], v, preferred_element_type=F32)


def _mla_kernel(q_ref, k_ref, v_ref, g_ref, bias_ref, o_ref, sa_ref, sb_ref, pa_ref, pb_ref, acc_ref, m_ref,
                l_ref, *, tile):
    n_q = q_ref.shape[0] // tile
    acc_ref[...] = jnp.zeros_like(acc_ref)
    m_ref[...] = jnp.full_like(m_ref, -jnp.inf)
    l_ref[...] = jnp.zeros_like(l_ref)

    def rows(t):
        return pl.ds(pl.multiple_of(t * tile, tile), tile)

    def scores(qi, kb, dst_ref):
        dst_ref[...] = _qk(q_ref[rows(qi), :], k_ref[rows(kb), :])

    def pipeline(n_steps, start, advance, bias):
        def update(qi, kb, src_ref, p_ref):
            r = rows(qi)
            _softmax_update(src_ref, bias, v_ref[rows(kb), :], p_ref, acc_ref.at[r, :], m_ref.at[r, :],
                            l_ref.at[r, :])

        scores(*start, sa_ref)

        def pair(_, state):
            qa, ka = state
            qb, kb = advance(qa, ka)
            scores(qb, kb, sb_ref)
            update(qa, ka, sa_ref, pa_ref)
            qa, ka = advance(qb, kb)
            scores(qa, ka, sa_ref)
            update(qb, kb, sb_ref, pb_ref)
            return qa, ka

        qa, ka = lax.fori_loop(0, n_steps // 2, pair, start, unroll=4)
        if n_steps % 2:
            update(qa, ka, sa_ref, pa_ref)

    def next_lower(qi, kb):
        wrap = kb == qi - 1
        return jnp.where(wrap, jnp.minimum(qi + 1, n_q - 1), qi), jnp.where(wrap, 0, kb + 1)

    def next_diagonal(qi, kb):
        t = jnp.minimum(qi + 1, n_q - 1)
        return t, t

    zero, one = jnp.int32(0), jnp.int32(1)
    if n_q > 1:
        pipeline(n_q * (n_q - 1) // 2, (one, zero), next_lower, None)
    pipeline(n_q, (zero, zero), next_diagonal, bias_ref)

    g = g_ref[...].astype(F32)
    o_ref[...] = (acc_ref[...] * g / (l_ref[...] * (1.0 + jnp.exp(-g)))).astype(o_ref.dtype)


def _attn_call(kern, arrays, col_blocks, widths, consts, out_width_total, out_col0, n_heads, tile,
               scratch, name):
    q, k, v, g = arrays
    b, s, _ = q.shape
    cq, ck, cv, cg = col_blocks
    wq, wk, wv, wg = widths
    in_specs = [
        pl.BlockSpec((None, tile, wq), lambda bi, h, i: (bi, i, cq + h)),
        pl.BlockSpec((None, s, wk), lambda bi, h, i: (bi, 0, ck + h)),
        pl.BlockSpec((None, s, wv), lambda bi, h, i: (bi, 0, cv + h)),
        pl.BlockSpec((None, tile, wg), lambda bi, h, i: (bi, i, cg + h)),
    ]
    for cst in consts:
        in_specs.append(pl.BlockSpec(cst.shape, lambda bi, h, i, nd=cst.ndim: (0,) * nd))
    return pl.pallas_call(
        kern,
        grid=(b, n_heads, s // tile),
        in_specs=in_specs,
        out_specs=pl.BlockSpec((None, tile, wv), lambda bi, h, i: (bi, i, out_col0 + h)),
        out_shape=jax.ShapeDtypeStruct((b, s, out_width_total), BF16),
        scratch_shapes=scratch,
        compiler_params=_cparams(("parallel", "parallel", "arbitrary")),
        name=name,
    )(q, k, v, g, *consts)


def _dl_band_bias():
    i = np.arange(DL_TILE)[:, None]
    c = np.arange(2 * DL_TILE)[None, :]
    out = []
    for window, dil in DIL_PATTERNS:
        span = window // dil
        assert span <= DL_TILE
        dist = i + DL_TILE - c
        band = (dist >= 0) & (dist <= span)
        out.append(np.stack([band & (c >= DL_TILE), band]))
    return np.where(np.stack(out), 0.0, -np.inf).astype(np.float32)


def _rope_angles(positions, dim):
    inv = ROPE_THETA ** (-jnp.arange(0, dim, 2, dtype=F32) / dim)
    ang = positions.astype(F32)[..., None] * inv
    return jnp.cos(ang), jnp.sin(ang)


def _even_layer(x, h, gate, li, w_in, qn, kn, w_out, cosf, sinf, sb_u, dl_bias, mod_side=None):
    b, s, d = x.shape
    m = b * s
    n_in = w_in.shape[-1]
    w_part = n_in // 8
    n_heads = w_part // HEAD_DIM
    tm, tn = _tile(s, MM_TM), _tile(w_part, MM_TN)
    q_scale = 1.0 / math.sqrt(HEAD_DIM)
    proj = _matmul_ws(
        [h.reshape(m, d)], w_in, layer=li, tm=tm, tn=tn,
        extras=(qn.reshape(1, HEAD_DIM), kn.reshape(1, HEAD_DIM), cosf, sinf),
        extra_specs=[pl.BlockSpec((1, HEAD_DIM), lambda j, i: (0, 0)),
                     pl.BlockSpec((1, HEAD_DIM), lambda j, i: (0, 0)),
                     pl.BlockSpec((tm, HEAD_DIM), lambda j, i: (i, 0)),
                     pl.BlockSpec((tm, HEAD_DIM), lambda j, i: (i, 0))],
        out_shapes=[jax.ShapeDtypeStruct((m, n_in), BF16)],
        out_specs=[pl.BlockSpec((tm, tn), lambda j, i: (i, j))],
        epilogue=functools.partial(_epi_even, part_tiles=w_part // tn, q_scale=q_scale),
        name="even_in_proj")[0].reshape(b, s, n_in)

    assert s % DL_CHUNK == 0 and SB_TILE * SB_SUBTILES == DL_CHUNK
    n_pat = len(DIL_PATTERNS)

    def head_block(part, rows):
        if rows == s:
            return pl.BlockSpec((None, s, HEAD_DIM), lambda bi, h, i: (bi, 0, part * n_heads + h))
        return pl.BlockSpec((None, rows, HEAD_DIM), lambda bi, h, i: (bi, i, part * n_heads + h))

    part_rows = (DL_CHUNK, s, s, DL_CHUNK) * 2
    out_spec = pl.BlockSpec((None, DL_CHUNK, HEAD_DIM), lambda bi, h, i: (bi, i, h))
    n_chunks = s // DL_CHUNK
    inputs = [proj] * 8 + [sb_u, dl_bias]
    in_specs = [head_block(p, r) for p, r in enumerate(part_rows)] + [
        pl.BlockSpec(sb_u.shape, lambda bi, h, i: (0, 0)),
        pl.BlockSpec(dl_bias.shape, lambda bi, h, i: (0, 0, 0, 0))]
    out_specs = [out_spec, out_spec]
    out_shapes = [jax.ShapeDtypeStruct((b, s, w_part), BF16)] * 2
    if mod_side is not None:
        c_pad, ada_w, ada_b, first_layer = mod_side
        n_layers, _, n3 = ada_w.shape
        tn_mod = _tile(n3, MOD_SIDE_TN)
        per_layer = n3 // tn_mod
        n_blocks = (n_layers - first_layer) * per_layer
        assert n_blocks <= b * n_heads * n_chunks

        def mod_block(bi, h, i):
            blk = jnp.minimum((bi * n_heads + h) * n_chunks + i, n_blocks - 1)
            return blk // per_layer, blk % per_layer

        inputs += [c_pad, ada_w, ada_b.reshape(n_layers, 1, n3)]
        in_specs += [
            pl.BlockSpec(c_pad.shape, lambda bi, h, i: (0, 0)),
            pl.BlockSpec((None, d, tn_mod), lambda bi, h, i: (first_layer + mod_block(bi, h, i)[0], 0,
                                                              mod_block(bi, h, i)[1])),
            pl.BlockSpec((None, 1, tn_mod), lambda bi, h, i: (first_layer + mod_block(bi, h, i)[0], 0,
                                                              mod_block(bi, h, i)[1]))]
        out_specs.append(pl.BlockSpec((None, c_pad.shape[0], tn_mod),
                                      lambda bi, h, i: (mod_block(bi, h, i)[0], 0, mod_block(bi, h, i)[1])))
        out_shapes.append(jax.ShapeDtypeStruct((n_layers - first_layer, c_pad.shape[0], n3), F32))
    outs = pl.pallas_call(
        functools.partial(_even_attn_kernel, sb_tile=SB_TILE, chunk=DL_CHUNK, with_mod=mod_side is not None),
        grid=(b, n_heads, n_chunks),
        in_specs=in_specs,
        out_specs=out_specs,
        out_shape=out_shapes,
        scratch_shapes=[pltpu.VMEM((DL_CHUNK, HEAD_DIM), F32), pltpu.VMEM((DL_CHUNK, 1), F32),
                        pltpu.VMEM((DL_CHUNK, HEAD_DIM), F32), pltpu.VMEM((s, HEAD_DIM), F32),
                        pltpu.VMEM((s, HEAD_DIM), F32), pltpu.VMEM((n_pat, DL_CHUNK, HEAD_DIM), F32),
                        pltpu.VMEM((n_pat, DL_CHUNK, LANES), F32), pltpu.VMEM((n_pat, DL_CHUNK, LANES), F32)],
        compiler_params=_cparams(("arbitrary",) * 3 if mod_side is not None
                                 else ("parallel", "parallel", "arbitrary")),
        name="even_attention",
    )(*inputs)
    o_sb, o_dl = outs[:2]
    mod_rest = outs[2] if mod_side is not None else None
    return _out_proj(x, [o_sb.reshape(m, w_part), o_dl.reshape(m, w_part)], w_out, li, gate), mod_rest


def _out_proj(x, mixed_parts, w_out, li, gate):
    b, s, d = x.shape
    m = b * s
    tm, tn = _tile(s, MM_TM), _tile(d, MM_TN)
    rows_per_batch = s // tm
    return _matmul_ws(
        mixed_parts, w_out, layer=li, tm=tm, tn=tn,
        extras=(x.reshape(m, d), gate),
        extra_specs=[pl.BlockSpec((tm, tn), lambda j, i: (i, j)),
                     pl.BlockSpec((None, 1, tn), lambda j, i: (i // rows_per_batch, 0, j))],
        out_shapes=[jax.ShapeDtypeStruct((m, d), F32)],
        out_specs=[pl.BlockSpec((tm, tn), lambda j, i: (i, j))],
        epilogue=_epi_resid, name="out_proj")[0].reshape(b, s, d)


def _odd_layer(x, h, gate, li, w_in, q_lat_norm, kv_lat_norm, w_uq, w_ukv, q_norm, k_norm, w_out,
               rc, rs, mla_bias):
    b, s, d = x.shape
    m = b * s
    q_lora = q_lat_norm.shape[0]
    kv_lora = kv_lat_norm.shape[0]
    n_heads = w_uq.shape[1] // QK_HEAD
    lat_raw = q_lora + kv_lora + QK_ROPE
    n_lat = lat_raw + QK_ROPE
    assert n_lat % LANES == 0 and 2 * QK_ROPE == LANES
    h2 = h.reshape(m, d)
    tm, tk = _tile(s, XS_TILE), _tile(d, XS_TILE)
    first = (jnp.arange(LANES) < QK_ROPE).astype(F32).reshape(1, LANES)

    w_lat = jnp.concatenate([w_in[:, :lat_raw], w_in[:, lat_raw - QK_ROPE:lat_raw]], axis=1).astype(BF16)
    tm_lat = _tile(s, LATENT_TM)
    lat = _matmul(h2, w_lat, tm=tm_lat, tn=n_lat, tk=tk, extras=(), extra_specs=[],
                  out_shapes=[jax.ShapeDtypeStruct((m, n_lat), F32)],
                  out_specs=[pl.BlockSpec((tm_lat, n_lat), lambda i, j, k: (i, j))],
                  epilogue=_epi_plain, name="odd_latent_proj")[0]
    n_g = w_in.shape[1] - lat_raw
    tn_g = _tile(n_g, MM_TN)
    g = _matmul_ws([h2], w_in[:, lat_raw:].astype(BF16), tm=tm, tn=tn_g, extras=(), extra_specs=[],
                   out_shapes=[jax.ShapeDtypeStruct((m, n_g), BF16)],
                   out_specs=[pl.BlockSpec((tm, tn_g), lambda j, i: (i, j))],
                   epilogue=_epi_plain, name="odd_gate_proj")[0]

    w_uq3 = w_uq.reshape(q_lora, n_heads, QK_HEAD)
    w_uq_p = jnp.concatenate([w_uq3, w_uq3[:, :, QK_NOPE:]], axis=-1)
    w_uq_p = w_uq_p.reshape(q_lora, n_heads * MLA_QK_PAD).astype(BF16)
    qw = (jnp.concatenate([q_norm, q_norm[QK_NOPE:]]) * (LOG2_E / math.sqrt(QK_HEAD))).reshape(1, MLA_QK_PAD)
    tn_q = _tile(n_heads * MLA_QK_PAD, XS_TILE)
    rope_specs = [pl.BlockSpec((1, LANES), lambda i, j, k: (0, 0))] + \
        [pl.BlockSpec((tm, LANES), lambda i, j, k: (i, 0))] * 2
    q = _matmul(
        lat, w_uq_p, tm=tm, tn=tn_q, tk=q_lora,
        extras=(q_lat_norm.reshape(1, q_lora), qw, first, rc, rs),
        extra_specs=[pl.BlockSpec((1, q_lora), lambda i, j, k: (0, 0)),
                     pl.BlockSpec((1, MLA_QK_PAD), lambda i, j, k: (0, 0))] + rope_specs,
        out_shapes=[jax.ShapeDtypeStruct((m, n_heads * MLA_QK_PAD), BF16)],
        out_specs=[pl.BlockSpec((tm, tn_q), lambda i, j, k: (i, j))],
        epilogue=_epi_mla_q,
        prologue=_pro_rms, name="mla_q_proj")[0]

    assert q_lora % kv_lora == 0 and (q_lora + kv_lora) % LANES == 0
    kvw = QK_NOPE + V_HEAD
    tn_kv = _tile(n_heads * kvw, XS_TILE)
    heads_per_tile = tn_kv // kvw
    kwn = k_norm[:QK_NOPE].reshape(1, QK_NOPE)
    kwp = jnp.concatenate([k_norm[QK_NOPE:], k_norm[QK_NOPE:]]).reshape(1, LANES)
    kpe_block = (q_lora + kv_lora) // LANES
    k, v = _matmul(
        lat, w_ukv.astype(BF16), tm=tm, tn=tn_kv, tk=kv_lora,
        extras=(kv_lat_norm.reshape(1, kv_lora), lat, kwn, kwp, first, rc, rs),
        extra_specs=[pl.BlockSpec((1, kv_lora), lambda i, j, k: (0, 0)),
                     pl.BlockSpec((tm, LANES), lambda i, j, k: (i, kpe_block)),
                     pl.BlockSpec((1, QK_NOPE), lambda i, j, k: (0, 0)),
                     pl.BlockSpec((1, LANES), lambda i, j, k: (0, 0))] + rope_specs,
        out_shapes=[jax.ShapeDtypeStruct((m, n_heads * MLA_QK_PAD), BF16),
                    jax.ShapeDtypeStruct((m, n_heads * V_HEAD), BF16)],
        out_specs=[pl.BlockSpec((tm, heads_per_tile * MLA_QK_PAD), lambda i, j, k: (i, j)),
                   pl.BlockSpec((tm, heads_per_tile * V_HEAD), lambda i, j, k: (i, j))],
        epilogue=_epi_mla_kv,
        prologue=_pro_rms, x_col_block=q_lora // kv_lora, name="mla_kv_proj")
    mixed = _attn_call(
        functools.partial(_mla_kernel, tile=MLA_TILE),
        (q.reshape(b, s, -1), k.reshape(b, s, -1), v.reshape(b, s, -1), g.reshape(b, s, -1)),
        (0, 0, 0, 0), (MLA_QK_PAD, MLA_QK_PAD, V_HEAD, V_HEAD), (mla_bias,), n_heads * V_HEAD, 0,
        n_heads, s,
        [pltpu.VMEM((MLA_TILE, MLA_TILE), F32), pltpu.VMEM((MLA_TILE, MLA_TILE), F32),
         pltpu.VMEM((MLA_TILE, MLA_TILE), BF16), pltpu.VMEM((MLA_TILE, MLA_TILE), BF16),
         pltpu.VMEM((s, V_HEAD), F32), pltpu.VMEM((s, LANES), F32),
         pltpu.VMEM((s, LANES), F32)], "mla_attention")
    return _out_proj(x, [mixed.reshape(m, -1)], w_out, li, gate)


def kernel(x, c, positions, ada_w, ada_b, norm_w, ev_w_in, ev_q_norm, ev_k_norm, ev_w_out, od_w_in,
           od_q_lat_norm, od_kv_lat_norm, od_w_uq, od_w_ukv, od_q_norm, od_k_norm, od_w_out):
    b, s, d = x.shape
    depth = ada_w.shape[0]
    m = b * s

    cos_f, sin_f = _rope_angles(positions, HEAD_DIM)
    cosf = jnp.concatenate([cos_f, cos_f], axis=-1).reshape(m, HEAD_DIM)
    sinf = jnp.concatenate([-sin_f, sin_f], axis=-1).reshape(m, HEAD_DIM)
    cos_m, sin_m = _rope_angles(positions, QK_ROPE)
    z32 = jnp.zeros_like(cos_m)
    z64 = jnp.concatenate([z32, z32], axis=-1)
    rc = jnp.concatenate([cos_m, cos_m, z64], axis=-1).reshape(m, LANES)
    rs = jnp.concatenate([-sin_m, sin_m, z64], axis=-1).reshape(m, LANES)

    sb_u = jnp.asarray(np.tril(np.ones((SB_TILE, SB_TILE), np.float32), -1), BF16)
    dl_bias = jnp.asarray(_dl_band_bias())
    causal = np.tril(np.ones((MLA_TILE, MLA_TILE), bool))
    mla_bias = jnp.asarray(np.where(causal, 0.0, -np.inf).astype(np.float32))

    def split_mod(mod):
        return [mod[:, :, i * d:(i + 1) * d].reshape(mod.shape[0], b, 1, d) for i in range(3)]

    n_mod_blocks = (depth - 1) * (ada_w.shape[2] // _tile(ada_w.shape[2], MOD_SIDE_TN))
    ride_along = depth > 1 and n_mod_blocks <= b * (ev_w_in.shape[-1] // (8 * HEAD_DIM)) * (s // DL_CHUNK)
    shift, scale, gate = split_mod(_modulation(c, ada_w, ada_b, 1 if ride_along else depth))

    for layer in range(depth):
        h = _norm_mod(x, norm_w[layer], scale[layer], shift[layer])
        i = layer // 2
        if layer % 2 == 0:
            side = (_pad_rows(c), ada_w, ada_b, 1) if (ride_along and layer == 0) else None
            x, mod_rest = _even_layer(x, h, gate[layer], i, ev_w_in, ev_q_norm[i], ev_k_norm[i], ev_w_out,
                                      cosf, sinf, sb_u, dl_bias, side)
            if mod_rest is not None:
                shift, scale, gate = (jnp.concatenate([first, rest], axis=0) for first, rest in
                                      zip((shift, scale, gate), split_mod(mod_rest[:, :b])))
        else:
            x = _odd_layer(x, h, gate[layer], i, od_w_in[i], od_q_lat_norm[i], od_kv_lat_norm[i],
                           od_w_uq[i], od_w_ukv[i], od_q_norm[i], od_k_norm[i], od_w_out,
                           rc, rs, mla_bias)
    return x
```

```python
import functools
import math

import numpy as np
import jax
import jax.numpy as jnp
from jax import lax
from jax.experimental import pallas as pl
from jax.experimental.pallas import tpu as pltpu

F32 = jnp.float32
BF16 = jnp.bfloat16

HEAD_DIM = 128
DIL_PATTERNS = ((128, 1), (512, 4), (2048, 16))
QK_NOPE = 128
QK_ROPE = 64
QK_HEAD = QK_NOPE + QK_ROPE
V_HEAD = 128
MLA_QK_PAD = 256
ROPE_THETA = 10000.0
EPS = 1e-6

LANES = 128
SUBLANES = 8
MXU_COLS = 256
VMEM_LIMIT = 61 * 1024 * 1024
EXP_UNDERFLOW = -104.0
LOG2_E = math.log2(math.e)
SB_TILE = 256
SB_SUBTILES = 8
DL_TILE = 128
DL_CHUNK = DL_TILE * max(d for _, d in DIL_PATTERNS)
MLA_TILE = 512
MLA_ROW_BLOCK = 64
MM_TM = 512
MM_TN = 1024
XS_TILE = 1024
LATENT_TM = 512
MOD_TN = 1024
MOD_SIDE_TN = 768
NORM_ROWS = 512
NORM_ROW_BLOCK = 16


def _cparams(sem):
    return pltpu.CompilerParams(dimension_semantics=sem, vmem_limit_bytes=VMEM_LIMIT)


def _tile(n, want):
    if n <= want:
        return n
    t = (want // LANES) * LANES
    while n % t:
        t -= LANES
    return t


def _mod_kernel(c_ref, w_ref, b_ref, o_ref):
    a = c_ref[...]
    a = (a * jax.nn.sigmoid(a)).astype(BF16)
    o_ref[...] = jnp.dot(a, w_ref[...].astype(BF16), preferred_element_type=F32) + b_ref[...]


def _pad_rows(c):
    b = c.shape[0]
    return jnp.pad(c, ((0, -b % SUBLANES), (0, 0)))


def _modulation(c, ada_w, ada_b, depth):
    _, d, n3 = ada_w.shape
    b = c.shape[0]
    c_pad = _pad_rows(c)
    rows = c_pad.shape[0]
    tn = _tile(n3, MOD_TN)
    out = pl.pallas_call(
        _mod_kernel,
        grid=(depth, n3 // tn),
        in_specs=[pl.BlockSpec((rows, d), lambda l, j: (0, 0)),
                  pl.BlockSpec((None, d, tn), lambda l, j: (l, 0, j)),
                  pl.BlockSpec((None, 1, tn), lambda l, j: (l, 0, j))],
        out_specs=pl.BlockSpec((None, rows, tn), lambda l, j: (l, 0, j)),
        out_shape=jax.ShapeDtypeStruct((depth, rows, n3), F32),
        compiler_params=_cparams(("parallel", "parallel")),
        name="modulation",
    )(c_pad, ada_w, ada_b.reshape(-1, 1, n3))
    return out[:, :b]


def _norm_mod_kernel(x_ref, w_ref, scale_ref, shift_ref, o_ref):
    gain = w_ref[...] * (1.0 + scale_ref[...])

    def body(r, _):
        rows = pl.ds(pl.multiple_of(r * NORM_ROW_BLOCK, NORM_ROW_BLOCK), NORM_ROW_BLOCK)
        x = x_ref[rows, :]
        rstd = lax.rsqrt(jnp.mean(x * x, axis=-1, keepdims=True) + EPS)
        o_ref[rows, :] = (x_ref[rows, :] * rstd * gain + shift_ref[...]).astype(o_ref.dtype)
        return 0

    lax.fori_loop(0, x_ref.shape[0] // NORM_ROW_BLOCK, body, 0, unroll=2)


def _norm_mod(x, w, scale, shift):
    b, s, d = x.shape
    ts = _tile(s, NORM_ROWS)
    return pl.pallas_call(
        _norm_mod_kernel,
        grid=(b, s // ts),
        in_specs=[pl.BlockSpec((None, ts, d), lambda bi, i: (bi, i, 0)),
                  pl.BlockSpec((1, d), lambda bi, i: (0, 0)),
                  pl.BlockSpec((None, 1, d), lambda bi, i: (bi, 0, 0)),
                  pl.BlockSpec((None, 1, d), lambda bi, i: (bi, 0, 0))],
        out_specs=pl.BlockSpec((None, ts, d), lambda bi, i: (bi, i, 0)),
        out_shape=jax.ShapeDtypeStruct((b, s, d), BF16),
        compiler_params=_cparams(("parallel", "parallel")),
        name="norm_mod",
    )(x, w.reshape(1, d), scale, shift)


def _run_epilogue(variants, chunk_of, tn, acc_ref):
    chunks = [(c0, min(MXU_COLS, tn - c0)) for c0 in range(0, tn, MXU_COLS)]

    def fused(fn):
        for c0, w in chunks:
            fn(chunk_of(c0, w), c0)

    def matmul_only():
        acc_ref[...] = chunk_of(0, tn)

    def epilogue_only(fn):
        for c0, w in chunks:
            fn(acc_ref.at[:, c0:c0 + w], c0)

    split = [(cond, fn) for cond, fn in variants if getattr(fn, "via_ref", False) and cond is not None]
    if split:
        pl.when(functools.reduce(jnp.logical_or, [cond for cond, _ in split]))(matmul_only)
    for cond, fn in variants:
        if getattr(fn, "via_ref", False):
            if cond is None:
                matmul_only()
                epilogue_only(fn)
            else:
                pl.when(cond)(functools.partial(epilogue_only, fn))
        elif cond is None:
            fused(fn)
        else:
            pl.when(cond)(functools.partial(fused, fn))


def _via_ref(fn):
    fn.via_ref = True
    return fn


def _mm_kernel(*refs, n_extra, n_out, nk, tn, epilogue, prologue):
    x_ref, w_ref = refs[0], refs[1]
    extras = refs[2:2 + n_extra]
    outs = refs[2 + n_extra:2 + n_extra + n_out]
    scratch = refs[2 + n_extra + n_out:]
    acc_ref = scratch[0]
    if prologue is not None:
        h_ref = scratch[1]

        @pl.when(pl.program_id(1) == 0)
        def _():
            h_ref[...] = prologue(x_ref, extras)

        _run_epilogue(epilogue(extras, outs),
                      lambda c0, w: jnp.dot(h_ref[...], w_ref[:, c0:c0 + w], preferred_element_type=F32),
                      tn, acc_ref)
        return
    k = pl.program_id(2)

    @pl.when(k == 0)
    def _():
        acc_ref[...] = jnp.zeros_like(acc_ref)

    acc_ref[...] += jnp.dot(x_ref[...], w_ref[...], preferred_element_type=F32)

    @pl.when(k == nk - 1)
    def _():
        _run_epilogue(epilogue(extras, outs), lambda c0, w: acc_ref[:, c0:c0 + w], tn, acc_ref)


def _matmul(x, w, *, tm, tn, tk, extras, extra_specs, out_shapes, out_specs, epilogue,
            prologue=None, x_col_block=0, name):
    m = x.shape[0]
    kdim, n = w.shape
    nk = kdim // tk
    assert m % tm == 0 and n % tn == 0 and kdim % tk == 0 and x.shape[1] >= (x_col_block + nk) * tk
    scratch = [pltpu.VMEM((tm, tn), F32)]
    if prologue is not None:
        assert nk == 1
        scratch.append(pltpu.VMEM((tm, kdim), BF16))
    kern = functools.partial(_mm_kernel, n_extra=len(extras), n_out=len(out_shapes), nk=nk, tn=tn,
                             epilogue=epilogue, prologue=prologue)
    return pl.pallas_call(
        kern,
        grid=(m // tm, n // tn, nk),
        in_specs=[pl.BlockSpec((tm, tk), lambda i, j, k: (i, x_col_block + k)),
                  pl.BlockSpec((tk, tn), lambda i, j, k: (k, j))] + list(extra_specs),
        out_specs=list(out_specs),
        out_shape=list(out_shapes),
        scratch_shapes=scratch,
        compiler_params=_cparams(("parallel", "arbitrary", "arbitrary")),
        name=name,
    )(x, w, *extras)


def _mm_ws_kernel(*refs, k_parts, n_extra, n_out, tn, cast, epilogue):
    n_x = len(k_parts)
    x_refs = refs[:n_x]
    w_ref = refs[n_x]
    extras = refs[n_x + 1:n_x + 1 + n_extra]
    outs = refs[n_x + 1 + n_extra:n_x + 1 + n_extra + n_out]
    scratch = refs[n_x + 1 + n_extra + n_out:]
    acc_ref = scratch[0]
    if cast:
        wb_ref = scratch[1]

        @pl.when(pl.program_id(1) == 0)
        def _():
            wb_ref[...] = w_ref[...].astype(BF16)
    else:
        wb_ref = w_ref

    def chunk_of(c0, w):
        acc = None
        k0 = 0
        for x_ref, kp in zip(x_refs, k_parts):
            part = jnp.dot(x_ref[...], wb_ref[k0:k0 + kp, c0:c0 + w], preferred_element_type=F32)
            acc = part if acc is None else acc + part
            k0 += kp
        return acc

    _run_epilogue(epilogue(extras, outs), chunk_of, tn, acc_ref)


def _matmul_ws(xs, w, *, tm, tn, extras, extra_specs, out_shapes, out_specs, epilogue, name, layer=None):
    m = xs[0].shape[0]
    kdim, n = w.shape[-2:]
    if layer is None:
        w_spec = pl.BlockSpec((kdim, tn), lambda j, i: (0, j))
    else:
        w_spec = pl.BlockSpec((None, kdim, tn), lambda j, i: (layer, 0, j))
    k_parts = tuple(x.shape[1] for x in xs)
    assert m % tm == 0 and n % tn == 0 and sum(k_parts) == kdim
    cast = w.dtype != BF16
    scratch = [pltpu.VMEM((tm, tn), F32)]
    if cast:
        scratch.append(pltpu.VMEM((kdim, tn), BF16))
    kern = functools.partial(_mm_ws_kernel, k_parts=k_parts, n_extra=len(extras),
                             n_out=len(out_shapes), tn=tn, cast=cast, epilogue=epilogue)
    x_specs = [pl.BlockSpec((tm, kp), lambda j, i: (i, 0)) for kp in k_parts]
    return pl.pallas_call(
        kern,
        grid=(n // tn, m // tm),
        in_specs=x_specs + [w_spec] + list(extra_specs),
        out_specs=list(out_specs),
        out_shape=list(out_shapes),
        scratch_shapes=scratch,
        compiler_params=_cparams(("parallel", "arbitrary")),
        name=name,
    )(*xs, w, *extras)


def _epi_plain(extras, outs):
    o_ref = outs[0]

    def fn(a, c0):
        o_ref[:, c0:c0 + a.shape[1]] = a.astype(o_ref.dtype)

    return [(None, fn)]


def _epi_resid(extras, outs):
    xres_ref, gate_ref = extras
    o_ref = outs[0]

    def fn(a, c0):
        sl = slice(c0, c0 + a.shape[1])
        o_ref[:, sl] = xres_ref[:, sl] + gate_ref[:, sl] * a

    return [(None, fn)]


def _rope_full(y, cos_ref, sin_ref):
    return y * cos_ref[...] + pltpu.roll(y, HEAD_DIM // 2, 1) * sin_ref[...]


def _rope_mla(r, c_ref, s_ref):
    return r * c_ref[...] + pltpu.roll(r, QK_ROPE // 2, 1) * s_ref[...]


def _epi_even(extras, outs, *, part_tiles, q_scale):
    qn_ref, kn_ref, cos_ref, sin_ref = extras
    o_ref = outs[0]
    part = pl.program_id(0) // part_tiles

    def scaled(scale):
        def fn(a, c0):
            y = a if scale == 1.0 else a * scale
            o_ref[:, c0:c0 + a.shape[1]] = y.astype(o_ref.dtype)
        return fn

    def headnorm_rope(w_ref, scale):
        @_via_ref
        def fn(a, c0):
            for hh in range(a.shape[1] // HEAD_DIM):
                ah = a[:, hh * HEAD_DIM:(hh + 1) * HEAD_DIM]
                y = ah * lax.rsqrt(jnp.mean(ah * ah, axis=-1, keepdims=True) + EPS) * w_ref[...]
                y = _rope_full(y, cos_ref, sin_ref)
                if scale != 1.0:
                    y = y * scale
                o_ref[:, c0 + hh * HEAD_DIM:c0 + (hh + 1) * HEAD_DIM] = y.astype(o_ref.dtype)
        return fn

    return [(part == 0, scaled(q_scale)),
            (part == 4, headnorm_rope(qn_ref, q_scale * LOG2_E)),
            (part == 5, headnorm_rope(kn_ref, 1.0)),
            ((part != 0) & (part != 4) & (part != 5), scaled(1.0))]


def _pro_rms(x_ref, extras):
    x = x_ref[...]
    w_ref = extras[0]
    y = x * lax.rsqrt(jnp.mean(x * x, axis=-1, keepdims=True) + EPS) * w_ref[...]
    return y.astype(BF16)


def _epi_mla_q(extras, outs):
    _, qw_ref, first_ref, c_ref, s_ref = extras
    o_ref = outs[0]
    assert MXU_COLS == MLA_QK_PAD

    @_via_ref
    def fn(a, c0):
        nope, rope = a[:, :QK_NOPE], a[:, QK_NOPE:]
        sq = nope * nope + rope * rope * first_ref[...]
        rstd = lax.rsqrt(jnp.sum(sq, axis=-1, keepdims=True) * (1.0 / QK_HEAD) + EPS)
        o_ref[:, c0:c0 + QK_NOPE] = (nope * rstd * qw_ref[:, :QK_NOPE]).astype(o_ref.dtype)
        r = _rope_mla(rope * rstd * qw_ref[:, QK_NOPE:], c_ref, s_ref)
        o_ref[:, c0 + QK_NOPE:c0 + MLA_QK_PAD] = r.astype(o_ref.dtype)

    return [(None, fn)]


def _epi_mla_kv(extras, outs):
    _, kpe_ref, kwn_ref, kwp_ref, first_ref, c_ref, s_ref = extras
    k_ref, v_ref = outs
    assert MXU_COLS == QK_NOPE + V_HEAD
    kpe = kpe_ref[...]
    ss_pe = jnp.sum(kpe * kpe * first_ref[...], axis=-1, keepdims=True)
    kr = _rope_mla(kpe * kwp_ref[...], c_ref, s_ref)

    @_via_ref
    def fn(a, c0):
        head = c0 // (QK_NOPE + V_HEAD)
        kn = a[:, :QK_NOPE]
        rstd = lax.rsqrt((jnp.sum(kn * kn, axis=-1, keepdims=True) + ss_pe) * (1.0 / QK_HEAD) + EPS)
        kb = head * MLA_QK_PAD
        k_ref[:, kb:kb + QK_NOPE] = (kn * rstd * kwn_ref[...]).astype(k_ref.dtype)
        k_ref[:, kb + QK_NOPE:kb + MLA_QK_PAD] = (kr * rstd).astype(k_ref.dtype)
        v_ref[:, head * V_HEAD:(head + 1) * V_HEAD] = a[:, QK_NOPE:].astype(v_ref.dtype)

    return [(None, fn)]


def _qk(q, k):
    return lax.dot_general(q, k, (((1,), (1,)), ((), ())), preferred_element_type=F32)


def _sb_stages(q_ref, k_ref, v_ref, g_ref, u_ref, o_ref, acc_ref, carry_ref, qi, tile):
    n_sub = q_ref.shape[0] // tile

    def tile_rows(kb):
        return pl.ds(pl.multiple_of(kb * tile, tile), tile)

    def weights(q, kb, carry, diagonal=False, valid=None):
        z = _qk(q, k_ref[tile_rows(kb), :])
        sp = jnp.log(1.0 + jnp.exp(-jnp.abs(z)))
        log_beta = jnp.minimum(z, 0.0) - sp
        log_keep = jnp.minimum(-z, 0.0) - sp
        keep = valid
        if diagonal:
            row = lax.broadcasted_iota(jnp.int32, z.shape, 0)
            col = lax.broadcasted_iota(jnp.int32, z.shape, 1)
            keep = col < row
        if keep is not None:
            log_keep = jnp.where(keep, log_keep, 0.0)
        hi = log_keep.astype(BF16)
        lo = (log_keep - hi.astype(F32)).astype(BF16)
        u = u_ref[...]
        excl = jnp.dot(hi, u, preferred_element_type=F32) + jnp.dot(lo, u, preferred_element_type=F32)
        if carry is not None:
            excl = excl + carry
        w = jnp.exp(log_beta + excl)
        if keep is not None:
            w = jnp.where(keep, w, 0.0)
        return w.astype(BF16), jnp.sum(log_keep, axis=-1, keepdims=True)

    def pv(w, kb):
        return jnp.dot(w, v_ref[tile_rows(kb), :], preferred_element_type=F32)

    def front():
        tails = []
        for sub in range(n_sub):
            rows = slice(sub * tile, (sub + 1) * tile)
            q = q_ref[rows, :]
            diag = qi * n_sub + sub
            prev = jnp.maximum(diag - 1, 0)
            w_d, sum_d = weights(q, diag, None, diagonal=True)
            w_p, sum_p = weights(q, prev, sum_d, valid=diag > 0)
            carry = sum_d + sum_p
            acc_ref[rows, :] = pv(w_d, diag) + pv(w_p, prev)
            carry_ref[rows, :] = carry
            tails.append((rows, diag - 2, jnp.max(carry)))
        return tails

    def tail(tails):
        for rows, kb0, cmax0 in tails:
            def cond(state):
                kb, cmax = state
                return (kb >= 0) & (cmax > EXP_UNDERFLOW)

            def body(state, rows=rows):
                kb, _ = state
                carry = carry_ref[rows, :]
                w, sum_k = weights(q_ref[rows, :], kb, carry)
                acc_ref[rows, :] += pv(w, kb)
                carry = carry + sum_k
                carry_ref[rows, :] = carry
                return kb - 1, jnp.max(carry)

            lax.while_loop(cond, body, (kb0, cmax0))

    def finish():
        g = g_ref[...].astype(F32)
        o_ref[...] = (acc_ref[...] * (g * jax.nn.sigmoid(g))).astype(o_ref.dtype)

    return front, tail, finish


def _dl_stages(q_ref, k_ref, v_ref, g_ref, bias_ref, o_ref, qf_ref, kf_ref, vf_ref, acc_ref, m_ref, l_ref,
               ci, chunk):
    def prepare():
        @pl.when(ci == 0)
        def _():
            kf_ref[...] = k_ref[...].astype(F32)
            vf_ref[...] = v_ref[...].astype(F32)

    base = ci * chunk

    def body(idx, _):
        for pi, (_, dil) in enumerate(DIL_PATTERNS):
            unit = DL_TILE * dil
            q0 = (idx // dil) * unit + idx % dil
            rows = pl.ds(q0, DL_TILE, stride=dil)
            cur0 = base + q0
            has_prev = cur0 >= unit
            prev0 = jnp.maximum(cur0 - unit, 0)
            prev_rows = pl.ds(prev0, DL_TILE, stride=dil)
            cur_rows = pl.ds(cur0, DL_TILE, stride=dil)
            kk = jnp.concatenate([kf_ref[prev_rows, :], kf_ref[cur_rows, :]], axis=0).astype(BF16)
            vv = jnp.concatenate([vf_ref[prev_rows, :], vf_ref[cur_rows, :]], axis=0).astype(BF16)
            s = _qk(qf_ref[rows, :].astype(BF16), kk) + bias_ref[pi, has_prev.astype(jnp.int32)]
            m = jnp.max(s, axis=-1, keepdims=True)
            p = jnp.exp2(s - m)
            l = jnp.sum(p, axis=-1, keepdims=True)
            acc_ref[pi, rows, :] = jnp.dot(p.astype(BF16), vv, preferred_element_type=F32)
            m_ref[pi, rows, :] = jnp.broadcast_to(m, (DL_TILE, LANES))
            l_ref[pi, rows, :] = jnp.broadcast_to(l, (DL_TILE, LANES))
        return 0

    def front():
        qf_ref[...] = q_ref[...].astype(F32)
        lax.fori_loop(0, chunk // DL_TILE, body, 0, unroll=True)

    def finish():
        n_pat = len(DIL_PATTERNS)
        m_all = functools.reduce(jnp.maximum, [m_ref[pi] for pi in range(n_pat)])
        num = jnp.zeros(acc_ref.shape[1:], F32)
        den = jnp.zeros(acc_ref.shape[1:], F32)
        for pi in range(n_pat):
            w = jnp.exp2(m_ref[pi] - m_all)
            num = num + w * acc_ref[pi]
            den = den + w * l_ref[pi]
        g = g_ref[...].astype(F32)
        o_ref[...] = (num / den * (g * jax.nn.sigmoid(g))).astype(o_ref.dtype)

    return prepare, front, finish


def _even_attn_kernel(sq_ref, sk_ref, sv_ref, sg_ref, dq_ref, dk_ref, dv_ref, dg_ref, u_ref, bias_ref,
                      *rest, sb_tile, chunk, with_mod):
    if with_mod:
        c_ref, adaw_ref, adab_ref, o_sb_ref, o_dl_ref, mod_ref = rest[:6]
        rest = rest[6:]
    else:
        o_sb_ref, o_dl_ref = rest[:2]
        rest = rest[2:]
    sb_acc_ref, sb_carry_ref, qf_ref, kf_ref, vf_ref, dl_acc_ref, dl_m_ref, dl_l_ref = rest
    ci = pl.program_id(2)
    sb_front, sb_tail, sb_finish = _sb_stages(sq_ref, sk_ref, sv_ref, sg_ref, u_ref, o_sb_ref, sb_acc_ref,
                                              sb_carry_ref, ci, sb_tile)
    dl_prepare, dl_front, dl_finish = _dl_stages(dq_ref, dk_ref, dv_ref, dg_ref, bias_ref, o_dl_ref, qf_ref,
                                                 kf_ref, vf_ref, dl_acc_ref, dl_m_ref, dl_l_ref, ci, chunk)
    dl_prepare()
    if with_mod:
        _mod_kernel(c_ref, adaw_ref, adab_ref, mod_ref)
    tails = sb_front()
    dl_front()
    sb_tail(tails)
    sb_finish()
    dl_finish()


def _softmax_update(s_ref, bias_ref, v, p_ref, acc_ref, m_ref, l_ref):
    tq, tk = s_ref.shape
    for r0 in range(0, tq, MLA_ROW_BLOCK):
        rs = slice(r0, r0 + MLA_ROW_BLOCK)
        cols = [s_ref[rs, c * LANES:(c + 1) * LANES] for c in range(tk // LANES)]
        if bias_ref is not None:
            cols = [col + bias_ref[rs, c * LANES:(c + 1) * LANES] for c, col in enumerate(cols)]
        m_old = m_ref[rs, :]
        tile_max = jnp.max(functools.reduce(jnp.maximum, cols), axis=-1, keepdims=True)
        m_new = jnp.maximum(m_old, tile_max)
        ps = [jnp.exp2(col - m_new) for col in cols]
        alpha = jnp.exp2(m_old - m_new)
        l_ref[rs, :] = alpha * l_ref[rs, :] + jnp.sum(functools.reduce(jnp.add, ps), axis=-1, keepdims=True)
        m_ref[rs, :] = m_new
        acc_ref[rs, :] = alpha * acc_ref[rs, :]
        for c, x in enumerate(ps):
            p_ref[rs, c * LANES:(c + 1) * LANES] = x.astype(BF16)
    acc_ref[...] += jnp.dot(p_ref[...], v, preferred_element_type=F32)


def _mla_kernel(q_ref, k_ref, v_ref, g_ref, bias_ref, o_ref, sa_ref, sb_ref, pa_ref, pb_ref, acc_ref, m_ref,
                l_ref, *, tile):
    n_q = q_ref.shape[0] // tile
    acc_ref[...] = jnp.zeros_like(acc_ref)
    m_ref[...] = jnp.full_like(m_ref, -jnp.inf)
    l_ref[...] = jnp.zeros_like(l_ref)

    def rows(t):
        return pl.ds(pl.multiple_of(t * tile, tile), tile)

    def scores(qi, kb, dst_ref):
        dst_ref[...] = _qk(q_ref[rows(qi), :], k_ref[rows(kb), :])

    def pipeline(n_steps, start, advance, bias):
        def update(qi, kb, src_ref, p_ref):
            r = rows(qi)
            _softmax_update(src_ref, bias, v_ref[rows(kb), :], p_ref, acc_ref.at[r, :], m_ref.at[r, :],
                            l_ref.at[r, :])

        scores(*start, sa_ref)

        def pair(_, state):
            qa, ka = state
            qb, kb = advance(qa, ka)
            scores(qb, kb, sb_ref)
            update(qa, ka, sa_ref, pa_ref)
            qa, ka = advance(qb, kb)
            scores(qa, ka, sa_ref)
            update(qb, kb, sb_ref, pb_ref)
            return qa, ka

        qa, ka = lax.fori_loop(0, n_steps // 2, pair, start, unroll=4)
        if n_steps % 2:
            update(qa, ka, sa_ref, pa_ref)

    def next_lower(qi, kb):
        wrap = kb == qi - 1
        return jnp.where(wrap, jnp.minimum(qi + 1, n_q - 1), qi), jnp.where(wrap, 0, kb + 1)

    def next_diagonal(qi, kb):
        t = jnp.minimum(qi + 1, n_q - 1)
        return t, t

    zero, one = jnp.int32(0), jnp.int32(1)
    if n_q > 1:
        pipeline(n_q * (n_q - 1) // 2, (one, zero), next_lower, None)
    pipeline(n_q, (zero, zero), next_diagonal, bias_ref)

    g = g_ref[...].astype(F32)
    o_ref[...] = (acc_ref[...] * g / (l_ref[...] * (1.0 + jnp.exp(-g)))).astype(o_ref.dtype)


def _attn_call(kern, arrays, col_blocks, widths, consts, out_width_total, out_col0, n_heads, tile,
               scratch, name):
    q, k, v, g = arrays
    b, s, _ = q.shape
    cq, ck, cv, cg = col_blocks
    wq, wk, wv, wg = widths
    in_specs = [
        pl.BlockSpec((None, tile, wq), lambda bi, h, i: (bi, i, cq + h)),
        pl.BlockSpec((None, s, wk), lambda bi, h, i: (bi, 0, ck + h)),
        pl.BlockSpec((None, s, wv), lambda bi, h, i: (bi, 0, cv + h)),
        pl.BlockSpec((None, tile, wg), lambda bi, h, i: (bi, i, cg + h)),
    ]
    for cst in consts:
        in_specs.append(pl.BlockSpec(cst.shape, lambda bi, h, i, nd=cst.ndim: (0,) * nd))
    return pl.pallas_call(
        kern,
        grid=(b, n_heads, s // tile),
        in_specs=in_specs,
        out_specs=pl.BlockSpec((None, tile, wv), lambda bi, h, i: (bi, i, out_col0 + h)),
        out_shape=jax.ShapeDtypeStruct((b, s, out_width_total), BF16),
        scratch_shapes=scratch,
        compiler_params=_cparams(("parallel", "parallel", "arbitrary")),
        name=name,
    )(q, k, v, g, *consts)


def _dl_band_bias():
    i = np.arange(DL_TILE)[:, None]
    c = np.arange(2 * DL_TILE)[None, :]
    out = []
    for window, dil in DIL_PATTERNS:
        span = window // dil
        assert span <= DL_TILE
        dist = i + DL_TILE - c
        band = (dist >= 0) & (dist <= span)
        out.append(np.stack([band & (c >= DL_TILE), band]))
    return np.where(np.stack(out), 0.0, -np.inf).astype(np.float32)


def _rope_angles(positions, dim):
    inv = ROPE_THETA ** (-jnp.arange(0, dim, 2, dtype=F32) / dim)
    ang = positions.astype(F32)[..., None] * inv
    return jnp.cos(ang), jnp.sin(ang)


def _even_layer(x, h, gate, li, w_in, qn, kn, w_out, cosf, sinf, sb_u, dl_bias, mod_side=None):
    b, s, d = x.shape
    m = b * s
    n_in = w_in.shape[-1]
    w_part = n_in // 8
    n_heads = w_part // HEAD_DIM
    tm, tn = _tile(s, MM_TM), _tile(w_part, MM_TN)
    q_scale = 1.0 / math.sqrt(HEAD_DIM)
    proj = _matmul_ws(
        [h.reshape(m, d)], w_in, layer=li, tm=tm, tn=tn,
        extras=(qn.reshape(1, HEAD_DIM), kn.reshape(1, HEAD_DIM), cosf, sinf),
        extra_specs=[pl.BlockSpec((1, HEAD_DIM), lambda j, i: (0, 0)),
                     pl.BlockSpec((1, HEAD_DIM), lambda j, i: (0, 0)),
                     pl.BlockSpec((tm, HEAD_DIM), lambda j, i: (i, 0)),
                     pl.BlockSpec((tm, HEAD_DIM), lambda j, i: (i, 0))],
        out_shapes=[jax.ShapeDtypeStruct((m, n_in), BF16)],
        out_specs=[pl.BlockSpec((tm, tn), lambda j, i: (i, j))],
        epilogue=functools.partial(_epi_even, part_tiles=w_part // tn, q_scale=q_scale),
        name="even_in_proj")[0].reshape(b, s, n_in)

    assert s % DL_CHUNK == 0 and SB_TILE * SB_SUBTILES == DL_CHUNK
    n_pat = len(DIL_PATTERNS)

    def head_block(part, rows):
        if rows == s:
            return pl.BlockSpec((None, s, HEAD_DIM), lambda bi, h, i: (bi, 0, part * n_heads + h))
        return pl.BlockSpec((None, rows, HEAD_DIM), lambda bi, h, i: (bi, i, part * n_heads + h))

    part_rows = (DL_CHUNK, s, s, DL_CHUNK) * 2
    out_spec = pl.BlockSpec((None, DL_CHUNK, HEAD_DIM), lambda bi, h, i: (bi, i, h))
    n_chunks = s // DL_CHUNK
    inputs = [proj] * 8 + [sb_u, dl_bias]
    in_specs = [head_block(p, r) for p, r in enumerate(part_rows)] + [
        pl.BlockSpec(sb_u.shape, lambda bi, h, i: (0, 0)),
        pl.BlockSpec(dl_bias.shape, lambda bi, h, i: (0, 0, 0, 0))]
    out_specs = [out_spec, out_spec]
    out_shapes = [jax.ShapeDtypeStruct((b, s, w_part), BF16)] * 2
    if mod_side is not None:
        c_pad, ada_w, ada_b, first_layer = mod_side
        n_layers, _, n3 = ada_w.shape
        tn_mod = _tile(n3, MOD_SIDE_TN)
        per_layer = n3 // tn_mod
        n_blocks = (n_layers - first_layer) * per_layer
        assert n_blocks <= b * n_heads * n_chunks

        def mod_block(bi, h, i):
            blk = jnp.minimum((bi * n_heads + h) * n_chunks + i, n_blocks - 1)
            return blk // per_layer, blk % per_layer

        inputs += [c_pad, ada_w, ada_b.reshape(n_layers, 1, n3)]
        in_specs += [
            pl.BlockSpec(c_pad.shape, lambda bi, h, i: (0, 0)),
            pl.BlockSpec((None, d, tn_mod), lambda bi, h, i: (first_layer + mod_block(bi, h, i)[0], 0,
                                                              mod_block(bi, h, i)[1])),
            pl.BlockSpec((None, 1, tn_mod), lambda bi, h, i: (first_layer + mod_block(bi, h, i)[0], 0,
                                                              mod_block(bi, h, i)[1]))]
        out_specs.append(pl.BlockSpec((None, c_pad.shape[0], tn_mod),
                                      lambda bi, h, i: (mod_block(bi, h, i)[0], 0, mod_block(bi, h, i)[1])))
        out_shapes.append(jax.ShapeDtypeStruct((n_layers - first_layer, c_pad.shape[0], n3), F32))
    outs = pl.pallas_call(
        functools.partial(_even_attn_kernel, sb_tile=SB_TILE, chunk=DL_CHUNK, with_mod=mod_side is not None),
        grid=(b, n_heads, n_chunks),
        in_specs=in_specs,
        out_specs=out_specs,
        out_shape=out_shapes,
        scratch_shapes=[pltpu.VMEM((DL_CHUNK, HEAD_DIM), F32), pltpu.VMEM((DL_CHUNK, 1), F32),
                        pltpu.VMEM((DL_CHUNK, HEAD_DIM), F32), pltpu.VMEM((s, HEAD_DIM), F32),
                        pltpu.VMEM((s, HEAD_DIM), F32), pltpu.VMEM((n_pat, DL_CHUNK, HEAD_DIM), F32),
                        pltpu.VMEM((n_pat, DL_CHUNK, LANES), F32), pltpu.VMEM((n_pat, DL_CHUNK, LANES), F32)],
        compiler_params=_cparams(("arbitrary",) * 3 if mod_side is not None
                                 else ("parallel", "parallel", "arbitrary")),
        name="even_attention",
    )(*inputs)
    o_sb, o_dl = outs[:2]
    mod_rest = outs[2] if mod_side is not None else None
    return _out_proj(x, [o_sb.reshape(m, w_part), o_dl.reshape(m, w_part)], w_out, li, gate), mod_rest


def _out_proj(x, mixed_parts, w_out, li, gate):
    b, s, d = x.shape
    m = b * s
    tm, tn = _tile(s, MM_TM), _tile(d, MM_TN)
    rows_per_batch = s // tm
    return _matmul_ws(
        mixed_parts, w_out, layer=li, tm=tm, tn=tn,
        extras=(x.reshape(m, d), gate),
        extra_specs=[pl.BlockSpec((tm, tn), lambda j, i: (i, j)),
                     pl.BlockSpec((None, 1, tn), lambda j, i: (i // rows_per_batch, 0, j))],
        out_shapes=[jax.ShapeDtypeStruct((m, d), F32)],
        out_specs=[pl.BlockSpec((tm, tn), lambda j, i: (i, j))],
        epilogue=_epi_resid, name="out_proj")[0].reshape(b, s, d)


def _odd_layer(x, h, gate, li, w_in, q_lat_norm, kv_lat_norm, w_uq, w_ukv, q_norm, k_norm, w_out,
               rc, rs, mla_bias):
    b, s, d = x.shape
    m = b * s
    q_lora = q_lat_norm.shape[0]
    kv_lora = kv_lat_norm.shape[0]
    n_heads = w_uq.shape[1] // QK_HEAD
    lat_raw = q_lora + kv_lora + QK_ROPE
    n_lat = lat_raw + QK_ROPE
    assert n_lat % LANES == 0 and 2 * QK_ROPE == LANES
    h2 = h.reshape(m, d)
    tm = _tile(s, XS_TILE)
    first = (jnp.arange(LANES) < QK_ROPE).astype(F32).reshape(1, LANES)

    w_lat = jnp.concatenate([w_in[:, :lat_raw], w_in[:, lat_raw - QK_ROPE:lat_raw]], axis=1).astype(BF16)
    tm_lat = _tile(s, LATENT_TM)
    lat = _matmul_ws([h2], w_lat, tm=tm_lat, tn=n_lat, extras=(), extra_specs=[],
                     out_shapes=[jax.ShapeDtypeStruct((m, n_lat), F32)],
                     out_specs=[pl.BlockSpec((tm_lat, n_lat), lambda j, i: (i, j))],
                     epilogue=_epi_plain, name="odd_latent_proj")[0]
    n_g = w_in.shape[1] - lat_raw
    tn_g = _tile(n_g, MM_TN)
    g = _matmul_ws([h2], w_in[:, lat_raw:].astype(BF16), tm=tm, tn=tn_g, extras=(), extra_specs=[],
                   out_shapes=[jax.ShapeDtypeStruct((m, n_g), BF16)],
                   out_specs=[pl.BlockSpec((tm, tn_g), lambda j, i: (i, j))],
                   epilogue=_epi_plain, name="odd_gate_proj")[0]

    w_uq3 = w_uq.reshape(q_lora, n_heads, QK_HEAD)
    w_uq_p = jnp.concatenate([w_uq3, w_uq3[:, :, QK_NOPE:]], axis=-1)
    w_uq_p = w_uq_p.reshape(q_lora, n_heads * MLA_QK_PAD).astype(BF16)
    qw = (jnp.concatenate([q_norm, q_norm[QK_NOPE:]]) * (LOG2_E / math.sqrt(QK_HEAD))).reshape(1, MLA_QK_PAD)
    tn_q = _tile(n_heads * MLA_QK_PAD, XS_TILE)
    rope_specs = [pl.BlockSpec((1, LANES), lambda i, j, k: (0, 0))] + \
        [pl.BlockSpec((tm, LANES), lambda i, j, k: (i, 0))] * 2
    q = _matmul(
        lat, w_uq_p, tm=tm, tn=tn_q, tk=q_lora,
        extras=(q_lat_norm.reshape(1, q_lora), qw, first, rc, rs),
        extra_specs=[pl.BlockSpec((1, q_lora), lambda i, j, k: (0, 0)),
                     pl.BlockSpec((1, MLA_QK_PAD), lambda i, j, k: (0, 0))] + rope_specs,
        out_shapes=[jax.ShapeDtypeStruct((m, n_heads * MLA_QK_PAD), BF16)],
        out_specs=[pl.BlockSpec((tm, tn_q), lambda i, j, k: (i, j))],
        epilogue=_epi_mla_q,
        prologue=_pro_rms, name="mla_q_proj")[0]

    assert q_lora % kv_lora == 0 and (q_lora + kv_lora) % LANES == 0
    kvw = QK_NOPE + V_HEAD
    tn_kv = _tile(n_heads * kvw, XS_TILE)
    heads_per_tile = tn_kv // kvw
    kwn = k_norm[:QK_NOPE].reshape(1, QK_NOPE)
    kwp = jnp.concatenate([k_norm[QK_NOPE:], k_norm[QK_NOPE:]]).reshape(1, LANES)
    kpe_block = (q_lora + kv_lora) // LANES
    k, v = _matmul(
        lat, w_ukv.astype(BF16), tm=tm, tn=tn_kv, tk=kv_lora,
        extras=(kv_lat_norm.reshape(1, kv_lora), lat, kwn, kwp, first, rc, rs),
        extra_specs=[pl.BlockSpec((1, kv_lora), lambda i, j, k: (0, 0)),
                     pl.BlockSpec((tm, LANES), lambda i, j, k: (i, kpe_block)),
                     pl.BlockSpec((1, QK_NOPE), lambda i, j, k: (0, 0)),
                     pl.BlockSpec((1, LANES), lambda i, j, k: (0, 0))] + rope_specs,
        out_shapes=[jax.ShapeDtypeStruct((m, n_heads * MLA_QK_PAD), BF16),
                    jax.ShapeDtypeStruct((m, n_heads * V_HEAD), BF16)],
        out_specs=[pl.BlockSpec((tm, heads_per_tile * MLA_QK_PAD), lambda i, j, k: (i, j)),
                   pl.BlockSpec((tm, heads_per_tile * V_HEAD), lambda i, j, k: (i, j))],
        epilogue=_epi_mla_kv,
        prologue=_pro_rms, x_col_block=q_lora // kv_lora, name="mla_kv_proj")
    mixed = _attn_call(
        functools.partial(_mla_kernel, tile=MLA_TILE),
        (q.reshape(b, s, -1), k.reshape(b, s, -1), v.reshape(b, s, -1), g.reshape(b, s, -1)),
        (0, 0, 0, 0), (MLA_QK_PAD, MLA_QK_PAD, V_HEAD, V_HEAD), (mla_bias,), n_heads * V_HEAD, 0,
        n_heads, s,
        [pltpu.VMEM((MLA_TILE, MLA_TILE), F32), pltpu.VMEM((MLA_TILE, MLA_TILE), F32),
         pltpu.VMEM((MLA_TILE, MLA_TILE), BF16), pltpu.VMEM((MLA_TILE, MLA_TILE), BF16),
         pltpu.VMEM((s, V_HEAD), F32), pltpu.VMEM((s, LANES), F32),
         pltpu.VMEM((s, LANES), F32)], "mla_attention")
    return _out_proj(x, [mixed.reshape(m, -1)], w_out, li, gate)


def kernel(x, c, positions, ada_w, ada_b, norm_w, ev_w_in, ev_q_norm, ev_k_norm, ev_w_out, od_w_in,
           od_q_lat_norm, od_kv_lat_norm, od_w_uq, od_w_ukv, od_q_norm, od_k_norm, od_w_out):
    b, s, d = x.shape
    depth = ada_w.shape[0]
    m = b * s

    cos_f, sin_f = _rope_angles(positions, HEAD_DIM)
    cosf = jnp.concatenate([cos_f, cos_f], axis=-1).reshape(m, HEAD_DIM)
    sinf = jnp.concatenate([-sin_f, sin_f], axis=-1).reshape(m, HEAD_DIM)
    cos_m, sin_m = _rope_angles(positions, QK_ROPE)
    z32 = jnp.zeros_like(cos_m)
    z64 = jnp.concatenate([z32, z32], axis=-1)
    rc = jnp.concatenate([cos_m, cos_m, z64], axis=-1).reshape(m, LANES)
    rs = jnp.concatenate([-sin_m, sin_m, z64], axis=-1).reshape(m, LANES)

    sb_u = jnp.asarray(np.tril(np.ones((SB_TILE, SB_TILE), np.float32), -1), BF16)
    dl_bias = jnp.asarray(_dl_band_bias())
    causal = np.tril(np.ones((MLA_TILE, MLA_TILE), bool))
    mla_bias = jnp.asarray(np.where(causal, 0.0, -np.inf).astype(np.float32))

    def split_mod(mod):
        return [mod[:, :, i * d:(i + 1) * d].reshape(mod.shape[0], b, 1, d) for i in range(3)]

    n_mod_blocks = (depth - 1) * (ada_w.shape[2] // _tile(ada_w.shape[2], MOD_SIDE_TN))
    ride_along = depth > 1 and n_mod_blocks <= b * (ev_w_in.shape[-1] // (8 * HEAD_DIM)) * (s // DL_CHUNK)
    shift, scale, gate = split_mod(_modulation(c, ada_w, ada_b, 1 if ride_along else depth))

    for layer in range(depth):
        h = _norm_mod(x, norm_w[layer], scale[layer], shift[layer])
        i = layer // 2
        if layer % 2 == 0:
            side = (_pad_rows(c), ada_w, ada_b, 1) if (ride_along and layer == 0) else None
            x, mod_rest = _even_layer(x, h, gate[layer], i, ev_w_in, ev_q_norm[i], ev_k_norm[i], ev_w_out,
                                      cosf, sinf, sb_u, dl_bias, side)
            if mod_rest is not None:
                shift, scale, gate = (jnp.concatenate([first, rest], axis=0) for first, rest in
                                      zip((shift, scale, gate), split_mod(mod_rest[:, :b])))
        else:
            x = _odd_layer(x, h, gate[layer], i, od_w_in[i], od_q_lat_norm[i], od_kv_lat_norm[i],
                           od_w_uq[i], od_w_ukv[i], od_q_norm[i], od_k_norm[i], od_w_out,
                           rc, rs, mla_bias)
    return x
```
